```python
import math
import jax
import jax.numpy as jnp
from jax import lax
import numpy as np

D_MODEL = 1024
BATCH = 2
SEQ = 8192
DEPTH = 4
DEC_BATCH = 4
DEC_SEQ = 8192
PAST_LEN = 128

HEAD_DIM = 64
DA_HEADS = 4
DA_VDIM = 2 * HEAD_DIM
CONV_CH = 512
CONV_WIDTH = 31
WA_HEADS = 8
WA_KV_HEADS = 2
WA_GROUP = WA_HEADS // WA_KV_HEADS
WINDOW = 128
BLOCK = 128
Q_BLOCK = 128
MEM_TOKENS = 256
MA_HEADS = 4
MA_HEAD_DIM = 128
BRANCH_WIDTH = 512
N_BRANCHES = 4
D_FF = 4 * D_MODEL
REL_BUCKETS = 32
REL_MAX_DIST = 128
REL_HEADS = 2 * DA_HEADS + WA_HEADS
EPS = 1e-6
NEG_INF = -1e30
SPLIT_SIZES = (DA_HEADS * 2 * HEAD_DIM, DA_HEADS * 2 * HEAD_DIM, DA_HEADS * DA_VDIM, 2 * CONV_CH, WA_HEADS * HEAD_DIM, WA_KV_HEADS * HEAD_DIM, WA_KV_HEADS * HEAD_DIM, MA_HEADS * MA_HEAD_DIM, N_BRANCHES * D_MODEL)
IN_WIDTH = sum(SPLIT_SIZES)

kernel_name = 'hybrid_gated_bidir_encoder'


def rms_norm(x, g):
    xf = x.astype(jnp.float32)
    y = xf * lax.rsqrt(jnp.mean(xf * xf, axis=-1, keepdims=True) + EPS)
    return (y * g.astype(jnp.float32)).astype(x.dtype)


def layer_norm(x, g, b):
    xf = x.astype(jnp.float32)
    mu = jnp.mean(xf, axis=-1, keepdims=True)
    xc = xf - mu
    var = jnp.mean(xc * xc, axis=-1, keepdims=True)
    return (xc * lax.rsqrt(var + EPS) * g.astype(jnp.float32) + b.astype(jnp.float32)).astype(x.dtype)


def rel_bucket(rel):
    nb = REL_BUCKETS // 2
    max_exact = nb // 2
    ret = jnp.where(rel > 0, nb, 0)
    n = jnp.abs(rel)
    nf = jnp.maximum(n, 1).astype(jnp.float32)
    large = max_exact + (jnp.log(nf / max_exact) / math.log(REL_MAX_DIST / max_exact) * (nb - max_exact)).astype(jnp.int32)
    large = jnp.minimum(large, nb - 1)
    return ret + jnp.where(n < max_exact, n, large)


def diff_attention(q, k, v, qk_g, lam_params, lam_init, subln_g, rel_table):
    B, S = q.shape[0], q.shape[1]
    q = rms_norm(q, qk_g[0])
    k = rms_norm(k, qk_g[1])
    lp = lam_params.astype(jnp.float32)
    lam = jnp.exp(jnp.sum(lp[0] * lp[1])) - jnp.exp(jnp.sum(lp[2] * lp[3])) + lam_init
    nq = S // Q_BLOCK
    qb = q.reshape(B, nq, Q_BLOCK, DA_HEADS, 2, HEAD_DIM).transpose(1, 0, 2, 3, 4, 5)
    kpos = jnp.arange(S)
    scale = HEAD_DIM ** -0.5

    def one_block(args):
        qi, i = args
        s = jnp.einsum('bqhmd,bkhmd->bhmqk', qi, k, preferred_element_type=jnp.float32) * scale
        qpos = i * Q_BLOCK + jnp.arange(Q_BLOCK)
        bias = rel_table[rel_bucket(kpos[None, :] - qpos[:, None])]
        bias = bias.reshape(Q_BLOCK, S, DA_HEADS, 2).transpose(2, 3, 0, 1).astype(jnp.float32)
        p = jax.nn.softmax(s + bias, axis=-1)
        pd = p[:, :, 0] - lam * p[:, :, 1]
        return jnp.einsum('bhqk,bkhe->bqhe', pd.astype(v.dtype), v)

    o = lax.map(one_block, (qb, jnp.arange(nq)))
    o = o.transpose(1, 0, 2, 3, 4).reshape(B, S, DA_HEADS, DA_VDIM)
    o = rms_norm(o, subln_g) * (1.0 - lam_init)
    return o.reshape(B, S, DA_HEADS * DA_VDIM)


def conv_module(u, conv_w, conv_b, ln_g, ln_b):
    a, g = jnp.split(u, 2, axis=-1)
    z = a * jax.nn.sigmoid(g)
    pad = CONV_WIDTH // 2
    z = lax.conv_general_dilated(z, conv_w[:, None, :].astype(z.dtype), window_strides=(1,), padding=[(pad, pad)], dimension_numbers=('NWC', 'WIO', 'NWC'), feature_group_count=CONV_CH) + conv_b
    z = layer_norm(z, ln_g, ln_b)
    return jax.nn.silu(z)


def window_attention(q, k, v, qk_g, sink, rel_table):
    B, S = q.shape[0], q.shape[1]
    q = rms_norm(q, qk_g[0])
    k = rms_norm(k, qk_g[1])
    n = S // BLOCK
    qb = q.reshape(B, n, BLOCK, WA_KV_HEADS, WA_GROUP, HEAD_DIM)

    def band(t):
        t = t.reshape(B, n, BLOCK, WA_KV_HEADS, HEAD_DIM)
        tp = jnp.pad(t, ((0, 0), (1, 1), (0, 0), (0, 0), (0, 0)))
        return jnp.concatenate([tp[:, :-2], tp[:, 1:-1], tp[:, 2:]], axis=2)

    kb = band(k)
    vb = band(v)
    s = jnp.einsum('bnqhgd,bnkhd->bnhgqk', qb, kb, preferred_element_type=jnp.float32) * (HEAD_DIM ** -0.5)
    qoff = jnp.arange(BLOCK)
    koff = jnp.arange(3 * BLOCK) - BLOCK
    rel = koff[None, :] - qoff[:, None]
    bias = rel_table[rel_bucket(rel)].transpose(2, 0, 1).reshape(WA_KV_HEADS, WA_GROUP, BLOCK, 3 * BLOCK)
    kpos = jnp.arange(n)[:, None] * BLOCK + koff[None, :]
    valid = (jnp.abs(rel) <= WINDOW)[None] & ((kpos >= 0) & (kpos < S))[:, None, :]
    s = jnp.where(valid[None, :, None, None], s + bias.astype(jnp.float32), NEG_INF)
    sink_l = sink.astype(jnp.float32).reshape(WA_KV_HEADS, WA_GROUP)[None, None, :, :, None, None]
    m = jnp.maximum(jnp.max(s, axis=-1, keepdims=True), sink_l)
    e = jnp.exp(s - m)
    p = e / (jnp.sum(e, axis=-1, keepdims=True) + jnp.exp(sink_l - m))
    o = jnp.einsum('bnhgqk,bnkhd->bnqhgd', p.astype(v.dtype), vb)
    return o.reshape(B, S, WA_HEADS * HEAD_DIM)


def memory_attention(q, mem_n, w_mem_kv, qk_g):
    B, S = q.shape[0], q.shape[1]
    M = mem_n.shape[1]
    kv = mem_n @ w_mem_kv
    k, v = jnp.split(kv, 2, axis=-1)
    k = k.reshape(B, M, MA_HEADS, MA_HEAD_DIM)
    v = v.reshape(B, M, MA_HEADS, MA_HEAD_DIM)
    q = rms_norm(q, qk_g[0])
    k = rms_norm(k, qk_g[1])
    s = jnp.einsum('bshd,bmhd->bhsm', q, k, preferred_element_type=jnp.float32) * (MA_HEAD_DIM ** -0.5)
    p = jax.nn.softmax(s, axis=-1)
    o = jnp.einsum('bhsm,bmhd->bshd', p.astype(v.dtype), v)
    return o.reshape(B, S, MA_HEADS * MA_HEAD_DIM)


def encoder_layer(x, mem, layer_idx, rel_bias, norm1_g, w_in, da_qk_g, da_lambda, da_subln_g, conv_w, conv_b, conv_ln_g, conv_ln_b, wa_qk_g, wa_sink, mem_norm_g, w_mem_kv, ma_qk_g, w_branch, w_out, norm2_g, w_ff1, w_ff2):
    B, S, _ = x.shape
    h = rms_norm(x, norm1_g)
    u = h @ w_in
    split_idx = tuple(int(i) for i in np.cumsum(SPLIT_SIZES)[:-1])
    da_q, da_k, da_v, conv_in, wa_q, wa_k, wa_v, ma_q, gate_logits = jnp.split(u, split_idx, axis=-1)
    lam_init = 0.8 - 0.6 * math.exp(-0.3 * layer_idx)
    br_da = diff_attention(da_q.reshape(B, S, DA_HEADS, 2, HEAD_DIM), da_k.reshape(B, S, DA_HEADS, 2, HEAD_DIM), da_v.reshape(B, S, DA_HEADS, DA_VDIM), da_qk_g, da_lambda, lam_init, da_subln_g, rel_bias[:, :2 * DA_HEADS])
    br_conv = conv_module(conv_in, conv_w, conv_b, conv_ln_g, conv_ln_b)
    br_wa = window_attention(wa_q.reshape(B, S, WA_HEADS, HEAD_DIM), wa_k.reshape(B, S, WA_KV_HEADS, HEAD_DIM), wa_v.reshape(B, S, WA_KV_HEADS, HEAD_DIM), wa_qk_g, wa_sink, rel_bias[:, 2 * DA_HEADS:])
    br_ma = memory_attention(ma_q.reshape(B, S, MA_HEADS, MA_HEAD_DIM), rms_norm(mem, mem_norm_g), w_mem_kv, ma_qk_g)
    gates = jax.nn.sigmoid(gate_logits).reshape(B, S, N_BRANCHES, D_MODEL)
    merged = gates[:, :, 0] * (br_da @ w_branch[0])
    merged = merged + gates[:, :, 1] * (br_conv @ w_branch[1])
    merged = merged + gates[:, :, 2] * (br_wa @ w_branch[2])
    merged = merged + gates[:, :, 3] * (br_ma @ w_branch[3])
    x = x + merged @ w_out
    h2 = rms_norm(x, norm2_g)
    f = jnp.square(jax.nn.relu(h2 @ w_ff1)) @ w_ff2
    return x + f


def run_trunk(x, mem, rel_bias, norm1_g, w_in, da_qk_g, da_lambda, da_subln_g, conv_w, conv_b, conv_ln_g, conv_ln_b, wa_qk_g, wa_sink, mem_norm_g, w_mem_kv, ma_qk_g, w_branch, w_out, norm2_g, w_ff1, w_ff2):
    for l in range(DEPTH):
        x = encoder_layer(x, mem, l, rel_bias, norm1_g[l], w_in[l], da_qk_g[l], da_lambda[l], da_subln_g[l], conv_w[l], conv_b[l], conv_ln_g[l], conv_ln_b[l], wa_qk_g[l], wa_sink[l], mem_norm_g[l], w_mem_kv[l], ma_qk_g[l], w_branch[l], w_out[l], norm2_g[l], w_ff1[l], w_ff2[l])
    return x


def setup_inputs(seed: int = 0) -> dict:
    key = jax.random.key(seed)
    ks = jax.random.split(key, 24)

    def nrm(k, shape, scale):
        return jax.random.normal(k, shape, jnp.float32) * scale

    def gain(k, shape):
        return 1.0 + 0.02 * jax.random.normal(k, shape, jnp.float32)

    return {
        'x_prompt': nrm(ks[0], (BATCH, SEQ, D_MODEL), 1.0),
        'x_sample': nrm(ks[1], (DEC_BATCH, DEC_SEQ, D_MODEL), 1.0),
        'mem_prompt': nrm(ks[2], (BATCH, MEM_TOKENS, D_MODEL), 1.0),
        'mem_sample': nrm(ks[3], (DEC_BATCH, MEM_TOKENS, D_MODEL), 1.0),
        'rel_bias': nrm(ks[4], (REL_BUCKETS, REL_HEADS), 0.5),
        'norm1_g': gain(ks[5], (DEPTH, D_MODEL)),
        'w_in': nrm(ks[6], (DEPTH, D_MODEL, IN_WIDTH), D_MODEL ** -0.5),
        'da_qk_g': gain(ks[7], (DEPTH, 2, HEAD_DIM)),
        'da_lambda': nrm(ks[8], (DEPTH, 4, HEAD_DIM), 0.1),
        'da_subln_g': gain(ks[9], (DEPTH, DA_VDIM)),
        'conv_w': nrm(ks[10], (DEPTH, CONV_WIDTH, CONV_CH), CONV_WIDTH ** -0.5),
        'conv_b': nrm(ks[11], (DEPTH, CONV_CH), 0.02),
        'conv_ln_g': gain(ks[12], (DEPTH, CONV_CH)),
        'conv_ln_b': nrm(ks[13], (DEPTH, CONV_CH), 0.02),
        'wa_qk_g': gain(ks[14], (DEPTH, 2, HEAD_DIM)),
        'wa_sink': nrm(ks[15], (DEPTH, WA_HEADS), 0.5),
        'mem_norm_g': gain(ks[16], (DEPTH, D_MODEL)),
        'w_mem_kv': nrm(ks[17], (DEPTH, D_MODEL, 2 * MA_HEADS * MA_HEAD_DIM), D_MODEL ** -0.5),
        'ma_qk_g': gain(ks[18], (DEPTH, 2, MA_HEAD_DIM)),
        'w_branch': nrm(ks[19], (DEPTH, N_BRANCHES, BRANCH_WIDTH, D_MODEL), BRANCH_WIDTH ** -0.5),
        'w_out': nrm(ks[20], (DEPTH, D_MODEL, D_MODEL), D_MODEL ** -0.5),
        'norm2_g': gain(ks[21], (DEPTH, D_MODEL)),
        'w_ff1': nrm(ks[22], (DEPTH, D_MODEL, D_FF), D_MODEL ** -0.5),
        'w_ff2': nrm(ks[23], (DEPTH, D_FF, D_MODEL), D_FF ** -0.5),
    }


def reference(x_prompt, x_sample, mem_prompt, mem_sample, rel_bias, norm1_g, w_in, da_qk_g, da_lambda, da_subln_g, conv_w, conv_b, conv_ln_g, conv_ln_b, wa_qk_g, wa_sink, mem_norm_g, w_mem_kv, ma_qk_g, w_branch, w_out, norm2_g, w_ff1, w_ff2):
    y_prompt = run_trunk(x_prompt, mem_prompt, rel_bias, norm1_g, w_in, da_qk_g, da_lambda, da_subln_g, conv_w, conv_b, conv_ln_g, conv_ln_b, wa_qk_g, wa_sink, mem_norm_g, w_mem_kv, ma_qk_g, w_branch, w_out, norm2_g, w_ff1, w_ff2)
    y_sample = run_trunk(x_sample, mem_sample, rel_bias, norm1_g, w_in, da_qk_g, da_lambda, da_subln_g, conv_w, conv_b, conv_ln_g, conv_ln_b, wa_qk_g, wa_sink, mem_norm_g, w_mem_kv, ma_qk_g, w_branch, w_out, norm2_g, w_ff1, w_ff2)
    return (y_prompt, y_sample)
```

```python
import functools
import math

import jax
import jax.numpy as jnp
from jax import lax
from jax.experimental import pallas as pl
from jax.experimental.pallas import tpu as pltpu

F32 = jnp.float32
BF16 = jnp.bfloat16

D_MODEL = 1024
DEPTH = 4
HEAD_DIM = 64
DA_HEADS = 4
DA_VDIM = 2 * HEAD_DIM
CONV_CH = 512
CONV_WIDTH = 31
WA_HEADS = 8
WA_KV_HEADS = 2
WINDOW = 128
BLOCK = 128
MEM_TOKENS = 256
MA_HEADS = 4
MA_HEAD_DIM = 128
BRANCH_WIDTH = 512
N_BRANCHES = 4
D_FF = 4 * D_MODEL
REL_BUCKETS = 32
REL_MAX_DIST = 128
EPS = 1e-6
NEG_INF = -1e30

U_WIDTH = 4096
COL_DA_Q, COL_DA_K, COL_DA_V, COL_CONV, COL_WA_Q, COL_WA_K, COL_WA_V, COL_MA_Q = (
    0, 512, 1024, 1536, 2560, 3072, 3328, 3584)
SUB = 256
SUB_SEG = (64, 64, 64, 64, 0, 0, 0, 0, 0, 0, 64, 64, 64, 0, 128, 128)
GATE_START = 3840

VMEM_LIMIT = 56 * 1024 * 1024


def _cparams(sem):
    return pltpu.CompilerParams(dimension_semantics=sem, vmem_limit_bytes=VMEM_LIMIT)


def _resident(shape, index_map):
    return pl.BlockSpec(shape, index_map, pipeline_mode=pl.Buffered(1))


def _rms(x, g):
    return x * lax.rsqrt(jnp.mean(x * x, axis=-1, keepdims=True) + EPS) * g


def _inproj_kernel(x_ref, g1_ref, w_ref, gain_ref, o_ref):
    h = _rms(x_ref[...], g1_ref[...]).astype(BF16)
    r = lax.broadcasted_iota(jnp.int32, (SUB, SUB), 0)
    c = lax.broadcasted_iota(jnp.int32, (SUB, SUB), 1)
    for s, seg in enumerate(SUB_SEG):
        cols = slice(s * SUB, (s + 1) * SUB)
        u = jnp.dot(h, w_ref[:, cols], preferred_element_type=F32)
        if seg:
            e = jnp.where(r // seg == c // seg, 1.0 / seg, 0.0).astype(BF16)
            ms = jnp.dot((u * u).astype(BF16), e, preferred_element_type=F32)
            u = u * lax.rsqrt(ms + EPS) * gain_ref[:, cols]
        o_ref[:, cols] = u.astype(BF16)


def _inproj(x2d, g1, w_ext, gain, tm):
    t = x2d.shape[0]
    return pl.pallas_call(
        _inproj_kernel,
        grid=(t // tm,),
        in_specs=[
            pl.BlockSpec((tm, D_MODEL), lambda i: (i, 0)),
            _resident((1, D_MODEL), lambda i: (0, 0)),
            _resident((D_MODEL, U_WIDTH), lambda i: (0, 0)),
            _resident((1, U_WIDTH), lambda i: (0, 0)),
        ],
        out_specs=pl.BlockSpec((tm, U_WIDTH), lambda i: (i, 0)),
        out_shape=jax.ShapeDtypeStruct((t, U_WIDTH), BF16),
        compiler_params=_cparams(("parallel",)),
        name="inproj",
    )(x2d, g1, w_ext, gain)


def _da_kernel(q_ref, k_ref, vt_ref, bias_ref, cfar_ref, lam_ref, g_ref, o_ref,
               q2_s, m_s, l_s, acc_s, *, tq, tk, nk, lam_init):
    i = pl.program_id(2)
    q = q_ref[0]
    lane = lax.broadcasted_iota(jnp.int32, (tq, 2 * HEAD_DIM), 1)
    zero = jnp.zeros_like(q)
    q2_s[0:tq, :] = jnp.where(lane < HEAD_DIM, q, zero)
    q2_s[tq:2 * tq, :] = jnp.where(lane >= HEAD_DIM, q, zero)
    m_s[...] = jnp.full(m_s.shape, NEG_INF, F32)
    l_s[...] = jnp.zeros(l_s.shape, F32)
    acc_s[...] = jnp.zeros(acc_s.shape, F32)

    def step(j, bias, cvec):
        kc = k_ref[0, pl.ds(pl.multiple_of(j * tk, tk), tk), :]
        st = lax.dot_general(kc, q2_s[...], (((1,), (1,)), ((), ())),
                             preferred_element_type=F32)
        if bias is not None:
            st = st + bias
        mx = jnp.max(st, axis=0, keepdims=True)
        if cvec is not None:
            mx = mx + cvec
        m_old = m_s[...]
        m_new = jnp.maximum(m_old, mx)
        alpha = jnp.exp(m_old - m_new)
        shift = m_new if cvec is None else m_new - cvec
        p = jnp.exp(st - shift)
        l_s[...] = alpha * l_s[...] + jnp.sum(p, axis=0, keepdims=True)
        m_s[...] = m_new
        pv = jnp.dot(vt_ref[0, 0, j], p.astype(BF16), preferred_element_type=F32)
        acc_s[...] = alpha * acc_s[...] + pv

    def far_left(j, carry):
        step(j, None, cfar_ref[0, 0])
        return carry

    def far_right(j, carry):
        step(j, None, cfar_ref[0, 1])
        return carry

    lax.fori_loop(0, jnp.maximum(i - 1, 0), far_left, 0)
    for d in (-1, 0, 1):
        j = i + d

        @pl.when(jnp.logical_and(j >= 0, j < nk))
        def _():
            step(j, bias_ref[0, d + 1], None)

    lax.fori_loop(jnp.minimum(i + 2, nk), nk, far_right, 0)

    lp = lam_ref[...]
    lam = (jnp.exp(jnp.sum(lp[0:1] * lp[1:2], keepdims=True))
           - jnp.exp(jnp.sum(lp[2:3] * lp[3:4], keepdims=True)) + lam_init)
    o = acc_s[...] * (1.0 / l_s[...])
    dd = o[:, 0:tq] - lam * o[:, tq:2 * tq]
    ms = jnp.mean(dd * dd, axis=0, keepdims=True)
    y = dd * lax.rsqrt(ms + EPS) * g_ref[...] * (1.0 - lam_init)
    o_ref[0] = y.T.astype(BF16)


def _diff_attention(u3, vt, bias_t, cfar, lam_p, subln_g, lam_init, tq, tk):
    nb, s, _ = u3.shape
    nq, nk = s // tq, s // tk
    kern = functools.partial(_da_kernel, tq=tq, tk=tk, nk=nk, lam_init=lam_init)
    return pl.pallas_call(
        kern,
        grid=(nb, DA_HEADS, nq),
        in_specs=[
            pl.BlockSpec((1, tq, DA_VDIM), lambda b, h, i: (b, i, COL_DA_Q // DA_VDIM + h)),
            pl.BlockSpec((1, s, DA_VDIM), lambda b, h, i: (b, 0, COL_DA_K // DA_VDIM + h)),
            pl.BlockSpec((1, 1, nk, DA_VDIM, tk), lambda b, h, i: (b, h, 0, 0, 0)),
            pl.BlockSpec((1, 3, tk, 2 * tq), lambda b, h, i: (h, 0, 0, 0)),
            pl.BlockSpec((1, 2, 1, 2 * tq), lambda b, h, i: (h, 0, 0, 0)),
            pl.BlockSpec((4, HEAD_DIM), lambda b, h, i: (0, 0)),
            pl.BlockSpec((DA_VDIM, 1), lambda b, h, i: (0, 0)),
        ],
        out_specs=pl.BlockSpec((1, tq, DA_VDIM), lambda b, h, i: (b, i, h)),
        out_shape=jax.ShapeDtypeStruct((nb, s, DA_HEADS * DA_VDIM), BF16),
        scratch_shapes=[
            pltpu.VMEM((2 * tq, DA_VDIM), BF16),
            pltpu.VMEM((1, 2 * tq), F32),
            pltpu.VMEM((1, 2 * tq), F32),
            pltpu.VMEM((DA_VDIM, 2 * tq), F32),
        ],
        compiler_params=_cparams(("parallel", "parallel", "parallel")),
        name="diff_attn",
    )(u3, u3, vt, bias_t, cfar, lam_p, subln_g)


CONV_HALO = 16
CONV_ROWS = 64
SUBLANES = 8


def _conv_kernel(ap_ref, ac_ref, an_ref, gp_ref, gc_ref, gn_ref, w_ref, b_ref, lg_ref, lb_ref,
                 o_ref, z_s, zp_s, *, tc, nt):
    i = pl.program_id(1)

    def glu(a_ref, g_ref):
        return a_ref[0].astype(F32) * jax.nn.sigmoid(g_ref[0].astype(F32))

    z_s[0:CONV_HALO, :] = jnp.where(i > 0, glu(ap_ref, gp_ref), 0.0)
    z_s[CONV_HALO:CONV_HALO + tc, :] = glu(ac_ref, gc_ref)
    z_s[CONV_HALO + tc:2 * CONV_HALO + tc, :] = jnp.where(i < nt - 1, glu(an_ref, gn_ref), 0.0)

    pad = CONV_WIDTH // 2
    span = tc + 3 * SUBLANES
    for p in range(SUBLANES):
        zp_s[p, 0:span, :] = z_s[p:p + span, :]

    def rows(r, carry):
        r0 = pl.multiple_of(r * CONV_ROWS, CONV_ROWS)
        acc = jnp.zeros((CONV_ROWS, CONV_CH), F32) + b_ref[...]
        for t in range(CONV_WIDTH):
            off = CONV_HALO - pad + t
            p = off % SUBLANES
            base = pl.multiple_of(r0 + (off - p), SUBLANES)
            acc = acc + w_ref[t:t + 1, :] * zp_s[p, pl.ds(base, CONV_ROWS), :]
        mu = jnp.mean(acc, axis=-1, keepdims=True)
        xc = acc - mu
        var = jnp.mean(xc * xc, axis=-1, keepdims=True)
        y = xc * lax.rsqrt(var + EPS) * lg_ref[...] + lb_ref[...]
        o_ref[0, pl.ds(r0, CONV_ROWS), :] = (y * jax.nn.sigmoid(y)).astype(BF16)
        return carry

    lax.fori_loop(0, tc // CONV_ROWS, rows, 0)


def _conv_module(u3, conv_w, conv_b, ln_g, ln_b, tc):
    nb, s, _ = u3.shape
    nt = s // tc
    hb = tc // CONV_HALO
    nhb = s // CONV_HALO
    ca = COL_CONV // CONV_CH
    cg = ca + 1

    def prev(col):
        return pl.BlockSpec((1, CONV_HALO, CONV_CH),
                            lambda b, i: (b, jnp.maximum(i * hb - 1, 0), col))

    def cur(col):
        return pl.BlockSpec((1, tc, CONV_CH), lambda b, i: (b, i, col))

    def nxt(col):
        return pl.BlockSpec((1, CONV_HALO, CONV_CH),
                            lambda b, i: (b, jnp.minimum((i + 1) * hb, nhb - 1), col))

    vec = pl.BlockSpec((1, CONV_CH), lambda b, i: (0, 0))
    kern = functools.partial(_conv_kernel, tc=tc, nt=nt)
    return pl.pallas_call(
        kern,
        grid=(nb, nt),
        in_specs=[prev(ca), cur(ca), nxt(ca), prev(cg), cur(cg), nxt(cg),
                  pl.BlockSpec((CONV_WIDTH, CONV_CH), lambda b, i: (0, 0)), vec, vec, vec],
        out_specs=pl.BlockSpec((1, tc, CONV_CH), lambda b, i: (b, i, 0)),
        out_shape=jax.ShapeDtypeStruct((nb, s, CONV_CH), BF16),
        scratch_shapes=[pltpu.VMEM((tc + 2 * CONV_HALO, CONV_CH), F32),
                        pltpu.VMEM((SUBLANES, tc + 3 * SUBLANES, CONV_CH), F32)],
        compiler_params=_cparams(("parallel", "parallel")),
        name="conv_module",
    )(u3, u3, u3, u3, u3, u3, conv_w, conv_b, ln_g, ln_b)


def _wa_kernel(sink_ref, q_ref, kp_ref, kc_ref, kn_ref, vp_ref, vc_ref, vn_ref, bias_ref,
               o_ref, k_s, v_s, *, tw, s_len):
    i = pl.program_id(1)
    k_s[0:BLOCK, :] = kp_ref[0]
    k_s[BLOCK:BLOCK + tw, :] = kc_ref[0]
    k_s[BLOCK + tw:2 * BLOCK + tw, :] = kn_ref[0]
    v_s[0:BLOCK, :] = vp_ref[0]
    v_s[BLOCK:BLOCK + tw, :] = vc_ref[0]
    v_s[BLOCK + tw:2 * BLOCK + tw, :] = vn_ref[0]

    nkey = 3 * BLOCK
    pair = 2 * HEAD_DIM
    lane_q = lax.broadcasted_iota(jnp.int32, (BLOCK, pair), 1)
    lane_v = lax.broadcasted_iota(jnp.int32, (nkey, pair), 1)
    row2 = lax.broadcasted_iota(jnp.int32, (2 * BLOCK, 1), 0)
    col = lax.broadcasted_iota(jnp.int32, (1, nkey), 1)

    def blk(sb, carry):
        r0 = pl.multiple_of(sb * BLOCK, BLOCK)
        kpos = i * tw + r0 - BLOCK + col
        inside = jnp.logical_and(kpos >= 0, kpos < s_len)
        for jg in range(WA_HEADS // 2):
            kvh = (2 * jg) // (WA_HEADS // WA_KV_HEADS)
            qp = q_ref[0, pl.ds(r0, BLOCK), jg * pair:(jg + 1) * pair]
            zq = jnp.zeros_like(qp)
            q2 = jnp.concatenate([jnp.where(lane_q < HEAD_DIM, qp, zq),
                                  jnp.where(lane_q >= HEAD_DIM, qp, zq)], axis=0)
            kd = k_s[pl.ds(r0, nkey), kvh * pair:(kvh + 1) * pair]
            vd = v_s[pl.ds(r0, nkey), kvh * pair:(kvh + 1) * pair]
            sc = lax.dot_general(q2, kd, (((1,), (1,)), ((), ())),
                                 preferred_element_type=F32)
            sc = jnp.where(inside, sc + bias_ref[jg], NEG_INF)
            snk = jnp.where(row2 < BLOCK, sink_ref[2 * jg], sink_ref[2 * jg + 1])
            m = jnp.maximum(jnp.max(sc, axis=-1, keepdims=True), snk)
            e = jnp.exp(sc - m)
            den = jnp.sum(e, axis=-1, keepdims=True) + jnp.exp(snk - m)
            p = (e / den).astype(BF16)
            zv = jnp.zeros_like(vd)
            o = (jnp.dot(p[0:BLOCK], jnp.where(lane_v < HEAD_DIM, vd, zv),
                         preferred_element_type=F32)
                 + jnp.dot(p[BLOCK:2 * BLOCK], jnp.where(lane_v >= HEAD_DIM, vd, zv),
                           preferred_element_type=F32))
            o_ref[0, pl.ds(r0, BLOCK), jg * pair:(jg + 1) * pair] = o.astype(BF16)
        return carry

    lax.fori_loop(0, tw // BLOCK, blk, 0)


def _window_attention(u3, wa_bias, sink, tw):
    nb, s, _ = u3.shape
    nt = s // tw
    bpt = tw // BLOCK
    nblk = s // BLOCK
    width = WA_HEADS * HEAD_DIM
    kvw = 2 * WA_KV_HEADS * HEAD_DIM

    def prev(col):
        return pl.BlockSpec((1, BLOCK, kvw), lambda b, i: (b, jnp.maximum(i * bpt - 1, 0), col))

    def cur(col):
        return pl.BlockSpec((1, tw, kvw), lambda b, i: (b, i, col))

    def nxt(col):
        return pl.BlockSpec((1, BLOCK, kvw),
                            lambda b, i: (b, jnp.minimum((i + 1) * bpt, nblk - 1), col))

    ck, cv = COL_WA_K // kvw, COL_WA_V // kvw
    kern = functools.partial(_wa_kernel, tw=tw, s_len=s)
    return pl.pallas_call(
        kern,
        grid=(nb, nt),
        in_specs=[
            pl.BlockSpec(memory_space=pltpu.SMEM),
            pl.BlockSpec((1, tw, width), lambda b, i: (b, i, COL_WA_Q // width)),
            prev(ck), cur(ck), nxt(ck), prev(cv), cur(cv), nxt(cv),
            pl.BlockSpec((WA_HEADS // 2, 2 * BLOCK, 3 * BLOCK), lambda b, i: (0, 0, 0)),
        ],
        out_specs=pl.BlockSpec((1, tw, width), lambda b, i: (b, i, 0)),
        out_shape=jax.ShapeDtypeStruct((nb, s, width), BF16),
        scratch_shapes=[pltpu.VMEM((tw + 2 * BLOCK, kvw), BF16),
                        pltpu.VMEM((tw + 2 * BLOCK, kvw), BF16)],
        compiler_params=_cparams(("parallel", "parallel")),
        name="window_attn",
    )(sink, u3, u3, u3, u3, u3, u3, u3, wa_bias)


def _memkv_kernel(mem_ref, g_ref, w_ref, gk_ref, k_ref, v_ref):
    hn = _rms(mem_ref[0], g_ref[0]).astype(BF16)
    kv = jnp.dot(hn, w_ref[0], preferred_element_type=F32)
    width = MA_HEADS * MA_HEAD_DIM
    for h in range(MA_HEADS):
        cols = slice(h * MA_HEAD_DIM, (h + 1) * MA_HEAD_DIM)
        k_ref[0, 0, :, cols] = _rms(kv[:, cols], gk_ref[0]).astype(BF16)
    v_ref[0, 0] = kv[:, width:2 * width].astype(BF16)


def _mem_kv(mem, mem_norm_g, w_mem_kv, gk):
    nb, m, _ = mem.shape
    width = MA_HEADS * MA_HEAD_DIM
    out = jax.ShapeDtypeStruct((DEPTH, nb, m, width), BF16)
    return pl.pallas_call(
        _memkv_kernel,
        grid=(DEPTH, nb),
        in_specs=[
            pl.BlockSpec((1, m, D_MODEL), lambda l, b: (b, 0, 0)),
            pl.BlockSpec((1, 1, D_MODEL), lambda l, b: (l, 0, 0)),
            pl.BlockSpec((1, D_MODEL, 2 * width), lambda l, b: (l, 0, 0)),
            pl.BlockSpec((1, 1, MA_HEAD_DIM), lambda l, b: (l, 0, 0)),
        ],
        out_specs=[pl.BlockSpec((1, 1, m, width), lambda l, b: (l, b, 0, 0)),
                   pl.BlockSpec((1, 1, m, width), lambda l, b: (l, b, 0, 0))],
        out_shape=[out, out],
        compiler_params=_cparams(("parallel", "parallel")),
        name="mem_kv",
    )(mem, mem_norm_g, w_mem_kv, gk)


def _ma_kernel(q_ref, k_ref, v_ref, o_ref):
    for h in range(MA_HEADS):
        cols = slice(h * MA_HEAD_DIM, (h + 1) * MA_HEAD_DIM)
        sc = lax.dot_general(q_ref[0, :, cols], k_ref[0, :, cols], (((1,), (1,)), ((), ())),
                             preferred_element_type=F32)
        e = jnp.exp(sc - jnp.max(sc, axis=-1, keepdims=True))
        p = (e / jnp.sum(e, axis=-1, keepdims=True)).astype(BF16)
        o_ref[0, :, cols] = jnp.dot(p, v_ref[0, :, cols],
                                    preferred_element_type=F32).astype(BF16)


def _memory_attention(u3, kmem, vmem, tq):
    nb, s, _ = u3.shape
    m = kmem.shape[1]
    width = MA_HEADS * MA_HEAD_DIM
    return pl.pallas_call(
        _ma_kernel,
        grid=(nb, s // tq),
        in_specs=[
            pl.BlockSpec((1, tq, width), lambda b, i: (b, i, COL_MA_Q // width)),
            pl.BlockSpec((1, m, width), lambda b, i: (b, 0, 0)),
            pl.BlockSpec((1, m, width), lambda b, i: (b, 0, 0)),
        ],
        out_specs=pl.BlockSpec((1, tq, width), lambda b, i: (b, i, 0)),
        out_shape=jax.ShapeDtypeStruct((nb, s, width), BF16),
        compiler_params=_cparams(("parallel", "parallel")),
        name="mem_attn",
    )(u3, kmem, vmem)


def _merge_kernel(x_ref, g1_ref, da_ref, cv_ref, wa_ref, ma_ref, wg_ref, wb_ref, wo_ref, o_ref):
    x = x_ref[...]
    h = _rms(x, g1_ref[...]).astype(BF16)
    merged = None
    for n, br in enumerate((da_ref, cv_ref, wa_ref, ma_ref)):
        logits = jnp.dot(h, wg_ref[:, n * D_MODEL:(n + 1) * D_MODEL], preferred_element_type=F32)
        t = jax.nn.sigmoid(logits) * jnp.dot(br[...], wb_ref[n], preferred_element_type=F32)
        merged = t if merged is None else merged + t
    o_ref[...] = x + jnp.dot(merged.astype(BF16), wo_ref[...], preferred_element_type=F32)


def _merge(x2d, g1, br_da, br_conv, br_wa, br_ma, w_gate, w_branch, w_out, tm):
    t = x2d.shape[0]
    br = pl.BlockSpec((tm, BRANCH_WIDTH), lambda i: (i, 0))
    return pl.pallas_call(
        _merge_kernel,
        grid=(t // tm,),
        in_specs=[
            pl.BlockSpec((tm, D_MODEL), lambda i: (i, 0)),
            _resident((1, D_MODEL), lambda i: (0, 0)),
            br, br, br, br,
            _resident((D_MODEL, N_BRANCHES * D_MODEL), lambda i: (0, 0)),
            _resident((N_BRANCHES, BRANCH_WIDTH, D_MODEL), lambda i: (0, 0, 0)),
            _resident((D_MODEL, D_MODEL), lambda i: (0, 0)),
        ],
        out_specs=pl.BlockSpec((tm, D_MODEL), lambda i: (i, 0)),
        out_shape=jax.ShapeDtypeStruct((t, D_MODEL), F32),
        compiler_params=_cparams(("parallel",)),
        name="merge",
    )(x2d, g1, br_da, br_conv, br_wa, br_ma, w_gate, w_branch, w_out)


FF_CHUNK = 1024


def _ffn_kernel(x_ref, g2_ref, w1_ref, w2_ref, o_ref):
    x = x_ref[...]
    h = _rms(x, g2_ref[...]).astype(BF16)
    acc = x
    for c in range(D_FF // FF_CHUNK):
        cols = slice(c * FF_CHUNK, (c + 1) * FF_CHUNK)
        f = jnp.maximum(jnp.dot(h, w1_ref[:, cols], preferred_element_type=F32), 0.0)
        acc = acc + jnp.dot((f * f).astype(BF16), w2_ref[cols, :], preferred_element_type=F32)
    o_ref[...] = acc


def _ffn(x2d, g2, w1, w2, tm):
    t = x2d.shape[0]
    return pl.pallas_call(
        _ffn_kernel,
        grid=(t // tm,),
        in_specs=[
            pl.BlockSpec((tm, D_MODEL), lambda i: (i, 0)),
            _resident((1, D_MODEL), lambda i: (0, 0)),
            _resident((D_MODEL, D_FF), lambda i: (0, 0)),
            _resident((D_FF, D_MODEL), lambda i: (0, 0)),
        ],
        out_specs=pl.BlockSpec((tm, D_MODEL), lambda i: (i, 0)),
        out_shape=jax.ShapeDtypeStruct((t, D_MODEL), F32),
        compiler_params=_cparams(("parallel",)),
        name="ffn",
    )(x2d, g2, w1, w2)


def _rel_bucket(rel):
    nb = REL_BUCKETS // 2
    max_exact = nb // 2
    ret = jnp.where(rel > 0, nb, 0)
    n = jnp.abs(rel)
    nf = jnp.maximum(n, 1).astype(F32)
    large = max_exact + (jnp.log(nf / max_exact) / math.log(REL_MAX_DIST / max_exact)
                         * (nb - max_exact)).astype(jnp.int32)
    large = jnp.minimum(large, nb - 1)
    return ret + jnp.where(n < max_exact, n, large)


def _da_bias_tables(rel_bias, tq, tk):
    table = rel_bias[:, :2 * DA_HEADS].astype(F32)
    kk = jnp.arange(tk)[None, :, None]
    qq = jnp.arange(tq)[None, None, :]
    dd = (jnp.arange(3) - 1)[:, None, None]
    vals = table[_rel_bucket(dd * tk + kk - qq)]
    vals = vals.reshape(3, tk, tq, DA_HEADS, 2).transpose(3, 0, 1, 4, 2)
    bias_t = vals.reshape(DA_HEADS, 3, tk, 2 * tq)
    nbk = REL_BUCKETS // 2
    far = jnp.stack([table[nbk - 1], table[REL_BUCKETS - 1]])
    far = far.reshape(2, DA_HEADS, 2).transpose(1, 0, 2)
    cfar = jnp.repeat(far, tq, axis=-1).reshape(DA_HEADS, 2, 1, 2 * tq)
    return bias_t, cfar


def _wa_bias_table(rel_bias):
    table = rel_bias[:, 2 * DA_HEADS:].astype(F32)
    qoff = jnp.arange(BLOCK)
    koff = jnp.arange(3 * BLOCK) - BLOCK
    rel = koff[None, :] - qoff[:, None]
    bias = table[_rel_bucket(rel)].transpose(2, 0, 1)
    bias = jnp.where((jnp.abs(rel) <= WINDOW)[None], bias, NEG_INF)
    return bias.reshape(WA_HEADS // 2, 2 * BLOCK, 3 * BLOCK)


def _dup_heads(w, heads, dim):
    w = w.reshape(w.shape[0], heads, 1, dim)
    return jnp.broadcast_to(w, (w.shape[0], heads, 2, dim)).reshape(w.shape[0], heads * 2 * dim)


def _layer_params(l, w_in, da_qk_g, wa_qk_g, ma_qk_g):
    w = w_in[l]
    wk = w[:, 3072:3200]
    wv = w[:, 3200:3328]
    w_ext = jnp.concatenate([
        w[:, 0:2560], w[:, 2560:3072],
        _dup_heads(wk, WA_KV_HEADS, HEAD_DIM), _dup_heads(wv, WA_KV_HEADS, HEAD_DIM),
        w[:, 3328:3840]], axis=1).astype(BF16)
    ones = lambda n: jnp.ones((n,), F32)
    gain = jnp.concatenate([
        jnp.tile(da_qk_g[l, 0], 2 * DA_HEADS) * HEAD_DIM ** -0.5,
        jnp.tile(da_qk_g[l, 1], 2 * DA_HEADS),
        ones(512 + 1024),
        jnp.tile(wa_qk_g[l, 0], WA_HEADS) * HEAD_DIM ** -0.5,
        jnp.tile(wa_qk_g[l, 1], 2 * WA_KV_HEADS),
        ones(256),
        jnp.tile(ma_qk_g[l, 0], MA_HEADS) * MA_HEAD_DIM ** -0.5,
    ]).reshape(1, U_WIDTH).astype(F32)
    w_gate = w[:, GATE_START:].astype(BF16)
    return w_ext, gain, w_gate


def _tile(n, pref):
    return pref if n % pref == 0 else n


def _trunk(x, mem, rel_bias, norm1_g, w_in, da_qk_g, da_lambda, da_subln_g, conv_w, conv_b,
           conv_ln_g, conv_ln_b, wa_qk_g, wa_sink, mem_norm_g, w_mem_kv, ma_qk_g, w_branch, w_out,
           norm2_g, w_ff1, w_ff2, *, tq=512, tm=512):
    nb, s, _ = x.shape
    tq = _tile(s, tq)
    tk = tq
    tm = _tile(nb * s, tm)
    nk = s // tk
    bias_t, cfar = _da_bias_tables(rel_bias, tq, tk)
    wa_bias = _wa_bias_table(rel_bias)
    kmem, vmem = _mem_kv(mem, mem_norm_g.reshape(DEPTH, 1, D_MODEL), w_mem_kv.astype(BF16),
                         ma_qk_g[:, 1].reshape(DEPTH, 1, MA_HEAD_DIM))
    x2d = x.reshape(nb * s, D_MODEL)
    for l in range(DEPTH):
        w_ext, gain, w_gate = _layer_params(l, w_in, da_qk_g, wa_qk_g, ma_qk_g)
        g1 = norm1_g[l].reshape(1, D_MODEL)
        u = _inproj(x2d, g1, w_ext, gain, tm)
        u3 = u.reshape(nb, s, U_WIDTH)
        v = u3[:, :, COL_DA_V:COL_DA_V + DA_HEADS * DA_VDIM]
        vt = v.reshape(nb, nk, tk, DA_HEADS, DA_VDIM).transpose(0, 3, 1, 4, 2)
        lam_init = 0.8 - 0.6 * math.exp(-0.3 * l)
        br_da = _diff_attention(u3, vt, bias_t, cfar, da_lambda[l],
                                da_subln_g[l].reshape(DA_VDIM, 1), lam_init, tq, tk)
        br_conv = _conv_module(u3, conv_w[l], conv_b[l].reshape(1, CONV_CH),
                               conv_ln_g[l].reshape(1, CONV_CH), conv_ln_b[l].reshape(1, CONV_CH),
                               _tile(s, 512))
        br_wa = _window_attention(u3, wa_bias, wa_sink[l], _tile(s, 512))
        br_ma = _memory_attention(u3, kmem[l], vmem[l], _tile(s, 1024))
        flat = lambda a: a.reshape(nb * s, BRANCH_WIDTH)
        x2d = _merge(x2d, g1, flat(br_da), flat(br_conv), flat(br_wa), flat(br_ma), w_gate,
                     w_branch[l].astype(BF16), w_out[l].astype(BF16), tm)
        x2d = _ffn(x2d, norm2_g[l].reshape(1, D_MODEL), w_ff1[l].astype(BF16),
                   w_ff2[l].astype(BF16), tm)
    return x2d.reshape(nb, s, D_MODEL)


def kernel(x_prompt, x_sample, mem_prompt, mem_sample, rel_bias, norm1_g, w_in, da_qk_g, da_lambda, da_subln_g, conv_w, conv_b, conv_ln_g, conv_ln_b, wa_qk_g, wa_sink, mem_norm_g, w_mem_kv, ma_qk_g, w_branch, w_out, norm2_g, w_ff1, w_ff2):
    params = (rel_bias, norm1_g, w_in, da_qk_g, da_lambda, da_subln_g, conv_w, conv_b, conv_ln_g,
              conv_ln_b, wa_qk_g, wa_sink, mem_norm_g, w_mem_kv, ma_qk_g, w_branch, w_out,
              norm2_g, w_ff1, w_ff2)
    if x_prompt.shape[1:] == x_sample.shape[1:]:
        nbp = x_prompt.shape[0]
        y = _trunk(jnp.concatenate([x_prompt, x_sample], axis=0),
                   jnp.concatenate([mem_prompt, mem_sample], axis=0), *params)
        return (y[:nbp], y[nbp:])
    return (_trunk(x_prompt, mem_prompt, *params), _trunk(x_sample, mem_sample, *params))
```

```python
import functools
import math

import jax
import jax.numpy as jnp
from jax import lax
from jax.experimental import pallas as pl
from jax.experimental.pallas import tpu as pltpu

F32 = jnp.float32
BF16 = jnp.bfloat16

D_MODEL = 1024
DEPTH = 4
HEAD_DIM = 64
DA_HEADS = 4
DA_VDIM = 2 * HEAD_DIM
CONV_CH = 512
CONV_WIDTH = 31
WA_HEADS = 8
WA_KV_HEADS = 2
WINDOW = 128
BLOCK = 128
MEM_TOKENS = 256
MA_HEADS = 4
MA_HEAD_DIM = 128
BRANCH_WIDTH = 512
N_BRANCHES = 4
D_FF = 4 * D_MODEL
REL_BUCKETS = 32
REL_MAX_DIST = 128
EPS = 1e-6
NEG_INF = -1e30

U_WIDTH = 4608
COL_DA_Q, COL_DA_K, COL_DA_V, COL_CONV, COL_WA_Q, COL_WA_K, COL_WA_V, COL_MA_Q = (
    0, 512, 1536, 2048, 3072, 3584, 3840, 4096)
DA_KW = 4 * HEAD_DIM
N_AUG = 3
SUB = 256
SUB_SEG = (64, 64, 64, 64, 64, 64, 0, 0, 0, 0, 0, 0, 64, 64, 64, 0, 128, 128)
SUB_DA_K = (2, 3, 4, 5)
GATE_START = 3840
SUBLANES = 8
DA_VROWS = DA_VDIM + SUBLANES
LOG2E = 1.4426950408889634
MAX_SHIFT_GAP = 100.0

VMEM_LIMIT = 56 * 1024 * 1024


def _cparams(sem):
    return pltpu.CompilerParams(dimension_semantics=sem, vmem_limit_bytes=VMEM_LIMIT)


def _resident(shape, index_map):
    return pl.BlockSpec(shape, index_map, pipeline_mode=pl.Buffered(1))


def _rms(x, g):
    return x * lax.rsqrt(jnp.mean(x * x, axis=-1, keepdims=True) + EPS) * g


def _inproj_kernel(x_ref, g1_ref, w_ref, gain_ref, o_ref):
    h = _rms(x_ref[...], g1_ref[...]).astype(BF16)
    r = lax.broadcasted_iota(jnp.int32, (SUB, SUB), 0)
    c = lax.broadcasted_iota(jnp.int32, (SUB, SUB), 1)
    lane = lax.broadcasted_iota(jnp.int32, (1, SUB), 1)
    for s, seg in enumerate(SUB_SEG):
        cols = slice(s * SUB, (s + 1) * SUB)
        u = jnp.dot(h, w_ref[:, cols], preferred_element_type=F32)
        if seg:
            e = jnp.where(r // seg == c // seg, 1.0 / seg, 0.0).astype(BF16)
            ms = jnp.dot((u * u).astype(BF16), e, preferred_element_type=F32)
            u = u * lax.rsqrt(ms + EPS) * gain_ref[:, cols]
        if s in SUB_DA_K:
            ones = jnp.logical_or(
                jnp.logical_and(lane >= HEAD_DIM, lane < HEAD_DIM + N_AUG),
                jnp.logical_and(lane >= 2 * HEAD_DIM, lane < 2 * HEAD_DIM + N_AUG))
            u = jnp.where(ones, 1.0, u)
        o_ref[:, cols] = u.astype(BF16)


def _inproj(x2d, g1, w_ext, gain, tm):
    t = x2d.shape[0]
    return pl.pallas_call(
        _inproj_kernel,
        grid=(t // tm,),
        in_specs=[
            pl.BlockSpec((tm, D_MODEL), lambda i: (i, 0)),
            _resident((1, D_MODEL), lambda i: (0, 0)),
            _resident((D_MODEL, U_WIDTH), lambda i: (0, 0)),
            _resident((1, U_WIDTH), lambda i: (0, 0)),
        ],
        out_specs=pl.BlockSpec((tm, U_WIDTH), lambda i: (i, 0)),
        out_shape=jax.ShapeDtypeStruct((t, U_WIDTH), BF16),
        compiler_params=_cparams(("parallel",)),
        name="inproj",
    )(x2d, g1, w_ext, gain)


def _da_kernel(q_ref, k_ref, vt_ref, bias_ref, cfar_ref, aug_ref, lam_ref, g_ref, o_ref,
               q2_s, m_s, acc_s, *, tq, tk, nk, lam_init, fixed_shift):
    i = pl.program_id(2)
    q = q_ref[0]
    lane = lax.broadcasted_iota(jnp.int32, (tq, 2 * HEAD_DIM), 1)
    first = lane < HEAD_DIM
    zero = jnp.zeros_like(q)
    for var in range(q2_s.shape[0]):
        q2_s[var, 0:tq, 0:DA_VDIM] = jnp.where(first, q, aug_ref[0, 2 * var].astype(BF16))
        q2_s[var, 0:tq, DA_VDIM:DA_KW] = zero
        q2_s[var, tq:2 * tq, 0:DA_VDIM] = zero
        q2_s[var, tq:2 * tq, DA_VDIM:DA_KW] = jnp.where(first, aug_ref[0, 2 * var + 1].astype(BF16), q)
    m_s[...] = jnp.full(m_s.shape, NEG_INF, F32)
    acc_s[...] = jnp.zeros(acc_s.shape, F32)

    def step(j, var, bias, cvec):
        kc = k_ref[0, pl.ds(pl.multiple_of(j * tk, tk), tk), :]
        st = lax.dot_general(kc, q2_s[var], (((1,), (1,)), ((), ())),
                             preferred_element_type=F32)
        if bias is not None:
            st = st + bias
        if fixed_shift:
            p = jnp.exp2(st).astype(BF16)
            acc_s[...] += jnp.dot(vt_ref[0, 0, j], p, preferred_element_type=F32)
            return
        mx = jnp.max(st, axis=0, keepdims=True)
        if cvec is not None:
            mx = mx + cvec
        m_old = m_s[...]
        m_new = jnp.maximum(m_old, mx)
        alpha = jnp.exp2(m_old - m_new)
        shift = m_new if cvec is None else m_new - cvec
        p = jnp.exp2(st - shift).astype(BF16)
        m_s[...] = m_new
        acc_s[...] = alpha * acc_s[...] + jnp.dot(vt_ref[0, 0, j], p, preferred_element_type=F32)

    def far_left(j, carry):
        step(j, 0, None, cfar_ref[0, 0])
        return carry

    def far_right(j, carry):
        step(j, 2 if fixed_shift else 0, None, cfar_ref[0, 1])
        return carry

    lax.fori_loop(0, jnp.maximum(i - 1, 0), far_left, 0)
    for d in (-1, 0, 1):
        j = i + d

        @pl.when(jnp.logical_and(j >= 0, j < nk))
        def _():
            step(j, 1 if fixed_shift else 0, bias_ref[0, d + 1], None)

    lax.fori_loop(jnp.minimum(i + 2, nk), nk, far_right, 0)

    lp = lam_ref[...]
    lam = (jnp.exp(jnp.sum(lp[0:1] * lp[1:2], keepdims=True))
           - jnp.exp(jnp.sum(lp[2:3] * lp[3:4], keepdims=True)) + lam_init)
    acc = acc_s[...]
    o = acc[0:DA_VDIM] * (1.0 / acc[DA_VDIM:DA_VDIM + 1])
    dd = o[:, 0:tq] - lam * o[:, tq:2 * tq]
    ms = jnp.mean(dd * dd, axis=0, keepdims=True)
    y = dd * lax.rsqrt(ms + EPS) * g_ref[...] * (1.0 - lam_init)
    o_ref[0] = y.T.astype(BF16)


def _diff_attention(u3, vt, bias_t, cfar, aug, lam_p, subln_g, lam_init, tq, tk, fixed_shift):
    nb, s, _ = u3.shape
    nq, nk = s // tq, s // tk
    nvar = aug.shape[1] // 2
    kern = functools.partial(_da_kernel, tq=tq, tk=tk, nk=nk, lam_init=lam_init,
                             fixed_shift=fixed_shift)
    return pl.pallas_call(
        kern,
        grid=(nb, DA_HEADS, nq),
        in_specs=[
            pl.BlockSpec((1, tq, DA_VDIM), lambda b, h, i: (b, i, COL_DA_Q // DA_VDIM + h)),
            pl.BlockSpec((1, s, DA_KW), lambda b, h, i: (b, 0, COL_DA_K // DA_KW + h)),
            pl.BlockSpec((1, 1, nk, DA_VROWS, tk), lambda b, h, i: (b, h, 0, 0, 0)),
            pl.BlockSpec((1, 3, tk, 2 * tq), lambda b, h, i: (h, 0, 0, 0)),
            pl.BlockSpec((1, 2, 1, 2 * tq), lambda b, h, i: (h, 0, 0, 0)),
            pl.BlockSpec((1, 2 * nvar, 1, DA_VDIM), lambda b, h, i: (h, 0, 0, 0)),
            pl.BlockSpec((4, HEAD_DIM), lambda b, h, i: (0, 0)),
            pl.BlockSpec((DA_VDIM, 1), lambda b, h, i: (0, 0)),
        ],
        out_specs=pl.BlockSpec((1, tq, DA_VDIM), lambda b, h, i: (b, i, h)),
        out_shape=jax.ShapeDtypeStruct((nb, s, DA_HEADS * DA_VDIM), BF16),
        scratch_shapes=[
            pltpu.VMEM((nvar, 2 * tq, DA_KW), BF16),
            pltpu.VMEM((1, 2 * tq), F32),
            pltpu.VMEM((DA_VROWS, 2 * tq), F32),
        ],
        compiler_params=_cparams(("parallel", "parallel", "parallel")),
        name="diff_attn_fixed" if fixed_shift else "diff_attn_runmax",
    )(u3, u3, vt, bias_t, cfar, aug, lam_p, subln_g)


CONV_HALO = 16
CONV_ROWS = 64


def _conv_kernel(ap_ref, ac_ref, an_ref, gp_ref, gc_ref, gn_ref, w_ref, b_ref, lg_ref, lb_ref,
                 o_ref, z_s, zp_s, *, tc, nt):
    i = pl.program_id(1)

    def glu(a_ref, g_ref):
        return a_ref[0].astype(F32) * jax.nn.sigmoid(g_ref[0].astype(F32))

    z_s[0:CONV_HALO, :] = jnp.where(i > 0, glu(ap_ref, gp_ref), 0.0)
    z_s[CONV_HALO:CONV_HALO + tc, :] = glu(ac_ref, gc_ref)
    z_s[CONV_HALO + tc:2 * CONV_HALO + tc, :] = jnp.where(i < nt - 1, glu(an_ref, gn_ref), 0.0)

    pad = CONV_WIDTH // 2
    span = tc + 3 * SUBLANES
    for p in range(SUBLANES):
        zp_s[p, 0:span, :] = z_s[p:p + span, :]

    def rows(r, carry):
        r0 = pl.multiple_of(r * CONV_ROWS, CONV_ROWS)
        acc = jnp.zeros((CONV_ROWS, CONV_CH), F32) + b_ref[...]
        for t in range(CONV_WIDTH):
            off = CONV_HALO - pad + t
            p = off % SUBLANES
            base = pl.multiple_of(r0 + (off - p), SUBLANES)
            acc = acc + w_ref[t:t + 1, :] * zp_s[p, pl.ds(base, CONV_ROWS), :]
        mu = jnp.mean(acc, axis=-1, keepdims=True)
        xc = acc - mu
        var = jnp.mean(xc * xc, axis=-1, keepdims=True)
        y = xc * lax.rsqrt(var + EPS) * lg_ref[...] + lb_ref[...]
        o_ref[0, pl.ds(r0, CONV_ROWS), :] = (y * jax.nn.sigmoid(y)).astype(BF16)
        return carry

    lax.fori_loop(0, tc // CONV_ROWS, rows, 0)


def _conv_module(u3, conv_w, conv_b, ln_g, ln_b, tc):
    nb, s, _ = u3.shape
    nt = s // tc
    hb = tc // CONV_HALO
    nhb = s // CONV_HALO
    ca = COL_CONV // CONV_CH
    cg = ca + 1

    def prev(col):
        return pl.BlockSpec((1, CONV_HALO, CONV_CH),
                            lambda b, i: (b, jnp.maximum(i * hb - 1, 0), col))

    def cur(col):
        return pl.BlockSpec((1, tc, CONV_CH), lambda b, i: (b, i, col))

    def nxt(col):
        return pl.BlockSpec((1, CONV_HALO, CONV_CH),
                            lambda b, i: (b, jnp.minimum((i + 1) * hb, nhb - 1), col))

    vec = pl.BlockSpec((1, CONV_CH), lambda b, i: (0, 0))
    kern = functools.partial(_conv_kernel, tc=tc, nt=nt)
    return pl.pallas_call(
        kern,
        grid=(nb, nt),
        in_specs=[prev(ca), cur(ca), nxt(ca), prev(cg), cur(cg), nxt(cg),
                  pl.BlockSpec((CONV_WIDTH, CONV_CH), lambda b, i: (0, 0)), vec, vec, vec],
        out_specs=pl.BlockSpec((1, tc, CONV_CH), lambda b, i: (b, i, 0)),
        out_shape=jax.ShapeDtypeStruct((nb, s, CONV_CH), BF16),
        scratch_shapes=[pltpu.VMEM((tc + 2 * CONV_HALO, CONV_CH), F32),
                        pltpu.VMEM((SUBLANES, tc + 3 * SUBLANES, CONV_CH), F32)],
        compiler_params=_cparams(("parallel", "parallel")),
        name="conv_module",
    )(u3, u3, u3, u3, u3, u3, conv_w, conv_b, ln_g, ln_b)


def _wa_kernel(sink_ref, q_ref, kp_ref, kc_ref, kn_ref, vp_ref, vc_ref, vn_ref, bias_ref,
               o_ref, k_s, v_s, *, tw, s_len):
    i = pl.program_id(1)
    k_s[0:BLOCK, :] = kp_ref[0]
    k_s[BLOCK:BLOCK + tw, :] = kc_ref[0]
    k_s[BLOCK + tw:2 * BLOCK + tw, :] = kn_ref[0]
    v_s[0:BLOCK, :] = vp_ref[0]
    v_s[BLOCK:BLOCK + tw, :] = vc_ref[0]
    v_s[BLOCK + tw:2 * BLOCK + tw, :] = vn_ref[0]

    nkey = 3 * BLOCK
    pair = 2 * HEAD_DIM
    lane_q = lax.broadcasted_iota(jnp.int32, (BLOCK, pair), 1)
    lane_v = lax.broadcasted_iota(jnp.int32, (nkey, pair), 1)
    row2 = lax.broadcasted_iota(jnp.int32, (2 * BLOCK, 1), 0)
    col = lax.broadcasted_iota(jnp.int32, (1, nkey), 1)

    def blk(sb, carry):
        r0 = pl.multiple_of(sb * BLOCK, BLOCK)
        kpos = i * tw + r0 - BLOCK + col
        inside = jnp.logical_and(kpos >= 0, kpos < s_len)
        for jg in range(WA_HEADS // 2):
            kvh = (2 * jg) // (WA_HEADS // WA_KV_HEADS)
            qp = q_ref[0, pl.ds(r0, BLOCK), jg * pair:(jg + 1) * pair]
            zq = jnp.zeros_like(qp)
            q2 = jnp.concatenate([jnp.where(lane_q < HEAD_DIM, qp, zq),
                                  jnp.where(lane_q >= HEAD_DIM, qp, zq)], axis=0)
            kd = k_s[pl.ds(r0, nkey), kvh * pair:(kvh + 1) * pair]
            vd = v_s[pl.ds(r0, nkey), kvh * pair:(kvh + 1) * pair]
            sc = lax.dot_general(q2, kd, (((1,), (1,)), ((), ())),
                                 preferred_element_type=F32)
            sc = jnp.where(inside, sc + bias_ref[jg], NEG_INF)
            snk = jnp.where(row2 < BLOCK, sink_ref[2 * jg], sink_ref[2 * jg + 1])
            m = jnp.maximum(jnp.max(sc, axis=-1, keepdims=True), snk)
            e = jnp.exp(sc - m)
            den = jnp.sum(e, axis=-1, keepdims=True) + jnp.exp(snk - m)
            p = (e / den).astype(BF16)
            zv = jnp.zeros_like(vd)
            o = (jnp.dot(p[0:BLOCK], jnp.where(lane_v < HEAD_DIM, vd, zv),
                         preferred_element_type=F32)
                 + jnp.dot(p[BLOCK:2 * BLOCK], jnp.where(lane_v >= HEAD_DIM, vd, zv),
                           preferred_element_type=F32))
            o_ref[0, pl.ds(r0, BLOCK), jg * pair:(jg + 1) * pair] = o.astype(BF16)
        return carry

    lax.fori_loop(0, tw // BLOCK, blk, 0)


def _window_attention(u3, wa_bias, sink, tw):
    nb, s, _ = u3.shape
    nt = s // tw
    bpt = tw // BLOCK
    nblk = s // BLOCK
    width = WA_HEADS * HEAD_DIM
    kvw = 2 * WA_KV_HEADS * HEAD_DIM

    def prev(col):
        return pl.BlockSpec((1, BLOCK, kvw), lambda b, i: (b, jnp.maximum(i * bpt - 1, 0), col))

    def cur(col):
        return pl.BlockSpec((1, tw, kvw), lambda b, i: (b, i, col))

    def nxt(col):
        return pl.BlockSpec((1, BLOCK, kvw),
                            lambda b, i: (b, jnp.minimum((i + 1) * bpt, nblk - 1), col))

    ck, cv = COL_WA_K // kvw, COL_WA_V // kvw
    kern = functools.partial(_wa_kernel, tw=tw, s_len=s)
    return pl.pallas_call(
        kern,
        grid=(nb, nt),
        in_specs=[
            pl.BlockSpec(memory_space=pltpu.SMEM),
            pl.BlockSpec((1, tw, width), lambda b, i: (b, i, COL_WA_Q // width)),
            prev(ck), cur(ck), nxt(ck), prev(cv), cur(cv), nxt(cv),
            pl.BlockSpec((WA_HEADS // 2, 2 * BLOCK, 3 * BLOCK), lambda b, i: (0, 0, 0)),
        ],
        out_specs=pl.BlockSpec((1, tw, width), lambda b, i: (b, i, 0)),
        out_shape=jax.ShapeDtypeStruct((nb, s, width), BF16),
        scratch_shapes=[pltpu.VMEM((tw + 2 * BLOCK, kvw), BF16),
                        pltpu.VMEM((tw + 2 * BLOCK, kvw), BF16)],
        compiler_params=_cparams(("parallel", "parallel")),
        name="window_attn",
    )(sink, u3, u3, u3, u3, u3, u3, u3, wa_bias)


def _memkv_kernel(mem_ref, g_ref, w_ref, gk_ref, k_ref, v_ref):
    hn = _rms(mem_ref[0], g_ref[0]).astype(BF16)
    kv = jnp.dot(hn, w_ref[0], preferred_element_type=F32)
    width = MA_HEADS * MA_HEAD_DIM
    for h in range(MA_HEADS):
        cols = slice(h * MA_HEAD_DIM, (h + 1) * MA_HEAD_DIM)
        k_ref[0, 0, :, cols] = _rms(kv[:, cols], gk_ref[0]).astype(BF16)
    v_ref[0, 0] = kv[:, width:2 * width].astype(BF16)


def _mem_kv(mem, mem_norm_g, w_mem_kv, gk):
    nb, m, _ = mem.shape
    width = MA_HEADS * MA_HEAD_DIM
    out = jax.ShapeDtypeStruct((DEPTH, nb, m, width), BF16)
    return pl.pallas_call(
        _memkv_kernel,
        grid=(DEPTH, nb),
        in_specs=[
            pl.BlockSpec((1, m, D_MODEL), lambda l, b: (b, 0, 0)),
            pl.BlockSpec((1, 1, D_MODEL), lambda l, b: (l, 0, 0)),
            pl.BlockSpec((1, D_MODEL, 2 * width), lambda l, b: (l, 0, 0)),
            pl.BlockSpec((1, 1, MA_HEAD_DIM), lambda l, b: (l, 0, 0)),
        ],
        out_specs=[pl.BlockSpec((1, 1, m, width), lambda l, b: (l, b, 0, 0)),
                   pl.BlockSpec((1, 1, m, width), lambda l, b: (l, b, 0, 0))],
        out_shape=[out, out],
        compiler_params=_cparams(("parallel", "parallel")),
        name="mem_kv",
    )(mem, mem_norm_g, w_mem_kv, gk)


def _ma_kernel(q_ref, k_ref, v_ref, o_ref):
    for h in range(MA_HEADS):
        cols = slice(h * MA_HEAD_DIM, (h + 1) * MA_HEAD_DIM)
        sc = lax.dot_general(q_ref[0, :, cols], k_ref[0, :, cols], (((1,), (1,)), ((), ())),
                             preferred_element_type=F32)
        e = jnp.exp(sc - jnp.max(sc, axis=-1, keepdims=True))
        p = (e / jnp.sum(e, axis=-1, keepdims=True)).astype(BF16)
        o_ref[0, :, cols] = jnp.dot(p, v_ref[0, :, cols],
                                    preferred_element_type=F32).astype(BF16)


def _memory_attention(u3, kmem, vmem, tq):
    nb, s, _ = u3.shape
    m = kmem.shape[1]
    width = MA_HEADS * MA_HEAD_DIM
    return pl.pallas_call(
        _ma_kernel,
        grid=(nb, s // tq),
        in_specs=[
            pl.BlockSpec((1, tq, width), lambda b, i: (b, i, COL_MA_Q // width)),
            pl.BlockSpec((1, m, width), lambda b, i: (b, 0, 0)),
            pl.BlockSpec((1, m, width), lambda b, i: (b, 0, 0)),
        ],
        out_specs=pl.BlockSpec((1, tq, width), lambda b, i: (b, i, 0)),
        out_shape=jax.ShapeDtypeStruct((nb, s, width), BF16),
        compiler_params=_cparams(("parallel", "parallel")),
        name="mem_attn",
    )(u3, kmem, vmem)


def _merge_kernel(x_ref, g1_ref, da_ref, cv_ref, wa_ref, ma_ref, wg_ref, wb_ref, wo_ref, o_ref):
    x = x_ref[...]
    h = _rms(x, g1_ref[...]).astype(BF16)
    merged = None
    for n, br in enumerate((da_ref, cv_ref, wa_ref, ma_ref)):
        logits = jnp.dot(h, wg_ref[:, n * D_MODEL:(n + 1) * D_MODEL], preferred_element_type=F32)
        t = jax.nn.sigmoid(logits) * jnp.dot(br[...], wb_ref[n], preferred_element_type=F32)
        merged = t if merged is None else merged + t
    o_ref[...] = x + jnp.dot(merged.astype(BF16), wo_ref[...], preferred_element_type=F32)


def _merge(x2d, g1, br_da, br_conv, br_wa, br_ma, w_gate, w_branch, w_out, tm):
    t = x2d.shape[0]
    br = pl.BlockSpec((tm, BRANCH_WIDTH), lambda i: (i, 0))
    return pl.pallas_call(
        _merge_kernel,
        grid=(t // tm,),
        in_specs=[
            pl.BlockSpec((tm, D_MODEL), lambda i: (i, 0)),
            _resident((1, D_MODEL), lambda i: (0, 0)),
            br, br, br, br,
            _resident((D_MODEL, N_BRANCHES * D_MODEL), lambda i: (0, 0)),
            _resident((N_BRANCHES, BRANCH_WIDTH, D_MODEL), lambda i: (0, 0, 0)),
            _resident((D_MODEL, D_MODEL), lambda i: (0, 0)),
        ],
        out_specs=pl.BlockSpec((tm, D_MODEL), lambda i: (i, 0)),
        out_shape=jax.ShapeDtypeStruct((t, D_MODEL), F32),
        compiler_params=_cparams(("parallel",)),
        name="merge",
    )(x2d, g1, br_da, br_conv, br_wa, br_ma, w_gate, w_branch, w_out)


FF_CHUNK = 1024


def _ffn_kernel(x_ref, g2_ref, w1_ref, w2_ref, o_ref):
    x = x_ref[...]
    h = _rms(x, g2_ref[...]).astype(BF16)
    acc = x
    for c in range(D_FF // FF_CHUNK):
        cols = slice(c * FF_CHUNK, (c + 1) * FF_CHUNK)
        f = jnp.maximum(jnp.dot(h, w1_ref[:, cols], preferred_element_type=F32), 0.0)
        acc = acc + jnp.dot((f * f).astype(BF16), w2_ref[cols, :], preferred_element_type=F32)
    o_ref[...] = acc


def _ffn(x2d, g2, w1, w2, tm):
    t = x2d.shape[0]
    return pl.pallas_call(
        _ffn_kernel,
        grid=(t // tm,),
        in_specs=[
            pl.BlockSpec((tm, D_MODEL), lambda i: (i, 0)),
            _resident((1, D_MODEL), lambda i: (0, 0)),
            _resident((D_MODEL, D_FF), lambda i: (0, 0)),
            _resident((D_FF, D_MODEL), lambda i: (0, 0)),
        ],
        out_specs=pl.BlockSpec((tm, D_MODEL), lambda i: (i, 0)),
        out_shape=jax.ShapeDtypeStruct((t, D_MODEL), F32),
        compiler_params=_cparams(("parallel",)),
        name="ffn",
    )(x2d, g2, w1, w2)


def _rel_bucket(rel):
    nb = REL_BUCKETS // 2
    max_exact = nb // 2
    ret = jnp.where(rel > 0, nb, 0)
    n = jnp.abs(rel)
    nf = jnp.maximum(n, 1).astype(F32)
    large = max_exact + (jnp.log(nf / max_exact) / math.log(REL_MAX_DIST / max_exact)
                         * (nb - max_exact)).astype(jnp.int32)
    large = jnp.minimum(large, nb - 1)
    return ret + jnp.where(n < max_exact, n, large)


def _lookup(table, bucket):
    out = jnp.zeros(bucket.shape + (table.shape[1],), F32)
    for b in range(REL_BUCKETS):
        out = jnp.where((bucket == b)[..., None], table[b], out)
    return out


def _da_bias_tables(rel_bias, tq, tk):
    table = rel_bias[:, :2 * DA_HEADS].astype(F32) * LOG2E
    kk = jnp.arange(tk)[None, :, None]
    qq = jnp.arange(tq)[None, None, :]
    dd = (jnp.arange(3) - 1)[:, None, None]
    vals = _lookup(table, _rel_bucket(dd * tk + kk - qq))
    vals = vals.reshape(3, tk, tq, DA_HEADS, 2).transpose(3, 0, 1, 4, 2)
    bias_t = vals.reshape(DA_HEADS, 3, tk, 2 * tq)
    nbk = REL_BUCKETS // 2
    far = jnp.stack([table[nbk - 1], table[REL_BUCKETS - 1]])
    far = far.reshape(2, DA_HEADS, 2).transpose(1, 0, 2)
    cfar = jnp.repeat(far, tq, axis=-1).reshape(DA_HEADS, 2, 1, 2 * tq)
    return bias_t, cfar, table


def _split3(v):
    hi = v.astype(BF16).astype(F32)
    mid = (v - hi).astype(BF16).astype(F32)
    lo = (v - hi - mid).astype(BF16).astype(F32)
    return jnp.stack([hi, mid, lo], axis=-1)


def _da_shift_consts(table, gq, gk):
    bound = 1.02 * HEAD_DIM * jnp.max(jnp.abs(gq)) * jnp.max(jnp.abs(gk))
    cmax, cmin = jnp.max(table, axis=0), jnp.min(table, axis=0)
    shift = bound + cmax
    nbk = REL_BUCKETS // 2
    vals = jnp.stack([table[nbk - 1] - shift, -shift, table[REL_BUCKETS - 1] - shift])
    parts = _split3(vals).reshape(3, DA_HEADS, 2, N_AUG).transpose(1, 0, 2, 3)
    aug = jnp.zeros((DA_HEADS, 3, 2, DA_VDIM), F32)
    aug = aug.at[:, :, 0, HEAD_DIM:HEAD_DIM + N_AUG].set(parts[:, :, 0])
    aug = aug.at[:, :, 1, 0:N_AUG].set(parts[:, :, 1])
    safe = 2.0 * bound + jnp.max(cmax - cmin) <= MAX_SHIFT_GAP
    return aug.reshape(DA_HEADS, 6, 1, DA_VDIM), safe


def _wa_bias_table(rel_bias):
    table = rel_bias[:, 2 * DA_HEADS:].astype(F32)
    qoff = jnp.arange(BLOCK)
    koff = jnp.arange(3 * BLOCK) - BLOCK
    rel = koff[None, :] - qoff[:, None]
    bias = _lookup(table, _rel_bucket(rel)).transpose(2, 0, 1)
    bias = jnp.where((jnp.abs(rel) <= WINDOW)[None], bias, NEG_INF)
    return bias.reshape(WA_HEADS // 2, 2 * BLOCK, 3 * BLOCK)


def _dup_heads(w, heads, dim):
    w = w.reshape(w.shape[0], heads, 1, dim)
    return jnp.broadcast_to(w, (w.shape[0], heads, 2, dim)).reshape(w.shape[0], heads * 2 * dim)


def _layer_params(l, w_in, da_qk_g, wa_qk_g, ma_qk_g):
    w = w_in[l]
    wk = w[:, 3072:3200]
    wv = w[:, 3200:3328]

    def spread(a):
        a = a.reshape(a.shape[0], DA_HEADS, 2, HEAD_DIM)
        z = jnp.zeros_like(a[:, :, 0])
        return jnp.stack([a[:, :, 0], z, z, a[:, :, 1]], axis=2).reshape(a.shape[0], DA_HEADS * DA_KW)

    w_ext = jnp.concatenate([
        w[:, 0:512], spread(w[:, 512:1024]), w[:, 1024:2560], w[:, 2560:3072],
        _dup_heads(wk, WA_KV_HEADS, HEAD_DIM), _dup_heads(wv, WA_KV_HEADS, HEAD_DIM),
        w[:, 3328:3840]], axis=1).astype(BF16)
    ones = lambda n: jnp.ones((n,), F32)
    gq = da_qk_g[l, 0] * (HEAD_DIM ** -0.5 * LOG2E)
    gk = da_qk_g[l, 1]
    gain = jnp.concatenate([
        jnp.tile(gq, 2 * DA_HEADS),
        spread(jnp.tile(gk, 2 * DA_HEADS)[None])[0],
        ones(512 + 1024),
        jnp.tile(wa_qk_g[l, 0], WA_HEADS) * HEAD_DIM ** -0.5,
        jnp.tile(wa_qk_g[l, 1], 2 * WA_KV_HEADS),
        ones(256),
        jnp.tile(ma_qk_g[l, 0], MA_HEADS) * MA_HEAD_DIM ** -0.5,
    ]).reshape(1, U_WIDTH).astype(F32)
    w_gate = w[:, GATE_START:].astype(BF16)
    return w_ext, gain, w_gate, gq, gk


def _tile(n, pref):
    return pref if n % pref == 0 else n


def _trunk(x, mem, rel_bias, norm1_g, w_in, da_qk_g, da_lambda, da_subln_g, conv_w, conv_b,
           conv_ln_g, conv_ln_b, wa_qk_g, wa_sink, mem_norm_g, w_mem_kv, ma_qk_g, w_branch, w_out,
           norm2_g, w_ff1, w_ff2, *, tq=512, tm=512):
    nb, s, _ = x.shape
    tq = _tile(s, tq)
    tk = tq
    tm = _tile(nb * s, tm)
    nk = s // tk
    bias_t, cfar, da_table = _da_bias_tables(rel_bias, tq, tk)
    aug_zero = jnp.zeros((DA_HEADS, 2, 1, DA_VDIM), F32)
    wa_bias = _wa_bias_table(rel_bias)
    kmem, vmem = _mem_kv(mem, mem_norm_g.reshape(DEPTH, 1, D_MODEL), w_mem_kv.astype(BF16),
                         ma_qk_g[:, 1].reshape(DEPTH, 1, MA_HEAD_DIM))
    x2d = x.reshape(nb * s, D_MODEL)
    for l in range(DEPTH):
        w_ext, gain, w_gate, gq, gk = _layer_params(l, w_in, da_qk_g, wa_qk_g, ma_qk_g)
        g1 = norm1_g[l].reshape(1, D_MODEL)
        u = _inproj(x2d, g1, w_ext, gain, tm)
        u3 = u.reshape(nb, s, U_WIDTH)
        v = u3[:, :, COL_DA_V:COL_DA_V + DA_HEADS * DA_VDIM]
        vt = v.reshape(nb, nk, tk, DA_HEADS, DA_VDIM).transpose(0, 3, 1, 4, 2)
        ones_row = jnp.zeros((nb, DA_HEADS, nk, SUBLANES, tk), BF16).at[:, :, :, 0].set(1.0)
        vt = jnp.concatenate([vt, ones_row], axis=3)
        lam_init = 0.8 - 0.6 * math.exp(-0.3 * l)
        aug, safe = _da_shift_consts(da_table, gq, gk)
        da_args = (da_lambda[l], da_subln_g[l].reshape(DA_VDIM, 1), lam_init, tq, tk)
        br_da = lax.cond(
            safe,
            lambda u3, vt: _diff_attention(u3, vt, bias_t, cfar, aug, *da_args, True),
            lambda u3, vt: _diff_attention(u3, vt, bias_t, cfar, aug_zero, *da_args, False),
            u3, vt)
        br_conv = _conv_module(u3, conv_w[l], conv_b[l].reshape(1, CONV_CH),
                               conv_ln_g[l].reshape(1, CONV_CH), conv_ln_b[l].reshape(1, CONV_CH),
                               _tile(s, 512))
        br_wa = _window_attention(u3, wa_bias, wa_sink[l], _tile(s, 512))
        br_ma = _memory_attention(u3, kmem[l], vmem[l], _tile(s, 1024))
        flat = lambda a: a.reshape(nb * s, BRANCH_WIDTH)
        x2d = _merge(x2d, g1, flat(br_da), flat(br_conv), flat(br_wa), flat(br_ma), w_gate,
                     w_branch[l].astype(BF16), w_out[l].astype(BF16), tm)
        x2d = _ffn(x2d, norm2_g[l].reshape(1, D_MODEL), w_ff1[l].astype(BF16),
                   w_ff2[l].astype(BF16), tm)
    return x2d.reshape(nb, s, D_MODEL)


def kernel(x_prompt, x_sample, mem_prompt, mem_sample, rel_bias, norm1_g, w_in, da_qk_g, da_lambda, da_subln_g, conv_w, conv_b, conv_ln_g, conv_ln_b, wa_qk_g, wa_sink, mem_norm_g, w_mem_kv, ma_qk_g, w_branch, w_out, norm2_g, w_ff1, w_ff2):
    params = (rel_bias, norm1_g, w_in, da_qk_g, da_lambda, da_subln_g, conv_w, conv_b, conv_ln_g,
              conv_ln_b, wa_qk_g, wa_sink, mem_norm_g, w_mem_kv, ma_qk_g, w_branch, w_out,
              norm2_g, w_ff1, w_ff2)
    if x_prompt.shape[1:] == x_sample.shape[1:]:
        nbp = x_prompt.shape[0]
        y = _trunk(jnp.concatenate([x_prompt, x_sample], axis=0),
                   jnp.concatenate([mem_prompt, mem_sample], axis=0), *params)
        return (y[:nbp], y[nbp:])
    return (_trunk(x_prompt, mem_prompt, *params), _trunk(x_sample, mem_sample, *params))
```

```python
import functools
import math

import jax
import jax.numpy as jnp
from jax import lax
from jax.experimental import pallas as pl
from jax.experimental.pallas import tpu as pltpu

F32 = jnp.float32
BF16 = jnp.bfloat16

D_MODEL = 1024
DEPTH = 4
HEAD_DIM = 64
DA_HEADS = 4
DA_VDIM = 2 * HEAD_DIM
CONV_CH = 512
CONV_WIDTH = 31
WA_HEADS = 8
WA_KV_HEADS = 2
WINDOW = 128
BLOCK = 128
MEM_TOKENS = 256
MA_HEADS = 4
MA_HEAD_DIM = 128
BRANCH_WIDTH = 512
N_BRANCHES = 4
D_FF = 4 * D_MODEL
REL_BUCKETS = 32
REL_MAX_DIST = 128
EPS = 1e-6
NEG_INF = -1e30

U_WIDTH = 4608
COL_DA_Q, COL_DA_K, COL_DA_V, COL_CONV, COL_WA_Q, COL_WA_K, COL_WA_V, COL_MA_Q = (
    0, 512, 1536, 2048, 3072, 3584, 3840, 4096)
DA_KW = 4 * HEAD_DIM
N_AUG = 3
SUB = 256
SUB_SEG = (64, 64, 64, 64, 64, 64, 0, 0, 0, 0, 0, 0, 64, 64, 64, 0, 128, 128)
SUB_DA_K = (2, 3, 4, 5)
GATE_START = 3840
SUBLANES = 8
DA_VROWS = DA_VDIM + SUBLANES
LOG2E = 1.4426950408889634
MAX_SHIFT_GAP = 100.0

VMEM_LIMIT = 56 * 1024 * 1024


def _cparams(sem):
    return pltpu.CompilerParams(dimension_semantics=sem, vmem_limit_bytes=VMEM_LIMIT)


def _resident(shape, index_map):
    return pl.BlockSpec(shape, index_map, pipeline_mode=pl.Buffered(1))


def _rms(x, g):
    return x * lax.rsqrt(jnp.mean(x * x, axis=-1, keepdims=True) + EPS) * g


def _inproj_kernel(x_ref, g1_ref, w_ref, gain_ref, o_ref):
    h = _rms(x_ref[...], g1_ref[...]).astype(BF16)
    r = lax.broadcasted_iota(jnp.int32, (SUB, SUB), 0)
    c = lax.broadcasted_iota(jnp.int32, (SUB, SUB), 1)
    lane = lax.broadcasted_iota(jnp.int32, (1, SUB), 1)
    for s, seg in enumerate(SUB_SEG):
        cols = slice(s * SUB, (s + 1) * SUB)
        u = jnp.dot(h, w_ref[:, cols], preferred_element_type=F32)
        if seg:
            e = jnp.where(r // seg == c // seg, 1.0 / seg, 0.0).astype(BF16)
            ms = jnp.dot((u * u).astype(BF16), e, preferred_element_type=F32)
            u = u * lax.rsqrt(ms + EPS) * gain_ref[:, cols]
        if s in SUB_DA_K:
            ones = jnp.logical_or(
                jnp.logical_and(lane >= HEAD_DIM, lane < HEAD_DIM + N_AUG),
                jnp.logical_and(lane >= 2 * HEAD_DIM, lane < 2 * HEAD_DIM + N_AUG))
            u = jnp.where(ones, 1.0, u)
        o_ref[:, cols] = u.astype(BF16)


def _inproj(x2d, g1, w_ext, gain, tm):
    t = x2d.shape[0]
    return pl.pallas_call(
        _inproj_kernel,
        grid=(t // tm,),
        in_specs=[
            pl.BlockSpec((tm, D_MODEL), lambda i: (i, 0)),
            _resident((1, D_MODEL), lambda i: (0, 0)),
            _resident((D_MODEL, U_WIDTH), lambda i: (0, 0)),
            _resident((1, U_WIDTH), lambda i: (0, 0)),
        ],
        out_specs=pl.BlockSpec((tm, U_WIDTH), lambda i: (i, 0)),
        out_shape=jax.ShapeDtypeStruct((t, U_WIDTH), BF16),
        compiler_params=_cparams(("parallel",)),
        name="inproj",
    )(x2d, g1, w_ext, gain)


def _da_kernel(q_ref, k_ref, vt_ref, bias_ref, cfar_ref, aug_ref, lam_ref, g_ref, o_ref,
               q2_s, m_s, acc_s, *, tq, tk, nk, lam_init, fixed_shift):
    i = pl.program_id(2)
    q = q_ref[0]
    lane = lax.broadcasted_iota(jnp.int32, (tq, 2 * HEAD_DIM), 1)
    first = lane < HEAD_DIM
    zero = jnp.zeros_like(q)
    for var in range(q2_s.shape[0]):
        q2_s[var, 0:tq, 0:DA_VDIM] = jnp.where(first, q, aug_ref[0, 2 * var].astype(BF16))
        q2_s[var, 0:tq, DA_VDIM:DA_KW] = zero
        q2_s[var, tq:2 * tq, 0:DA_VDIM] = zero
        q2_s[var, tq:2 * tq, DA_VDIM:DA_KW] = jnp.where(first, aug_ref[0, 2 * var + 1].astype(BF16), q)
    m_s[...] = jnp.full(m_s.shape, NEG_INF, F32)
    acc_s[...] = jnp.zeros(acc_s.shape, F32)

    def scores(j, var, bias):
        kc = k_ref[0, pl.ds(pl.multiple_of(j * tk, tk), tk), :]
        st = lax.dot_general(kc, q2_s[var], (((1,), (1,)), ((), ())),
                             preferred_element_type=F32)
        return st if bias is None else st + bias

    def add_fixed(chunks):
        pv = None
        for j, var, bias in chunks:
            p = jnp.exp2(scores(j, var, bias)).astype(BF16)
            t = jnp.dot(vt_ref[0, 0, j], p, preferred_element_type=F32)
            pv = t if pv is None else pv + t
        acc_s[...] += pv

    def add_runmax(j, bias, cvec):
        st = scores(j, 0, bias)
        mx = jnp.max(st, axis=0, keepdims=True)
        if cvec is not None:
            mx = mx + cvec
        m_old = m_s[...]
        m_new = jnp.maximum(m_old, mx)
        alpha = jnp.exp2(m_old - m_new)
        shift = m_new if cvec is None else m_new - cvec
        p = jnp.exp2(st - shift).astype(BF16)
        m_s[...] = m_new
        acc_s[...] = alpha * acc_s[...] + jnp.dot(vt_ref[0, 0, j], p, preferred_element_type=F32)

    n_left = jnp.maximum(i - 1, 0)
    first_right = jnp.minimum(i + 2, nk)
    n_far = n_left + nk - first_right

    def far(t):
        left = t < n_left
        return jnp.where(left, t, t - n_left + first_right), jnp.where(left, 0, 1)

    near = [(i + d, jnp.logical_and(i + d >= 0, i + d < nk), bias_ref.at[0, d + 1])
            for d in (-1, 0, 1)]

    if fixed_shift:
        def far_pair(tp, carry):
            (ja, sa), (jb, sb) = far(2 * tp), far(2 * tp + 1)
            add_fixed([(ja, 2 * sa, None), (jb, 2 * sb, None)])
            return carry

        lax.fori_loop(0, n_far // 2, far_pair, 0)

        @pl.when(n_far % 2 == 1)
        def _():
            j, side = far(n_far - 1)
            add_fixed([(j, 2 * side, None)])

        interior = jnp.logical_and(i >= 1, i < nk - 1)

        @pl.when(interior)
        def _():
            add_fixed([(j, 1, b[...]) for j, _, b in near])

        for j, valid, b in near:
            @pl.when(jnp.logical_and(valid, jnp.logical_not(interior)))
            def _():
                add_fixed([(j, 1, b[...])])
    else:
        def far_one(t, carry):
            j, side = far(t)
            add_runmax(j, None, cfar_ref[0, side])
            return carry

        lax.fori_loop(0, n_far, far_one, 0)
        for j, valid, b in near:
            @pl.when(valid)
            def _():
                add_runmax(j, b[...], None)

    lp = lam_ref[...]
    lam = (jnp.exp(jnp.sum(lp[0:1] * lp[1:2], keepdims=True))
           - jnp.exp(jnp.sum(lp[2:3] * lp[3:4], keepdims=True)) + lam_init)
    acc = acc_s[...]
    o = acc[0:DA_VDIM] * (1.0 / acc[DA_VDIM:DA_VDIM + 1])
    dd = o[:, 0:tq] - lam * o[:, tq:2 * tq]
    ms = jnp.mean(dd * dd, axis=0, keepdims=True)
    y = dd * lax.rsqrt(ms + EPS) * g_ref[...] * (1.0 - lam_init)
    o_ref[0] = y.T.astype(BF16)


def _diff_attention(u3, vt, bias_t, cfar, aug, lam_p, subln_g, lam_init, tq, tk, fixed_shift):
    nb, s, _ = u3.shape
    nq, nk = s // tq, s // tk
    nvar = aug.shape[1] // 2
    kern = functools.partial(_da_kernel, tq=tq, tk=tk, nk=nk, lam_init=lam_init,
                             fixed_shift=fixed_shift)
    return pl.pallas_call(
        kern,
        grid=(nb, DA_HEADS, nq),
        in_specs=[
            pl.BlockSpec((1, tq, DA_VDIM), lambda b, h, i: (b, i, COL_DA_Q // DA_VDIM + h)),
            pl.BlockSpec((1, s, DA_KW), lambda b, h, i: (b, 0, COL_DA_K // DA_KW + h)),
            pl.BlockSpec((1, 1, nk, DA_VROWS, tk), lambda b, h, i: (b, h, 0, 0, 0)),
            pl.BlockSpec((1, 3, tk, 2 * tq), lambda b, h, i: (h, 0, 0, 0)),
            pl.BlockSpec((1, 2, 1, 2 * tq), lambda b, h, i: (h, 0, 0, 0)),
            pl.BlockSpec((1, 2 * nvar, 1, DA_VDIM), lambda b, h, i: (h, 0, 0, 0)),
            pl.BlockSpec((4, HEAD_DIM), lambda b, h, i: (0, 0)),
            pl.BlockSpec((DA_VDIM, 1), lambda b, h, i: (0, 0)),
        ],
        out_specs=pl.BlockSpec((1, tq, DA_VDIM), lambda b, h, i: (b, i, h)),
        out_shape=jax.ShapeDtypeStruct((nb, s, DA_HEADS * DA_VDIM), BF16),
        scratch_shapes=[
            pltpu.VMEM((nvar, 2 * tq, DA_KW), BF16),
            pltpu.VMEM((1, 2 * tq), F32),
            pltpu.VMEM((DA_VROWS, 2 * tq), F32),
        ],
        compiler_params=_cparams(("parallel", "parallel", "parallel")),
        name="diff_attn_fixed" if fixed_shift else "diff_attn_runmax",
    )(u3, u3, vt, bias_t, cfar, aug, lam_p, subln_g)


CONV_HALO = 16
CONV_ROWS = 64


def _conv_kernel(ap_ref, ac_ref, an_ref, gp_ref, gc_ref, gn_ref, w_ref, b_ref, lg_ref, lb_ref,
                 o_ref, z_s, zp_s, *, tc, nt):
    i = pl.program_id(1)

    def glu(a_ref, g_ref):
        return a_ref[0].astype(F32) * jax.nn.sigmoid(g_ref[0].astype(F32))

    z_s[0:CONV_HALO, :] = jnp.where(i > 0, glu(ap_ref, gp_ref), 0.0)
    z_s[CONV_HALO:CONV_HALO + tc, :] = glu(ac_ref, gc_ref)
    z_s[CONV_HALO + tc:2 * CONV_HALO + tc, :] = jnp.where(i < nt - 1, glu(an_ref, gn_ref), 0.0)

    pad = CONV_WIDTH // 2
    span = tc + 3 * SUBLANES
    for p in range(SUBLANES):
        zp_s[p, 0:span, :] = z_s[p:p + span, :]

    def rows(r, carry):
        r0 = pl.multiple_of(r * CONV_ROWS, CONV_ROWS)
        acc = jnp.zeros((CONV_ROWS, CONV_CH), F32) + b_ref[...]
        for t in range(CONV_WIDTH):
            off = CONV_HALO - pad + t
            p = off % SUBLANES
            base = pl.multiple_of(r0 + (off - p), SUBLANES)
            acc = acc + w_ref[t:t + 1, :] * zp_s[p, pl.ds(base, CONV_ROWS), :]
        mu = jnp.mean(acc, axis=-1, keepdims=True)
        xc = acc - mu
        var = jnp.mean(xc * xc, axis=-1, keepdims=True)
        y = xc * lax.rsqrt(var + EPS) * lg_ref[...] + lb_ref[...]
        o_ref[0, pl.ds(r0, CONV_ROWS), :] = (y * jax.nn.sigmoid(y)).astype(BF16)
        return carry

    lax.fori_loop(0, tc // CONV_ROWS, rows, 0)


def _conv_module(u3, conv_w, conv_b, ln_g, ln_b, tc):
    nb, s, _ = u3.shape
    nt = s // tc
    hb = tc // CONV_HALO
    nhb = s // CONV_HALO
    ca = COL_CONV // CONV_CH
    cg = ca + 1

    def prev(col):
        return pl.BlockSpec((1, CONV_HALO, CONV_CH),
                            lambda b, i: (b, jnp.maximum(i * hb - 1, 0), col))

    def cur(col):
        return pl.BlockSpec((1, tc, CONV_CH), lambda b, i: (b, i, col))

    def nxt(col):
        return pl.BlockSpec((1, CONV_HALO, CONV_CH),
                            lambda b, i: (b, jnp.minimum((i + 1) * hb, nhb - 1), col))

    vec = pl.BlockSpec((1, CONV_CH), lambda b, i: (0, 0))
    kern = functools.partial(_conv_kernel, tc=tc, nt=nt)
    return pl.pallas_call(
        kern,
        grid=(nb, nt),
        in_specs=[prev(ca), cur(ca), nxt(ca), prev(cg), cur(cg), nxt(cg),
                  pl.BlockSpec((CONV_WIDTH, CONV_CH), lambda b, i: (0, 0)), vec, vec, vec],
        out_specs=pl.BlockSpec((1, tc, CONV_CH), lambda b, i: (b, i, 0)),
        out_shape=jax.ShapeDtypeStruct((nb, s, CONV_CH), BF16),
        scratch_shapes=[pltpu.VMEM((tc + 2 * CONV_HALO, CONV_CH), F32),
                        pltpu.VMEM((SUBLANES, tc + 3 * SUBLANES, CONV_CH), F32)],
        compiler_params=_cparams(("parallel", "parallel")),
        name="conv_module",
    )(u3, u3, u3, u3, u3, u3, conv_w, conv_b, ln_g, ln_b)


def _wa_kernel(sink_ref, q_ref, kp_ref, kc_ref, kn_ref, vp_ref, vc_ref, vn_ref, bias_ref,
               o_ref, k_s, v_s, *, tw, s_len, fixed_shift):
    i = pl.program_id(1)
    nt = pl.num_programs(1)
    k_s[0:BLOCK, :] = kp_ref[0]
    k_s[BLOCK:BLOCK + tw, :] = kc_ref[0]
    k_s[BLOCK + tw:2 * BLOCK + tw, :] = kn_ref[0]

    nkey = 3 * BLOCK
    pair = 2 * HEAD_DIM
    lane_q = lax.broadcasted_iota(jnp.int32, (BLOCK, pair), 1)
    row2 = lax.broadcasted_iota(jnp.int32, (2 * BLOCK, 1), 0)

    if fixed_shift:
        for kvh in range(WA_KV_HEADS):
            cols = slice(kvh * pair, (kvh + 1) * pair)
            parts = ((0, BLOCK, vp_ref, i > 0), (BLOCK, tw, vc_ref, None),
                     (BLOCK + tw, BLOCK, vn_ref, i < nt - 1))
            for r0, n, ref, valid in parts:
                vals = ref[0, :, cols]
                one = jnp.where(lax.broadcasted_iota(jnp.int32, (n, pair), 1) == 0, 1.0, 0.0)
                one = one.astype(BF16)
                if valid is not None:
                    vals = jnp.where(valid, vals, jnp.zeros_like(vals))
                    one = jnp.where(valid, one, jnp.zeros_like(one))
                v_s[kvh, r0:r0 + n, 0:pair] = vals
                v_s[kvh, r0:r0 + n, pair:2 * pair] = one
    else:
        v_s[0:BLOCK, :] = vp_ref[0]
        v_s[BLOCK:BLOCK + tw, :] = vc_ref[0]
        v_s[BLOCK + tw:2 * BLOCK + tw, :] = vn_ref[0]
        lane_v = lax.broadcasted_iota(jnp.int32, (nkey, pair), 1)
        col = lax.broadcasted_iota(jnp.int32, (1, nkey), 1)

    def blk(sb, carry):
        r0 = pl.multiple_of(sb * BLOCK, BLOCK)
        for jg in range(WA_HEADS // 2):
            kvh = (2 * jg) // (WA_HEADS // WA_KV_HEADS)
            qp = q_ref[0, pl.ds(r0, BLOCK), jg * pair:(jg + 1) * pair]
            zq = jnp.zeros_like(qp)
            q2 = jnp.concatenate([jnp.where(lane_q < HEAD_DIM, qp, zq),
                                  jnp.where(lane_q >= HEAD_DIM, qp, zq)], axis=0)
            kd = k_s[pl.ds(r0, nkey), kvh * pair:(kvh + 1) * pair]
            sc = lax.dot_general(q2, kd, (((1,), (1,)), ((), ())),
                                 preferred_element_type=F32)
            snk = jnp.where(row2 < BLOCK, sink_ref[2 * jg], sink_ref[2 * jg + 1])
            if fixed_shift:
                p = jnp.exp2(sc + bias_ref[jg]).astype(BF16)
                oa = jnp.dot(p, v_s[kvh, pl.ds(r0, nkey), :], preferred_element_type=F32)
                den = oa[:, pair:pair + 1] + jnp.exp2(snk)
                o = oa[:, 0:pair] * (1.0 / den)
                o = jnp.where(lane_q < HEAD_DIM, o[0:BLOCK], o[BLOCK:2 * BLOCK])
            else:
                kpos = i * tw + r0 - BLOCK + col
                inside = jnp.logical_and(kpos >= 0, kpos < s_len)
                sc = jnp.where(inside, sc + bias_ref[jg], NEG_INF)
                m = jnp.maximum(jnp.max(sc, axis=-1, keepdims=True), snk)
                e = jnp.exp2(sc - m)
                den = jnp.sum(e, axis=-1, keepdims=True) + jnp.exp2(snk - m)
                p = (e / den).astype(BF16)
                vd = v_s[pl.ds(r0, nkey), kvh * pair:(kvh + 1) * pair]
                zv = jnp.zeros_like(vd)
                o = (jnp.dot(p[0:BLOCK], jnp.where(lane_v < HEAD_DIM, vd, zv),
                             preferred_element_type=F32)
                     + jnp.dot(p[BLOCK:2 * BLOCK], jnp.where(lane_v >= HEAD_DIM, vd, zv),
                               preferred_element_type=F32))
            o_ref[0, pl.ds(r0, BLOCK), jg * pair:(jg + 1) * pair] = o.astype(BF16)
        return carry

    lax.fori_loop(0, tw // BLOCK, blk, 0)


def _window_attention(u3, wa_bias, sink, tw, fixed_shift):
    nb, s, _ = u3.shape
    nt = s // tw
    bpt = tw // BLOCK
    nblk = s // BLOCK
    width = WA_HEADS * HEAD_DIM
    kvw = 2 * WA_KV_HEADS * HEAD_DIM

    def prev(col):
        return pl.BlockSpec((1, BLOCK, kvw), lambda b, i: (b, jnp.maximum(i * bpt - 1, 0), col))

    def cur(col):
        return pl.BlockSpec((1, tw, kvw), lambda b, i: (b, i, col))

    def nxt(col):
        return pl.BlockSpec((1, BLOCK, kvw),
                            lambda b, i: (b, jnp.minimum((i + 1) * bpt, nblk - 1), col))

    ck, cv = COL_WA_K // kvw, COL_WA_V // kvw
    kern = functools.partial(_wa_kernel, tw=tw, s_len=s, fixed_shift=fixed_shift)
    v_scratch = ((WA_KV_HEADS, tw + 2 * BLOCK, 4 * HEAD_DIM) if fixed_shift
                 else (tw + 2 * BLOCK, kvw))
    return pl.pallas_call(
        kern,
        grid=(nb, nt),
        in_specs=[
            pl.BlockSpec(memory_space=pltpu.SMEM),
            pl.BlockSpec((1, tw, width), lambda b, i: (b, i, COL_WA_Q // width)),
            prev(ck), cur(ck), nxt(ck), prev(cv), cur(cv), nxt(cv),
            pl.BlockSpec((WA_HEADS // 2, 2 * BLOCK, 3 * BLOCK), lambda b, i: (0, 0, 0)),
        ],
        out_specs=pl.BlockSpec((1, tw, width), lambda b, i: (b, i, 0)),
        out_shape=jax.ShapeDtypeStruct((nb, s, width), BF16),
        scratch_shapes=[pltpu.VMEM((tw + 2 * BLOCK, kvw), BF16),
                        pltpu.VMEM(v_scratch, BF16)],
        compiler_params=_cparams(("parallel", "parallel")),
        name="window_attn_fixed" if fixed_shift else "window_attn_runmax",
    )(sink, u3, u3, u3, u3, u3, u3, u3, wa_bias)


def _memkv_kernel(mem_ref, g_ref, w_ref, gk_ref, k_ref, v_ref):
    hn = _rms(mem_ref[0], g_ref[0]).astype(BF16)
    kv = jnp.dot(hn, w_ref[0], preferred_element_type=F32)
    width = MA_HEADS * MA_HEAD_DIM
    for h in range(MA_HEADS):
        cols = slice(h * MA_HEAD_DIM, (h + 1) * MA_HEAD_DIM)
        k_ref[0, 0, :, cols] = _rms(kv[:, cols], gk_ref[0]).astype(BF16)
    v_ref[0, 0] = kv[:, width:2 * width].astype(BF16)


def _mem_kv(mem, mem_norm_g, w_mem_kv, gk):
    nb, m, _ = mem.shape
    width = MA_HEADS * MA_HEAD_DIM
    out = jax.ShapeDtypeStruct((DEPTH, nb, m, width), BF16)
    return pl.pallas_call(
        _memkv_kernel,
        grid=(DEPTH, nb),
        in_specs=[
            pl.BlockSpec((1, m, D_MODEL), lambda l, b: (b, 0, 0)),
            pl.BlockSpec((1, 1, D_MODEL), lambda l, b: (l, 0, 0)),
            pl.BlockSpec((1, D_MODEL, 2 * width), lambda l, b: (l, 0, 0)),
            pl.BlockSpec((1, 1, MA_HEAD_DIM), lambda l, b: (l, 0, 0)),
        ],
        out_specs=[pl.BlockSpec((1, 1, m, width), lambda l, b: (l, b, 0, 0)),
                   pl.BlockSpec((1, 1, m, width), lambda l, b: (l, b, 0, 0))],
        out_shape=[out, out],
        compiler_params=_cparams(("parallel", "parallel")),
        name="mem_kv",
    )(mem, mem_norm_g, w_mem_kv, gk)


def _ma_kernel(q_ref, k_ref, v_ref, o_ref):
    for h in range(MA_HEADS):
        cols = slice(h * MA_HEAD_DIM, (h + 1) * MA_HEAD_DIM)
        sc = lax.dot_general(q_ref[0, :, cols], k_ref[0, :, cols], (((1,), (1,)), ((), ())),
                             preferred_element_type=F32)
        e = jnp.exp(sc - jnp.max(sc, axis=-1, keepdims=True))
        p = (e / jnp.sum(e, axis=-1, keepdims=True)).astype(BF16)
        o_ref[0, :, cols] = jnp.dot(p, v_ref[0, :, cols],
                                    preferred_element_type=F32).astype(BF16)


def _memory_attention(u3, kmem, vmem, tq):
    nb, s, _ = u3.shape
    m = kmem.shape[1]
    width = MA_HEADS * MA_HEAD_DIM
    return pl.pallas_call(
        _ma_kernel,
        grid=(nb, s // tq),
        in_specs=[
            pl.BlockSpec((1, tq, width), lambda b, i: (b, i, COL_MA_Q // width)),
            pl.BlockSpec((1, m, width), lambda b, i: (b, 0, 0)),
            pl.BlockSpec((1, m, width), lambda b, i: (b, 0, 0)),
        ],
        out_specs=pl.BlockSpec((1, tq, width), lambda b, i: (b, i, 0)),
        out_shape=jax.ShapeDtypeStruct((nb, s, width), BF16),
        compiler_params=_cparams(("parallel", "parallel")),
        name="mem_attn",
    )(u3, kmem, vmem)


def _merge_kernel(x_ref, g1_ref, da_ref, cv_ref, wa_ref, ma_ref, wg_ref, wb_ref, wo_ref, o_ref):
    x = x_ref[...]
    h = _rms(x, g1_ref[...]).astype(BF16)
    merged = None
    for n, br in enumerate((da_ref, cv_ref, wa_ref, ma_ref)):
        logits = jnp.dot(h, wg_ref[:, n * D_MODEL:(n + 1) * D_MODEL], preferred_element_type=F32)
        t = jax.nn.sigmoid(logits) * jnp.dot(br[...], wb_ref[n], preferred_element_type=F32)
        merged = t if merged is None else merged + t
    o_ref[...] = x + jnp.dot(merged.astype(BF16), wo_ref[...], preferred_element_type=F32)


def _merge(x2d, g1, br_da, br_conv, br_wa, br_ma, w_gate, w_branch, w_out, tm):
    t = x2d.shape[0]
    br = pl.BlockSpec((tm, BRANCH_WIDTH), lambda i: (i, 0))
    return pl.pallas_call(
        _merge_kernel,
        grid=(t // tm,),
        in_specs=[
            pl.BlockSpec((tm, D_MODEL), lambda i: (i, 0)),
            _resident((1, D_MODEL), lambda i: (0, 0)),
            br, br, br, br,
            _resident((D_MODEL, N_BRANCHES * D_MODEL), lambda i: (0, 0)),
            _resident((N_BRANCHES, BRANCH_WIDTH, D_MODEL), lambda i: (0, 0, 0)),
            _resident((D_MODEL, D_MODEL), lambda i: (0, 0)),
        ],
        out_specs=pl.BlockSpec((tm, D_MODEL), lambda i: (i, 0)),
        out_shape=jax.ShapeDtypeStruct((t, D_MODEL), F32),
        compiler_params=_cparams(("parallel",)),
        name="merge",
    )(x2d, g1, br_da, br_conv, br_wa, br_ma, w_gate, w_branch, w_out)


FF_CHUNK = 1024


def _ffn_kernel(x_ref, g2_ref, w1_ref, w2_ref, o_ref):
    x = x_ref[...]
    h = _rms(x, g2_ref[...]).astype(BF16)
    acc = x
    for c in range(D_FF // FF_CHUNK):
        cols = slice(c * FF_CHUNK, (c + 1) * FF_CHUNK)
        f = jnp.maximum(jnp.dot(h, w1_ref[:, cols], preferred_element_type=F32), 0.0)
        acc = acc + jnp.dot((f * f).astype(BF16), w2_ref[cols, :], preferred_element_type=F32)
    o_ref[...] = acc


def _ffn(x2d, g2, w1, w2, tm):
    t = x2d.shape[0]
    return pl.pallas_call(
        _ffn_kernel,
        grid=(t // tm,),
        in_specs=[
            pl.BlockSpec((tm, D_MODEL), lambda i: (i, 0)),
            _resident((1, D_MODEL), lambda i: (0, 0)),
            _resident((D_MODEL, D_FF), lambda i: (0, 0)),
            _resident((D_FF, D_MODEL), lambda i: (0, 0)),
        ],
        out_specs=pl.BlockSpec((tm, D_MODEL), lambda i: (i, 0)),
        out_shape=jax.ShapeDtypeStruct((t, D_MODEL), F32),
        compiler_params=_cparams(("parallel",)),
        name="ffn",
    )(x2d, g2, w1, w2)


def _rel_bucket(rel):
    nb = REL_BUCKETS // 2
    max_exact = nb // 2
    ret = jnp.where(rel > 0, nb, 0)
    n = jnp.abs(rel)
    nf = jnp.maximum(n, 1).astype(F32)
    large = max_exact + (jnp.log(nf / max_exact) / math.log(REL_MAX_DIST / max_exact)
                         * (nb - max_exact)).astype(jnp.int32)
    large = jnp.minimum(large, nb - 1)
    return ret + jnp.where(n < max_exact, n, large)


def _lookup(table, bucket):
    out = jnp.zeros(bucket.shape + (table.shape[1],), F32)
    for b in range(REL_BUCKETS):
        out = jnp.where((bucket == b)[..., None], table[b], out)
    return out


def _da_bias_tables(rel_bias, tq, tk):
    table = rel_bias[:, :2 * DA_HEADS].astype(F32) * LOG2E
    kk = jnp.arange(tk)[None, :, None]
    qq = jnp.arange(tq)[None, None, :]
    dd = (jnp.arange(3) - 1)[:, None, None]
    vals = _lookup(table, _rel_bucket(dd * tk + kk - qq))
    vals = vals.reshape(3, tk, tq, DA_HEADS, 2).transpose(3, 0, 1, 4, 2)
    bias_t = vals.reshape(DA_HEADS, 3, tk, 2 * tq)
    nbk = REL_BUCKETS // 2
    far = jnp.stack([table[nbk - 1], table[REL_BUCKETS - 1]])
    far = far.reshape(2, DA_HEADS, 2).transpose(1, 0, 2)
    cfar = jnp.repeat(far, tq, axis=-1).reshape(DA_HEADS, 2, 1, 2 * tq)
    return bias_t, cfar, table


def _split3(v):
    hi = v.astype(BF16).astype(F32)
    mid = (v - hi).astype(BF16).astype(F32)
    lo = (v - hi - mid).astype(BF16).astype(F32)
    return jnp.stack([hi, mid, lo], axis=-1)


def _da_shift_consts(table, gq, gk):
    bound = 1.02 * HEAD_DIM * jnp.max(jnp.abs(gq)) * jnp.max(jnp.abs(gk))
    cmax, cmin = jnp.max(table, axis=0), jnp.min(table, axis=0)
    shift = bound + cmax
    nbk = REL_BUCKETS // 2
    vals = jnp.stack([table[nbk - 1] - shift, -shift, table[REL_BUCKETS - 1] - shift])
    parts = _split3(vals).reshape(3, DA_HEADS, 2, N_AUG).transpose(1, 0, 2, 3)
    aug = jnp.zeros((DA_HEADS, 3, 2, DA_VDIM), F32)
    aug = aug.at[:, :, 0, HEAD_DIM:HEAD_DIM + N_AUG].set(parts[:, :, 0])
    aug = aug.at[:, :, 1, 0:N_AUG].set(parts[:, :, 1])
    safe = 2.0 * bound + jnp.max(cmax - cmin) <= MAX_SHIFT_GAP
    return aug.reshape(DA_HEADS, 6, 1, DA_VDIM), safe


def _wa_bias_table(rel_bias):
    table = rel_bias[:, 2 * DA_HEADS:].astype(F32) * LOG2E
    qoff = jnp.arange(BLOCK)
    koff = jnp.arange(3 * BLOCK) - BLOCK
    rel = koff[None, :] - qoff[:, None]
    bias = _lookup(table, _rel_bucket(rel)).transpose(2, 0, 1)
    return bias, jnp.abs(rel) <= WINDOW, table


def _wa_consts(bias, in_window, table, sink2, gq, gk):
    def tiles(b):
        b = jnp.where(in_window[None], b, NEG_INF)
        return b.reshape(WA_HEADS // 2, 2 * BLOCK, 3 * BLOCK)

    bound = 1.02 * HEAD_DIM * jnp.max(jnp.abs(gq)) * jnp.max(jnp.abs(gk))
    cmax, cmin = jnp.max(table, axis=0), jnp.min(table, axis=0)
    shift = jnp.maximum(bound + cmax, sink2)
    safe = jnp.max(shift - jnp.maximum(cmin - bound, sink2)) <= MAX_SHIFT_GAP
    return (tiles(bias), sink2), (tiles(bias - shift[:, None, None]), sink2 - shift), safe


def _dup_heads(w, heads, dim):
    w = w.reshape(w.shape[0], heads, 1, dim)
    return jnp.broadcast_to(w, (w.shape[0], heads, 2, dim)).reshape(w.shape[0], heads * 2 * dim)


def _layer_params(l, w_in, da_qk_g, wa_qk_g, ma_qk_g):
    w = w_in[l]
    wk = w[:, 3072:3200]
    wv = w[:, 3200:3328]

    def spread(a):
        a = a.reshape(a.shape[0], DA_HEADS, 2, HEAD_DIM)
        z = jnp.zeros_like(a[:, :, 0])
        return jnp.stack([a[:, :, 0], z, z, a[:, :, 1]], axis=2).reshape(a.shape[0], DA_HEADS * DA_KW)

    w_ext = jnp.concatenate([
        w[:, 0:512], spread(w[:, 512:1024]), w[:, 1024:2560], w[:, 2560:3072],
        _dup_heads(wk, WA_KV_HEADS, HEAD_DIM), _dup_heads(wv, WA_KV_HEADS, HEAD_DIM),
        w[:, 3328:3840]], axis=1).astype(BF16)
    ones = lambda n: jnp.ones((n,), F32)
    gq = da_qk_g[l, 0] * (HEAD_DIM ** -0.5 * LOG2E)
    gk = da_qk_g[l, 1]
    wq = wa_qk_g[l, 0] * (HEAD_DIM ** -0.5 * LOG2E)
    gain = jnp.concatenate([
        jnp.tile(gq, 2 * DA_HEADS),
        spread(jnp.tile(gk, 2 * DA_HEADS)[None])[0],
        ones(512 + 1024),
        jnp.tile(wq, WA_HEADS),
        jnp.tile(wa_qk_g[l, 1], 2 * WA_KV_HEADS),
        ones(256),
        jnp.tile(ma_qk_g[l, 0], MA_HEADS) * MA_HEAD_DIM ** -0.5,
    ]).reshape(1, U_WIDTH).astype(F32)
    w_gate = w[:, GATE_START:].astype(BF16)
    return w_ext, gain, w_gate, (gq, gk), (wq, wa_qk_g[l, 1])


def _tile(n, pref):
    return pref if n % pref == 0 else n


def _trunk(x, mem, rel_bias, norm1_g, w_in, da_qk_g, da_lambda, da_subln_g, conv_w, conv_b,
           conv_ln_g, conv_ln_b, wa_qk_g, wa_sink, mem_norm_g, w_mem_kv, ma_qk_g, w_branch, w_out,
           norm2_g, w_ff1, w_ff2, *, tq=512, tm=512):
    nb, s, _ = x.shape
    tq = _tile(s, tq)
    tk = tq
    tm = _tile(nb * s, tm)
    nk = s // tk
    bias_t, cfar, da_table = _da_bias_tables(rel_bias, tq, tk)
    aug_zero = jnp.zeros((DA_HEADS, 2, 1, DA_VDIM), F32)
    wa_tables = _wa_bias_table(rel_bias)
    kmem, vmem = _mem_kv(mem, mem_norm_g.reshape(DEPTH, 1, D_MODEL), w_mem_kv.astype(BF16),
                         ma_qk_g[:, 1].reshape(DEPTH, 1, MA_HEAD_DIM))
    x2d = x.reshape(nb * s, D_MODEL)
    for l in range(DEPTH):
        w_ext, gain, w_gate, da_g, wa_g = _layer_params(l, w_in, da_qk_g, wa_qk_g, ma_qk_g)
        g1 = norm1_g[l].reshape(1, D_MODEL)
        u = _inproj(x2d, g1, w_ext, gain, tm)
        u3 = u.reshape(nb, s, U_WIDTH)
        v = u3[:, :, COL_DA_V:COL_DA_V + DA_HEADS * DA_VDIM]
        vt = v.reshape(nb, nk, tk, DA_HEADS, DA_VDIM).transpose(0, 3, 1, 4, 2)
        ones_row = jnp.zeros((nb, DA_HEADS, nk, SUBLANES, tk), BF16).at[:, :, :, 0].set(1.0)
        vt = jnp.concatenate([vt, ones_row], axis=3)
        lam_init = 0.8 - 0.6 * math.exp(-0.3 * l)
        aug, safe = _da_shift_consts(da_table, *da_g)
        da_args = (da_lambda[l], da_subln_g[l].reshape(DA_VDIM, 1), lam_init, tq, tk)
        br_da = lax.cond(
            safe,
            lambda u3, vt: _diff_attention(u3, vt, bias_t, cfar, aug, *da_args, True),
            lambda u3, vt: _diff_attention(u3, vt, bias_t, cfar, aug_zero, *da_args, False),
            u3, vt)
        br_conv = _conv_module(u3, conv_w[l], conv_b[l].reshape(1, CONV_CH),
                               conv_ln_g[l].reshape(1, CONV_CH), conv_ln_b[l].reshape(1, CONV_CH),
                               _tile(s, 512))
        wa_plain, wa_shifted, wa_safe = _wa_consts(*wa_tables, wa_sink[l].astype(F32) * LOG2E,
                                                   *wa_g)
        tw = _tile(s, 512)
        br_wa = lax.cond(
            wa_safe,
            lambda u3: _window_attention(u3, *wa_shifted, tw, True),
            lambda u3: _window_attention(u3, *wa_plain, tw, False),
            u3)
        br_ma = _memory_attention(u3, kmem[l], vmem[l], _tile(s, 1024))
        flat = lambda a: a.reshape(nb * s, BRANCH_WIDTH)
        x2d = _merge(x2d, g1, flat(br_da), flat(br_conv), flat(br_wa), flat(br_ma), w_gate,
                     w_branch[l].astype(BF16), w_out[l].astype(BF16), tm)
        x2d = _ffn(x2d, norm2_g[l].reshape(1, D_MODEL), w_ff1[l].astype(BF16),
                   w_ff2[l].astype(BF16), tm)
    return x2d.reshape(nb, s, D_MODEL)


def kernel(x_prompt, x_sample, mem_prompt, mem_sample, rel_bias, norm1_g, w_in, da_qk_g, da_lambda, da_subln_g, conv_w, conv_b, conv_ln_g, conv_ln_b, wa_qk_g, wa_sink, mem_norm_g, w_mem_kv, ma_qk_g, w_branch, w_out, norm2_g, w_ff1, w_ff2):
    params = (rel_bias, norm1_g, w_in, da_qk_g, da_lambda, da_subln_g, conv_w, conv_b, conv_ln_g,
              conv_ln_b, wa_qk_g, wa_sink, mem_norm_g, w_mem_kv, ma_qk_g, w_branch, w_out,
              norm2_g, w_ff1, w_ff2)
    if x_prompt.shape[1:] == x_sample.shape[1:]:
        nbp = x_prompt.shape[0]
        y = _trunk(jnp.concatenate([x_prompt, x_sample], axis=0),
                   jnp.concatenate([mem_prompt, mem_sample], axis=0), *params)
        return (y[:nbp], y[nbp:])
    return (_trunk(x_prompt, mem_prompt, *params), _trunk(x_sample, mem_sample, *params))
```

```python
import functools
import math

import jax
import jax.numpy as jnp
from jax import lax
from jax.experimental import pallas as pl
from jax.experimental.pallas import tpu as pltpu

F32 = jnp.float32
BF16 = jnp.bfloat16

D_MODEL = 1024
DEPTH = 4
HEAD_DIM = 64
DA_HEADS = 4
DA_VDIM = 2 * HEAD_DIM
CONV_CH = 512
CONV_WIDTH = 31
WA_HEADS = 8
WA_KV_HEADS = 2
WINDOW = 128
BLOCK = 128
MEM_TOKENS = 256
MA_HEADS = 4
MA_HEAD_DIM = 128
BRANCH_WIDTH = 512
N_BRANCHES = 4
D_FF = 4 * D_MODEL
REL_BUCKETS = 32
REL_MAX_DIST = 128
EPS = 1e-6
NEG_INF = -1e30

U_WIDTH = 4608
COL_DA_Q, COL_DA_K, COL_DA_V, COL_CONV, COL_WA_Q, COL_WA_K, COL_WA_V, COL_MA_Q = (
    0, 512, 1536, 2048, 3072, 3584, 3840, 4096)
DA_KW = 4 * HEAD_DIM
N_AUG = 3
SUB = 256
MXU_COUNT = 2
SUB_SEG = (64, 64, 64, 64, 64, 64, 0, 0, 0, 0, 0, 0, 64, 64, 64, 0, 128, 128)
SUB_DA_K = (2, 3, 4, 5)
GATE_START = 3840
SUBLANES = 8
DA_VROWS = DA_VDIM + SUBLANES
DA_GROUP = 4
LOG2E = 1.4426950408889634
MAX_SHIFT_GAP = 100.0

VMEM_LIMIT = 56 * 1024 * 1024


def _cparams(sem):
    return pltpu.CompilerParams(dimension_semantics=sem, vmem_limit_bytes=VMEM_LIMIT)


def _resident(shape, index_map):
    return pl.BlockSpec(shape, index_map, pipeline_mode=pl.Buffered(1))


def _rms(x, g):
    return x * lax.rsqrt(jnp.mean(x * x, axis=-1, keepdims=True) + EPS) * g


def _inproj_kernel(x_ref, g1_ref, w_ref, gain_ref, o_ref, vt_ref):
    tk = vt_ref.shape[-1]
    ones_rows = jnp.where(lax.broadcasted_iota(jnp.int32, (SUBLANES, tk), 0) == 0, 1.0, 0.0)
    h = _rms(x_ref[...], g1_ref[...]).astype(BF16)
    r = lax.broadcasted_iota(jnp.int32, (SUB, SUB), 0)
    c = lax.broadcasted_iota(jnp.int32, (SUB, SUB), 1)
    lane = lax.broadcasted_iota(jnp.int32, (1, SUB), 1)
    for s, seg in enumerate(SUB_SEG):
        cols = slice(s * SUB, (s + 1) * SUB)
        if s % MXU_COUNT == 0:
            wide = jnp.dot(h, w_ref[:, s * SUB:(s + MXU_COUNT) * SUB], preferred_element_type=F32)
        u = wide[:, (s % MXU_COUNT) * SUB:(s % MXU_COUNT + 1) * SUB]
        if seg:
            e = jnp.where(r // seg == c // seg, 1.0 / seg, 0.0).astype(BF16)
            ms = jnp.dot((u * u).astype(BF16), e, preferred_element_type=F32)
            u = u * lax.rsqrt(ms + EPS) * gain_ref[:, cols]
        if s in SUB_DA_K:
            ones = jnp.logical_or(
                jnp.logical_and(lane >= HEAD_DIM, lane < HEAD_DIM + N_AUG),
                jnp.logical_and(lane >= 2 * HEAD_DIM, lane < 2 * HEAD_DIM + N_AUG))
            u = jnp.where(ones, 1.0, u)
        o_ref[:, cols] = u.astype(BF16)
        if s * SUB in range(COL_DA_V, COL_DA_V + DA_HEADS * DA_VDIM, SUB):
            for hh in range(SUB // DA_VDIM):
                head = (s * SUB - COL_DA_V) // DA_VDIM + hh
                for ch in range(vt_ref.shape[2]):
                    blk = u[ch * tk:(ch + 1) * tk, hh * DA_VDIM:(hh + 1) * DA_VDIM]
                    vt_ref[0, head, ch, 0:DA_VDIM, :] = blk.T.astype(BF16)
                    vt_ref[0, head, ch, DA_VDIM:DA_VROWS, :] = ones_rows.astype(BF16)


def _inproj(x2d, g1, w_ext, gain, tm, nb, tk):
    t = x2d.shape[0]
    s = t // nb
    assert tm % tk == 0 and s % tm == 0
    tiles, cpt = s // tm, tm // tk
    return pl.pallas_call(
        _inproj_kernel,
        grid=(t // tm,),
        in_specs=[
            pl.BlockSpec((tm, D_MODEL), lambda i: (i, 0)),
            _resident((1, D_MODEL), lambda i: (0, 0)),
            _resident((D_MODEL, U_WIDTH), lambda i: (0, 0)),
            _resident((1, U_WIDTH), lambda i: (0, 0)),
        ],
        out_specs=[
            pl.BlockSpec((tm, U_WIDTH), lambda i: (i, 0)),
            pl.BlockSpec((1, DA_HEADS, cpt, DA_VROWS, tk),
                         lambda i: (i // tiles, 0, i % tiles, 0, 0)),
        ],
        out_shape=[jax.ShapeDtypeStruct((t, U_WIDTH), BF16),
                   jax.ShapeDtypeStruct((nb, DA_HEADS, s // tk, DA_VROWS, tk), BF16)],
        compiler_params=_cparams(("parallel",)),
        name="inproj",
    )(x2d, g1, w_ext, gain)


def _da_kernel(q_ref, k_ref, vt_ref, bias_ref, cfar_ref, aug_ref, lam_ref, g_ref, o_ref,
               q2_s, m_s, acc_s, *, tq, tk, nk, lam_init, fixed_shift):
    i = pl.program_id(2)
    q = q_ref[0]
    lane = lax.broadcasted_iota(jnp.int32, (tq, 2 * HEAD_DIM), 1)
    first = lane < HEAD_DIM
    zero = jnp.zeros_like(q)
    for var in range(q2_s.shape[0]):
        q2_s[var, 0:tq, 0:DA_VDIM] = jnp.where(first, q, aug_ref[0, 2 * var].astype(BF16))
        q2_s[var, 0:tq, DA_VDIM:DA_KW] = zero
        q2_s[var, tq:2 * tq, 0:DA_VDIM] = zero
        q2_s[var, tq:2 * tq, DA_VDIM:DA_KW] = jnp.where(first, aug_ref[0, 2 * var + 1].astype(BF16), q)
    m_s[...] = jnp.full(m_s.shape, NEG_INF, F32)
    acc_s[...] = jnp.zeros(acc_s.shape, F32)

    def scores(j):
        d = j - i
        kind = jnp.where(d < -1, 0, jnp.where(d > 1, 2, 1))
        tile = jnp.where(kind == 1, d + 2, 0)
        kc = k_ref[0, pl.ds(pl.multiple_of(j * tk, tk), tk), :]
        st = lax.dot_general(kc, q2_s[kind if fixed_shift else 0], (((1,), (1,)), ((), ())),
                             preferred_element_type=F32)
        return st + bias_ref[0, tile], cfar_ref[0, kind]

    def group(g, carry):
        pv = None
        for c in range(group_size):
            j = g * group_size + c
            st, cvec = scores(j)
            if fixed_shift:
                p = jnp.exp2(st).astype(BF16)
            else:
                m_old = m_s[...]
                m_new = jnp.maximum(m_old, jnp.max(st, axis=0, keepdims=True) + cvec)
                alpha = jnp.exp2(m_old - m_new)
                p = jnp.exp2(st - (m_new - cvec)).astype(BF16)
                m_s[...] = m_new
                acc_s[...] = alpha * acc_s[...]
            t = jnp.dot(vt_ref[0, 0, j], p, preferred_element_type=F32)
            if fixed_shift:
                pv = t if pv is None else pv + t
            else:
                acc_s[...] += t
        if fixed_shift:
            acc_s[...] += pv
        return carry

    group_size = max(c for c in (1, 2, DA_GROUP) if nk % c == 0) if fixed_shift else 1
    lax.fori_loop(0, nk // group_size, group, 0)

    lp = lam_ref[...]
    lam = (jnp.exp(jnp.sum(lp[0:1] * lp[1:2], keepdims=True))
           - jnp.exp(jnp.sum(lp[2:3] * lp[3:4], keepdims=True)) + lam_init)
    acc = acc_s[...]
    o = acc[0:DA_VDIM] * (1.0 / acc[DA_VDIM:DA_VDIM + 1])
    dd = o[:, 0:tq] - lam * o[:, tq:2 * tq]
    ms = jnp.mean(dd * dd, axis=0, keepdims=True)
    y = dd * lax.rsqrt(ms + EPS) * g_ref[...] * (1.0 - lam_init)
    o_ref[0] = y.T.astype(BF16)


def _diff_attention(u3, vt, bias_t, cfar, aug, lam_p, subln_g, lam_init, tq, tk, fixed_shift):
    nb, s, _ = u3.shape
    nq, nk = s // tq, s // tk
    nvar = aug.shape[1] // 2
    kern = functools.partial(_da_kernel, tq=tq, tk=tk, nk=nk, lam_init=lam_init,
                             fixed_shift=fixed_shift)
    return pl.pallas_call(
        kern,
        grid=(nb, DA_HEADS, nq),
        in_specs=[
            pl.BlockSpec((1, tq, DA_VDIM), lambda b, h, i: (b, i, COL_DA_Q // DA_VDIM + h)),
            pl.BlockSpec((1, s, DA_KW), lambda b, h, i: (b, 0, COL_DA_K // DA_KW + h)),
            pl.BlockSpec((1, 1, nk, DA_VROWS, tk), lambda b, h, i: (b, h, 0, 0, 0)),
            pl.BlockSpec((1, 4, tk, 2 * tq), lambda b, h, i: (h, 0, 0, 0)),
            pl.BlockSpec((1, 3, 1, 2 * tq), lambda b, h, i: (h, 0, 0, 0)),
            pl.BlockSpec((1, 2 * nvar, 1, DA_VDIM), lambda b, h, i: (h, 0, 0, 0)),
            pl.BlockSpec((4, HEAD_DIM), lambda b, h, i: (0, 0)),
            pl.BlockSpec((DA_VDIM, 1), lambda b, h, i: (0, 0)),
        ],
        out_specs=pl.BlockSpec((1, tq, DA_VDIM), lambda b, h, i: (b, i, h)),
        out_shape=jax.ShapeDtypeStruct((nb, s, DA_HEADS * DA_VDIM), BF16),
        scratch_shapes=[
            pltpu.VMEM((nvar, 2 * tq, DA_KW), BF16),
            pltpu.VMEM((1, 2 * tq), F32),
            pltpu.VMEM((DA_VROWS, 2 * tq), F32),
        ],
        compiler_params=_cparams(("parallel", "parallel", "parallel")),
        name="diff_attn_fixed" if fixed_shift else "diff_attn_runmax",
    )(u3, u3, vt, bias_t, cfar, aug, lam_p, subln_g)


CONV_HALO = 16
CONV_ROWS = 64


def _conv_kernel(ap_ref, ac_ref, an_ref, gp_ref, gc_ref, gn_ref, w_ref, b_ref, lg_ref, lb_ref,
                 o_ref, z_s, zp_s, *, tc, nt):
    i = pl.program_id(1)

    def glu(a_ref, g_ref):
        return a_ref[0].astype(F32) * jax.nn.sigmoid(g_ref[0].astype(F32))

    z_s[0:CONV_HALO, :] = jnp.where(i > 0, glu(ap_ref, gp_ref), 0.0)
    z_s[CONV_HALO:CONV_HALO + tc, :] = glu(ac_ref, gc_ref)
    z_s[CONV_HALO + tc:2 * CONV_HALO + tc, :] = jnp.where(i < nt - 1, glu(an_ref, gn_ref), 0.0)

    pad = CONV_WIDTH // 2
    span = tc + 3 * SUBLANES
    for p in range(SUBLANES):
        zp_s[p, 0:span, :] = z_s[p:p + span, :]

    def rows(r, carry):
        r0 = pl.multiple_of(r * CONV_ROWS, CONV_ROWS)
        acc = jnp.zeros((CONV_ROWS, CONV_CH), F32) + b_ref[...]
        for t in range(CONV_WIDTH):
            off = CONV_HALO - pad + t
            p = off % SUBLANES
            base = pl.multiple_of(r0 + (off - p), SUBLANES)
            acc = acc + w_ref[t:t + 1, :] * zp_s[p, pl.ds(base, CONV_ROWS), :]
        mu = jnp.mean(acc, axis=-1, keepdims=True)
        xc = acc - mu
        var = jnp.mean(xc * xc, axis=-1, keepdims=True)
        y = xc * lax.rsqrt(var + EPS) * lg_ref[...] + lb_ref[...]
        o_ref[0, pl.ds(r0, CONV_ROWS), :] = (y * jax.nn.sigmoid(y)).astype(BF16)
        return carry

    lax.fori_loop(0, tc // CONV_ROWS, rows, 0)


def _conv_module(u3, conv_w, conv_b, ln_g, ln_b, tc):
    nb, s, _ = u3.shape
    nt = s // tc
    hb = tc // CONV_HALO
    nhb = s // CONV_HALO
    ca = COL_CONV // CONV_CH
    cg = ca + 1

    def prev(col):
        return pl.BlockSpec((1, CONV_HALO, CONV_CH),
                            lambda b, i: (b, jnp.maximum(i * hb - 1, 0), col))

    def cur(col):
        return pl.BlockSpec((1, tc, CONV_CH), lambda b, i: (b, i, col))

    def nxt(col):
        return pl.BlockSpec((1, CONV_HALO, CONV_CH),
                            lambda b, i: (b, jnp.minimum((i + 1) * hb, nhb - 1), col))

    vec = pl.BlockSpec((1, CONV_CH), lambda b, i: (0, 0))
    kern = functools.partial(_conv_kernel, tc=tc, nt=nt)
    return pl.pallas_call(
        kern,
        grid=(nb, nt),
        in_specs=[prev(ca), cur(ca), nxt(ca), prev(cg), cur(cg), nxt(cg),
                  pl.BlockSpec((CONV_WIDTH, CONV_CH), lambda b, i: (0, 0)), vec, vec, vec],
        out_specs=pl.BlockSpec((1, tc, CONV_CH), lambda b, i: (b, i, 0)),
        out_shape=jax.ShapeDtypeStruct((nb, s, CONV_CH), BF16),
        scratch_shapes=[pltpu.VMEM((tc + 2 * CONV_HALO, CONV_CH), F32),
                        pltpu.VMEM((SUBLANES, tc + 3 * SUBLANES, CONV_CH), F32)],
        compiler_params=_cparams(("parallel", "parallel")),
        name="conv_module",
    )(u3, u3, u3, u3, u3, u3, conv_w, conv_b, ln_g, ln_b)


def _wa_kernel(sink_ref, q_ref, kp_ref, kc_ref, kn_ref, vp_ref, vc_ref, vn_ref, bias_ref,
               o_ref, k_s, v_s, *, tw, s_len, fixed_shift):
    i = pl.program_id(1)
    nt = pl.num_programs(1)
    k_s[0:BLOCK, :] = kp_ref[0]
    k_s[BLOCK:BLOCK + tw, :] = kc_ref[0]
    k_s[BLOCK + tw:2 * BLOCK + tw, :] = kn_ref[0]

    nkey = 3 * BLOCK
    pair = 2 * HEAD_DIM
    lane_q = lax.broadcasted_iota(jnp.int32, (BLOCK, pair), 1)
    row2 = lax.broadcasted_iota(jnp.int32, (2 * BLOCK, 1), 0)

    if fixed_shift:
        for kvh in range(WA_KV_HEADS):
            cols = slice(kvh * pair, (kvh + 1) * pair)
            parts = ((0, BLOCK, vp_ref, i > 0), (BLOCK, tw, vc_ref, None),
                     (BLOCK + tw, BLOCK, vn_ref, i < nt - 1))
            for r0, n, ref, valid in parts:
                vals = ref[0, :, cols]
                one = jnp.where(lax.broadcasted_iota(jnp.int32, (n, pair), 1) == 0, 1.0, 0.0)
                one = one.astype(BF16)
                if valid is not None:
                    vals = jnp.where(valid, vals, jnp.zeros_like(vals))
                    one = jnp.where(valid, one, jnp.zeros_like(one))
                v_s[kvh, r0:r0 + n, 0:pair] = vals
                v_s[kvh, r0:r0 + n, pair:2 * pair] = one
    else:
        v_s[0:BLOCK, :] = vp_ref[0]
        v_s[BLOCK:BLOCK + tw, :] = vc_ref[0]
        v_s[BLOCK + tw:2 * BLOCK + tw, :] = vn_ref[0]
        lane_v = lax.broadcasted_iota(jnp.int32, (nkey, pair), 1)
        col = lax.broadcasted_iota(jnp.int32, (1, nkey), 1)

    def blk(sb, carry):
        r0 = pl.multiple_of(sb * BLOCK, BLOCK)
        for jg in range(WA_HEADS // 2):
            kvh = (2 * jg) // (WA_HEADS // WA_KV_HEADS)
            qp = q_ref[0, pl.ds(r0, BLOCK), jg * pair:(jg + 1) * pair]
            zq = jnp.zeros_like(qp)
            q2 = jnp.concatenate([jnp.where(lane_q < HEAD_DIM, qp, zq),
                                  jnp.where(lane_q >= HEAD_DIM, qp, zq)], axis=0)
            kd = k_s[pl.ds(r0, nkey), kvh * pair:(kvh + 1) * pair]
            sc = lax.dot_general(q2, kd, (((1,), (1,)), ((), ())),
                                 preferred_element_type=F32)
            snk = jnp.where(row2 < BLOCK, sink_ref[2 * jg], sink_ref[2 * jg + 1])
            if fixed_shift:
                p = jnp.exp2(sc + bias_ref[jg]).astype(BF16)
                oa = jnp.dot(p, v_s[kvh, pl.ds(r0, nkey), :], preferred_element_type=F32)
                den = oa[:, pair:pair + 1] + jnp.exp2(snk)
                o = oa[:, 0:pair] * (1.0 / den)
                o = jnp.where(lane_q < HEAD_DIM, o[0:BLOCK], o[BLOCK:2 * BLOCK])
            else:
                kpos = i * tw + r0 - BLOCK + col
                inside = jnp.logical_and(kpos >= 0, kpos < s_len)
                sc = jnp.where(inside, sc + bias_ref[jg], NEG_INF)
                m = jnp.maximum(jnp.max(sc, axis=-1, keepdims=True), snk)
                e = jnp.exp2(sc - m)
                den = jnp.sum(e, axis=-1, keepdims=True) + jnp.exp2(snk - m)
                p = (e / den).astype(BF16)
                vd = v_s[pl.ds(r0, nkey), kvh * pair:(kvh + 1) * pair]
                zv = jnp.zeros_like(vd)
                o = (jnp.dot(p[0:BLOCK], jnp.where(lane_v < HEAD_DIM, vd, zv),
                             preferred_element_type=F32)
                     + jnp.dot(p[BLOCK:2 * BLOCK], jnp.where(lane_v >= HEAD_DIM, vd, zv),
                               preferred_element_type=F32))
            o_ref[0, pl.ds(r0, BLOCK), jg * pair:(jg + 1) * pair] = o.astype(BF16)
        return carry

    lax.fori_loop(0, tw // BLOCK, blk, 0)


def _window_attention(u3, wa_bias, sink, tw, fixed_shift):
    nb, s, _ = u3.shape
    nt = s // tw
    bpt = tw // BLOCK
    nblk = s // BLOCK
    width = WA_HEADS * HEAD_DIM
    kvw = 2 * WA_KV_HEADS * HEAD_DIM

    def prev(col):
        return pl.BlockSpec((1, BLOCK, kvw), lambda b, i: (b, jnp.maximum(i * bpt - 1, 0), col))

    def cur(col):
        return pl.BlockSpec((1, tw, kvw), lambda b, i: (b, i, col))

    def nxt(col):
        return pl.BlockSpec((1, BLOCK, kvw),
                            lambda b, i: (b, jnp.minimum((i + 1) * bpt, nblk - 1), col))

    ck, cv = COL_WA_K // kvw, COL_WA_V // kvw
    kern = functools.partial(_wa_kernel, tw=tw, s_len=s, fixed_shift=fixed_shift)
    v_scratch = ((WA_KV_HEADS, tw + 2 * BLOCK, 4 * HEAD_DIM) if fixed_shift
                 else (tw + 2 * BLOCK, kvw))
    return pl.pallas_call(
        kern,
        grid=(nb, nt),
        in_specs=[
            pl.BlockSpec(memory_space=pltpu.SMEM),
            pl.BlockSpec((1, tw, width), lambda b, i: (b, i, COL_WA_Q // width)),
            prev(ck), cur(ck), nxt(ck), prev(cv), cur(cv), nxt(cv),
            pl.BlockSpec((WA_HEADS // 2, 2 * BLOCK, 3 * BLOCK), lambda b, i: (0, 0, 0)),
        ],
        out_specs=pl.BlockSpec((1, tw, width), lambda b, i: (b, i, 0)),
        out_shape=jax.ShapeDtypeStruct((nb, s, width), BF16),
        scratch_shapes=[pltpu.VMEM((tw + 2 * BLOCK, kvw), BF16),
                        pltpu.VMEM(v_scratch, BF16)],
        compiler_params=_cparams(("parallel", "parallel")),
        name="window_attn_fixed" if fixed_shift else "window_attn_runmax",
    )(sink, u3, u3, u3, u3, u3, u3, u3, wa_bias)


def _memkv_kernel(mem_ref, g_ref, w_ref, gk_ref, k_ref, v_ref):
    hn = _rms(mem_ref[0], g_ref[0]).astype(BF16)
    kv = jnp.dot(hn, w_ref[0], preferred_element_type=F32)
    width = MA_HEADS * MA_HEAD_DIM
    for h in range(MA_HEADS):
        cols = slice(h * MA_HEAD_DIM, (h + 1) * MA_HEAD_DIM)
        k_ref[0, 0, :, cols] = _rms(kv[:, cols], gk_ref[0]).astype(BF16)
    v_ref[0, 0] = kv[:, width:2 * width].astype(BF16)


def _mem_kv(mem, mem_norm_g, w_mem_kv, gk):
    nb, m, _ = mem.shape
    width = MA_HEADS * MA_HEAD_DIM
    out = jax.ShapeDtypeStruct((DEPTH, nb, m, width), BF16)
    return pl.pallas_call(
        _memkv_kernel,
        grid=(DEPTH, nb),
        in_specs=[
            pl.BlockSpec((1, m, D_MODEL), lambda l, b: (b, 0, 0)),
            pl.BlockSpec((1, 1, D_MODEL), lambda l, b: (l, 0, 0)),
            pl.BlockSpec((1, D_MODEL, 2 * width), lambda l, b: (l, 0, 0)),
            pl.BlockSpec((1, 1, MA_HEAD_DIM), lambda l, b: (l, 0, 0)),
        ],
        out_specs=[pl.BlockSpec((1, 1, m, width), lambda l, b: (l, b, 0, 0)),
                   pl.BlockSpec((1, 1, m, width), lambda l, b: (l, b, 0, 0))],
        out_shape=[out, out],
        compiler_params=_cparams(("parallel", "parallel")),
        name="mem_kv",
    )(mem, mem_norm_g, w_mem_kv, gk)


def _ma_kernel(q_ref, k_ref, v_ref, o_ref):
    for h in range(MA_HEADS):
        cols = slice(h * MA_HEAD_DIM, (h + 1) * MA_HEAD_DIM)
        sc = lax.dot_general(q_ref[0, :, cols], k_ref[0, :, cols], (((1,), (1,)), ((), ())),
                             preferred_element_type=F32)
        e = jnp.exp(sc - jnp.max(sc, axis=-1, keepdims=True))
        p = (e / jnp.sum(e, axis=-1, keepdims=True)).astype(BF16)
        o_ref[0, :, cols] = jnp.dot(p, v_ref[0, :, cols],
                                    preferred_element_type=F32).astype(BF16)


def _memory_attention(u3, kmem, vmem, tq):
    nb, s, _ = u3.shape
    m = kmem.shape[1]
    width = MA_HEADS * MA_HEAD_DIM
    return pl.pallas_call(
        _ma_kernel,
        grid=(nb, s // tq),
        in_specs=[
            pl.BlockSpec((1, tq, width), lambda b, i: (b, i, COL_MA_Q // width)),
            pl.BlockSpec((1, m, width), lambda b, i: (b, 0, 0)),
            pl.BlockSpec((1, m, width), lambda b, i: (b, 0, 0)),
        ],
        out_specs=pl.BlockSpec((1, tq, width), lambda b, i: (b, i, 0)),
        out_shape=jax.ShapeDtypeStruct((nb, s, width), BF16),
        compiler_params=_cparams(("parallel", "parallel")),
        name="mem_attn",
    )(u3, kmem, vmem)


def _merge_kernel(x_ref, g1_ref, da_ref, cv_ref, wa_ref, ma_ref, wg_ref, wb_ref, wo_ref, o_ref):
    x = x_ref[...]
    h = _rms(x, g1_ref[...]).astype(BF16)
    merged = None
    for n, br in enumerate((da_ref, cv_ref, wa_ref, ma_ref)):
        logits = jnp.dot(h, wg_ref[:, n * D_MODEL:(n + 1) * D_MODEL], preferred_element_type=F32)
        t = jax.nn.sigmoid(logits) * jnp.dot(br[...], wb_ref[n], preferred_element_type=F32)
        merged = t if merged is None else merged + t
    o_ref[...] = x + jnp.dot(merged.astype(BF16), wo_ref[...], preferred_element_type=F32)


def _merge(x2d, g1, br_da, br_conv, br_wa, br_ma, w_gate, w_branch, w_out, tm):
    t = x2d.shape[0]
    br = pl.BlockSpec((tm, BRANCH_WIDTH), lambda i: (i, 0))
    return pl.pallas_call(
        _merge_kernel,
        grid=(t // tm,),
        in_specs=[
            pl.BlockSpec((tm, D_MODEL), lambda i: (i, 0)),
            _resident((1, D_MODEL), lambda i: (0, 0)),
            br, br, br, br,
            _resident((D_MODEL, N_BRANCHES * D_MODEL), lambda i: (0, 0)),
            _resident((N_BRANCHES, BRANCH_WIDTH, D_MODEL), lambda i: (0, 0, 0)),
            _resident((D_MODEL, D_MODEL), lambda i: (0, 0)),
        ],
        out_specs=pl.BlockSpec((tm, D_MODEL), lambda i: (i, 0)),
        out_shape=jax.ShapeDtypeStruct((t, D_MODEL), F32),
        compiler_params=_cparams(("parallel",)),
        name="merge",
    )(x2d, g1, br_da, br_conv, br_wa, br_ma, w_gate, w_branch, w_out)


FF_CHUNK = 1024


def _ffn_kernel(x_ref, g2_ref, w1_ref, w2_ref, o_ref):
    x = x_ref[...]
    h = _rms(x, g2_ref[...]).astype(BF16)
    acc = x
    for c in range(D_FF // FF_CHUNK):
        cols = slice(c * FF_CHUNK, (c + 1) * FF_CHUNK)
        f = jnp.maximum(jnp.dot(h, w1_ref[:, cols], preferred_element_type=F32), 0.0)
        acc = acc + jnp.dot((f * f).astype(BF16), w2_ref[cols, :], preferred_element_type=F32)
    o_ref[...] = acc


def _ffn(x2d, g2, w1, w2, tm):
    t = x2d.shape[0]
    return pl.pallas_call(
        _ffn_kernel,
        grid=(t // tm,),
        in_specs=[
            pl.BlockSpec((tm, D_MODEL), lambda i: (i, 0)),
            _resident((1, D_MODEL), lambda i: (0, 0)),
            _resident((D_MODEL, D_FF), lambda i: (0, 0)),
            _resident((D_FF, D_MODEL), lambda i: (0, 0)),
        ],
        out_specs=pl.BlockSpec((tm, D_MODEL), lambda i: (i, 0)),
        out_shape=jax.ShapeDtypeStruct((t, D_MODEL), F32),
        compiler_params=_cparams(("parallel",)),
        name="ffn",
    )(x2d, g2, w1, w2)


def _rel_bucket(rel):
    nb = REL_BUCKETS // 2
    max_exact = nb // 2
    ret = jnp.where(rel > 0, nb, 0)
    n = jnp.abs(rel)
    nf = jnp.maximum(n, 1).astype(F32)
    large = max_exact + (jnp.log(nf / max_exact) / math.log(REL_MAX_DIST / max_exact)
                         * (nb - max_exact)).astype(jnp.int32)
    large = jnp.minimum(large, nb - 1)
    return ret + jnp.where(n < max_exact, n, large)


def _lookup(table, bucket):
    out = jnp.zeros(bucket.shape + (table.shape[1],), F32)
    for b in range(REL_BUCKETS):
        out = jnp.where((bucket == b)[..., None], table[b], out)
    return out


def _da_bias_tables(rel_bias, tq, tk):
    table = rel_bias[:, :2 * DA_HEADS].astype(F32) * LOG2E
    kk = jnp.arange(tk)[None, :, None]
    qq = jnp.arange(tq)[None, None, :]
    dd = (jnp.arange(3) - 1)[:, None, None]
    vals = _lookup(table, _rel_bucket(dd * tk + kk - qq))
    vals = vals.reshape(3, tk, tq, DA_HEADS, 2).transpose(3, 0, 1, 4, 2)
    bias_t = vals.reshape(DA_HEADS, 3, tk, 2 * tq)
    bias_t = jnp.concatenate([jnp.zeros_like(bias_t[:, :1]), bias_t], axis=1)
    nbk = REL_BUCKETS // 2
    far = jnp.stack([table[nbk - 1], jnp.zeros_like(table[0]), table[REL_BUCKETS - 1]])
    far = far.reshape(3, DA_HEADS, 2).transpose(1, 0, 2)
    cfar = jnp.repeat(far, tq, axis=-1).reshape(DA_HEADS, 3, 1, 2 * tq)
    return bias_t, cfar, table


def _split3(v):
    hi = v.astype(BF16).astype(F32)
    mid = (v - hi).astype(BF16).astype(F32)
    lo = (v - hi - mid).astype(BF16).astype(F32)
    return jnp.stack([hi, mid, lo], axis=-1)


def _da_shift_consts(table, gq, gk):
    bound = 1.02 * HEAD_DIM * jnp.max(jnp.abs(gq)) * jnp.max(jnp.abs(gk))
    cmax, cmin = jnp.max(table, axis=0), jnp.min(table, axis=0)
    shift = bound + cmax
    nbk = REL_BUCKETS // 2
    vals = jnp.stack([table[nbk - 1] - shift, -shift, table[REL_BUCKETS - 1] - shift])
    parts = _split3(vals).reshape(3, DA_HEADS, 2, N_AUG).transpose(1, 0, 2, 3)
    aug = jnp.zeros((DA_HEADS, 3, 2, DA_VDIM), F32)
    aug = aug.at[:, :, 0, HEAD_DIM:HEAD_DIM + N_AUG].set(parts[:, :, 0])
    aug = aug.at[:, :, 1, 0:N_AUG].set(parts[:, :, 1])
    safe = 2.0 * bound + jnp.max(cmax - cmin) <= MAX_SHIFT_GAP
    return aug.reshape(DA_HEADS, 6, 1, DA_VDIM), safe


def _wa_bias_table(rel_bias):
    table = rel_bias[:, 2 * DA_HEADS:].astype(F32) * LOG2E
    qoff = jnp.arange(BLOCK)
    koff = jnp.arange(3 * BLOCK) - BLOCK
    rel = koff[None, :] - qoff[:, None]
    bias = _lookup(table, _rel_bucket(rel)).transpose(2, 0, 1)
    return bias, jnp.abs(rel) <= WINDOW, table


def _wa_consts(bias, in_window, table, sink2, gq, gk):
    def tiles(b):
        b = jnp.where(in_window[None], b, NEG_INF)
        return b.reshape(WA_HEADS // 2, 2 * BLOCK, 3 * BLOCK)

    bound = 1.02 * HEAD_DIM * jnp.max(jnp.abs(gq)) * jnp.max(jnp.abs(gk))
    cmax, cmin = jnp.max(table, axis=0), jnp.min(table, axis=0)
    shift = jnp.maximum(bound + cmax, sink2)
    safe = jnp.max(shift - jnp.maximum(cmin - bound, sink2)) <= MAX_SHIFT_GAP
    return (tiles(bias), sink2), (tiles(bias - shift[:, None, None]), sink2 - shift), safe


def _dup_heads(w, heads, dim):
    w = w.reshape(w.shape[0], heads, 1, dim)
    return jnp.broadcast_to(w, (w.shape[0], heads, 2, dim)).reshape(w.shape[0], heads * 2 * dim)


def _layer_params(l, w_in, da_qk_g, wa_qk_g, ma_qk_g):
    w = w_in[l]
    wk = w[:, 3072:3200]
    wv = w[:, 3200:3328]

    def spread(a):
        a = a.reshape(a.shape[0], DA_HEADS, 2, HEAD_DIM)
        z = jnp.zeros_like(a[:, :, 0])
        return jnp.stack([a[:, :, 0], z, z, a[:, :, 1]], axis=2).reshape(a.shape[0], DA_HEADS * DA_KW)

    w_ext = jnp.concatenate([
        w[:, 0:512], spread(w[:, 512:1024]), w[:, 1024:2560], w[:, 2560:3072],
        _dup_heads(wk, WA_KV_HEADS, HEAD_DIM), _dup_heads(wv, WA_KV_HEADS, HEAD_DIM),
        w[:, 3328:3840]], axis=1).astype(BF16)
    ones = lambda n: jnp.ones((n,), F32)
    gq = da_qk_g[l, 0] * (HEAD_DIM ** -0.5 * LOG2E)
    gk = da_qk_g[l, 1]
    wq = wa_qk_g[l, 0] * (HEAD_DIM ** -0.5 * LOG2E)
    gain = jnp.concatenate([
        jnp.tile(gq, 2 * DA_HEADS),
        spread(jnp.tile(gk, 2 * DA_HEADS)[None])[0],
        ones(512 + 1024),
        jnp.tile(wq, WA_HEADS),
        jnp.tile(wa_qk_g[l, 1], 2 * WA_KV_HEADS),
        ones(256),
        jnp.tile(ma_qk_g[l, 0], MA_HEADS) * MA_HEAD_DIM ** -0.5,
    ]).reshape(1, U_WIDTH).astype(F32)
    w_gate = w[:, GATE_START:].astype(BF16)
    return w_ext, gain, w_gate, (gq, gk), (wq, wa_qk_g[l, 1])


def _tile(n, pref):
    return pref if n % pref == 0 else n


def _trunk(x, mem, rel_bias, norm1_g, w_in, da_qk_g, da_lambda, da_subln_g, conv_w, conv_b,
           conv_ln_g, conv_ln_b, wa_qk_g, wa_sink, mem_norm_g, w_mem_kv, ma_qk_g, w_branch, w_out,
           norm2_g, w_ff1, w_ff2, *, tq=512, tm=512):
    nb, s, _ = x.shape
    tq = _tile(s, tq)
    tk = tq
    tm = _tile(nb * s, tm)
    nk = s // tk
    bias_t, cfar, da_table = _da_bias_tables(rel_bias, tq, tk)
    aug_zero = jnp.zeros((DA_HEADS, 2, 1, DA_VDIM), F32)
    wa_tables = _wa_bias_table(rel_bias)
    kmem, vmem = _mem_kv(mem, mem_norm_g.reshape(DEPTH, 1, D_MODEL), w_mem_kv.astype(BF16),
                         ma_qk_g[:, 1].reshape(DEPTH, 1, MA_HEAD_DIM))
    x2d = x.reshape(nb * s, D_MODEL)
    for l in range(DEPTH):
        w_ext, gain, w_gate, da_g, wa_g = _layer_params(l, w_in, da_qk_g, wa_qk_g, ma_qk_g)
        g1 = norm1_g[l].reshape(1, D_MODEL)
        u, vt = _inproj(x2d, g1, w_ext, gain, tm, nb, tk)
        u3 = u.reshape(nb, s, U_WIDTH)
        lam_init = 0.8 - 0.6 * math.exp(-0.3 * l)
        aug, safe = _da_shift_consts(da_table, *da_g)
        da_args = (da_lambda[l], da_subln_g[l].reshape(DA_VDIM, 1), lam_init, tq, tk)
        br_da = lax.cond(
            safe,
            lambda u3, vt: _diff_attention(u3, vt, bias_t, cfar, aug, *da_args, True),
            lambda u3, vt: _diff_attention(u3, vt, bias_t, cfar, aug_zero, *da_args, False),
            u3, vt)
        br_conv = _conv_module(u3, conv_w[l], conv_b[l].reshape(1, CONV_CH),
                               conv_ln_g[l].reshape(1, CONV_CH), conv_ln_b[l].reshape(1, CONV_CH),
                               _tile(s, 512))
        wa_plain, wa_shifted, wa_safe = _wa_consts(*wa_tables, wa_sink[l].astype(F32) * LOG2E,
                                                   *wa_g)
        tw = _tile(s, 512)
        br_wa = lax.cond(
            wa_safe,
            lambda u3: _window_attention(u3, *wa_shifted, tw, True),
            lambda u3: _window_attention(u3, *wa_plain, tw, False),
            u3)
        br_ma = _memory_attention(u3, kmem[l], vmem[l], _tile(s, 1024))
        flat = lambda a: a.reshape(nb * s, BRANCH_WIDTH)
        x2d = _merge(x2d, g1, flat(br_da), flat(br_conv), flat(br_wa), flat(br_ma), w_gate,
                     w_branch[l].astype(BF16), w_out[l].astype(BF16), tm)
        x2d = _ffn(x2d, norm2_g[l].reshape(1, D_MODEL), w_ff1[l].astype(BF16),
                   w_ff2[l].astype(BF16), tm)
    return x2d.reshape(nb, s, D_MODEL)


def kernel(x_prompt, x_sample, mem_prompt, mem_sample, rel_bias, norm1_g, w_in, da_qk_g, da_lambda, da_subln_g, conv_w, conv_b, conv_ln_g, conv_ln_b, wa_qk_g, wa_sink, mem_norm_g, w_mem_kv, ma_qk_g, w_branch, w_out, norm2_g, w_ff1, w_ff2):
    params = (rel_bias, norm1_g, w_in, da_qk_g, da_lambda, da_subln_g, conv_w, conv_b, conv_ln_g,
              conv_ln_b, wa_qk_g, wa_sink, mem_norm_g, w_mem_kv, ma_qk_g, w_branch, w_out,
              norm2_g, w_ff1, w_ff2)
    if x_prompt.shape[1:] == x_sample.shape[1:]:
        nbp = x_prompt.shape[0]
        y = _trunk(jnp.concatenate([x_prompt, x_sample], axis=0),
                   jnp.concatenate([mem_prompt, mem_sample], axis=0), *params)
        return (y[:nbp], y[nbp:])
    return (_trunk(x_prompt, mem_prompt, *params), _trunk(x_sample, mem_sample, *params))
```

```python
import functools
import math

import jax
import jax.numpy as jnp
from jax import lax
from jax.experimental import pallas as pl
from jax.experimental.pallas import tpu as pltpu

F32 = jnp.float32
BF16 = jnp.bfloat16

D_MODEL = 1024
DEPTH = 4
HEAD_DIM = 64
DA_HEADS = 4
DA_VDIM = 2 * HEAD_DIM
CONV_CH = 512
CONV_WIDTH = 31
WA_HEADS = 8
WA_KV_HEADS = 2
WINDOW = 128
BLOCK = 128
MEM_TOKENS = 256
MA_HEADS = 4
MA_HEAD_DIM = 128
BRANCH_WIDTH = 512
N_BRANCHES = 4
D_FF = 4 * D_MODEL
REL_BUCKETS = 32
REL_MAX_DIST = 128
EPS = 1e-6
NEG_INF = -1e30

U_WIDTH = 4608
COL_DA_Q, COL_DA_K, COL_DA_V, COL_CONV, COL_WA_Q, COL_WA_K, COL_WA_V, COL_MA_Q = (
    0, 512, 1536, 2048, 3072, 3584, 3840, 4096)
DA_KW = 4 * HEAD_DIM
N_AUG = 3
SUB = 256
MXU_COUNT = 2
SUB_SEG = (64, 64, 64, 64, 64, 64, 0, 0, 0, 0, 0, 0, 64, 64, 64, 0, 128, 128)
SUB_DA_K = (2, 3, 4, 5)
GATE_START = 3840
SUBLANES = 8
DA_VROWS = DA_VDIM + SUBLANES
DA_GROUP = 8
LOG2E = 1.4426950408889634
MAX_SHIFT_GAP = 100.0

VMEM_LIMIT = 56 * 1024 * 1024


def _cparams(sem):
    return pltpu.CompilerParams(dimension_semantics=sem, vmem_limit_bytes=VMEM_LIMIT)


def _resident(shape, index_map):
    return pl.BlockSpec(shape, index_map, pipeline_mode=pl.Buffered(1))


def _rms(x, g):
    return x * lax.rsqrt(jnp.mean(x * x, axis=-1, keepdims=True) + EPS) * g


def _inproj_kernel(x_ref, g1_ref, w_ref, gain_ref, o_ref, vt_ref):
    tk = vt_ref.shape[-1]
    ones_rows = jnp.where(lax.broadcasted_iota(jnp.int32, (SUBLANES, tk), 0) == 0, 1.0, 0.0)
    h = _rms(x_ref[...], g1_ref[...]).astype(BF16)
    r = lax.broadcasted_iota(jnp.int32, (SUB, SUB), 0)
    c = lax.broadcasted_iota(jnp.int32, (SUB, SUB), 1)
    lane = lax.broadcasted_iota(jnp.int32, (1, SUB), 1)
    for s, seg in enumerate(SUB_SEG):
        cols = slice(s * SUB, (s + 1) * SUB)
        if s % MXU_COUNT == 0:
            wide = jnp.dot(h, w_ref[:, s * SUB:(s + MXU_COUNT) * SUB], preferred_element_type=F32)
        u = wide[:, (s % MXU_COUNT) * SUB:(s % MXU_COUNT + 1) * SUB]
        if seg:
            e = jnp.where(r // seg == c // seg, 1.0 / seg, 0.0).astype(BF16)
            ms = jnp.dot((u * u).astype(BF16), e, preferred_element_type=F32)
            u = u * lax.rsqrt(ms + EPS) * gain_ref[:, cols]
        if s in SUB_DA_K:
            ones = jnp.logical_or(
                jnp.logical_and(lane >= HEAD_DIM, lane < HEAD_DIM + N_AUG),
                jnp.logical_and(lane >= 2 * HEAD_DIM, lane < 2 * HEAD_DIM + N_AUG))
            u = jnp.where(ones, 1.0, u)
        o_ref[:, cols] = u.astype(BF16)
        if s * SUB in range(COL_DA_V, COL_DA_V + DA_HEADS * DA_VDIM, SUB):
            for hh in range(SUB // DA_VDIM):
                head = (s * SUB - COL_DA_V) // DA_VDIM + hh
                for ch in range(vt_ref.shape[2]):
                    blk = u[ch * tk:(ch + 1) * tk, hh * DA_VDIM:(hh + 1) * DA_VDIM]
                    vt_ref[0, head, ch, 0:DA_VDIM, :] = blk.T.astype(BF16)
                    vt_ref[0, head, ch, DA_VDIM:DA_VROWS, :] = ones_rows.astype(BF16)


def _inproj(x2d, g1, w_ext, gain, tm, nb, tk):
    t = x2d.shape[0]
    s = t // nb
    assert tm % tk == 0 and s % tm == 0
    tiles, cpt = s // tm, tm // tk
    return pl.pallas_call(
        _inproj_kernel,
        grid=(t // tm,),
        in_specs=[
            pl.BlockSpec((tm, D_MODEL), lambda i: (i, 0)),
            _resident((1, D_MODEL), lambda i: (0, 0)),
            _resident((D_MODEL, U_WIDTH), lambda i: (0, 0)),
            _resident((1, U_WIDTH), lambda i: (0, 0)),
        ],
        out_specs=[
            pl.BlockSpec((tm, U_WIDTH), lambda i: (i, 0)),
            pl.BlockSpec((1, DA_HEADS, cpt, DA_VROWS, tk),
                         lambda i: (i // tiles, 0, i % tiles, 0, 0)),
        ],
        out_shape=[jax.ShapeDtypeStruct((t, U_WIDTH), BF16),
                   jax.ShapeDtypeStruct((nb, DA_HEADS, s // tk, DA_VROWS, tk), BF16)],
        compiler_params=_cparams(("parallel",)),
        name="inproj",
    )(x2d, g1, w_ext, gain)


def _da_kernel(q_ref, k_ref, vt_ref, bias_ref, cfar_ref, aug_ref, lam_ref, g_ref, o_ref,
               q2_s, m_s, acc_s, *, tq, tk, nk, lam_init, fixed_shift):
    i = pl.program_id(2)
    q = q_ref[0]
    lane = lax.broadcasted_iota(jnp.int32, (tq, 2 * HEAD_DIM), 1)
    first = lane < HEAD_DIM
    zero = jnp.zeros_like(q)
    for var in range(q2_s.shape[0]):
        q2_s[var, 0:tq, 0:DA_VDIM] = jnp.where(first, q, aug_ref[0, 2 * var].astype(BF16))
        q2_s[var, 0:tq, DA_VDIM:DA_KW] = zero
        q2_s[var, tq:2 * tq, 0:DA_VDIM] = zero
        q2_s[var, tq:2 * tq, DA_VDIM:DA_KW] = jnp.where(first, aug_ref[0, 2 * var + 1].astype(BF16), q)
    m_s[...] = jnp.full(m_s.shape, NEG_INF, F32)
    acc_s[...] = jnp.zeros(acc_s.shape, F32)

    def scores(j):
        d = j - i
        kind = jnp.where(d < -1, 0, jnp.where(d > 1, 2, 1))
        tile = jnp.where(kind == 1, d + 2, 0)
        kc = k_ref[0, pl.ds(pl.multiple_of(j * tk, tk), tk), :]
        st = lax.dot_general(kc, q2_s[kind if fixed_shift else 0], (((1,), (1,)), ((), ())),
                             preferred_element_type=F32)
        return st + bias_ref[0, tile], cfar_ref[0, kind]

    def group(g, carry):
        pv = None
        for c in range(group_size):
            j = g * group_size + c
            st, cvec = scores(j)
            if fixed_shift:
                p = jnp.exp2(st).astype(BF16)
            else:
                m_old = m_s[...]
                m_new = jnp.maximum(m_old, jnp.max(st, axis=0, keepdims=True) + cvec)
                alpha = jnp.exp2(m_old - m_new)
                p = jnp.exp2(st - (m_new - cvec)).astype(BF16)
                m_s[...] = m_new
                acc_s[...] = alpha * acc_s[...]
            t = jnp.dot(vt_ref[0, 0, j], p, preferred_element_type=F32)
            if fixed_shift:
                pv = t if pv is None else pv + t
            else:
                acc_s[...] += t
        if fixed_shift:
            acc_s[...] += pv
        return carry

    group_size = max(c for c in (1, 2, DA_GROUP) if nk % c == 0) if fixed_shift else 1
    lax.fori_loop(0, nk // group_size, group, 0)

    lp = lam_ref[...]
    lam = (jnp.exp(jnp.sum(lp[0:1] * lp[1:2], keepdims=True))
           - jnp.exp(jnp.sum(lp[2:3] * lp[3:4], keepdims=True)) + lam_init)
    acc = acc_s[...]
    o = acc[0:DA_VDIM] * (1.0 / acc[DA_VDIM:DA_VDIM + 1])
    dd = o[:, 0:tq] - lam * o[:, tq:2 * tq]
    ms = jnp.mean(dd * dd, axis=0, keepdims=True)
    y = dd * lax.rsqrt(ms + EPS) * g_ref[...] * (1.0 - lam_init)
    o_ref[0] = y.T.astype(BF16)


def _diff_attention(u3, vt, bias_t, cfar, aug, lam_p, subln_g, lam_init, tq, tk, fixed_shift):
    nb, s, _ = u3.shape
    nq, nk = s // tq, s // tk
    nvar = aug.shape[1] // 2
    kern = functools.partial(_da_kernel, tq=tq, tk=tk, nk=nk, lam_init=lam_init,
                             fixed_shift=fixed_shift)
    return pl.pallas_call(
        kern,
        grid=(nb, DA_HEADS, nq),
        in_specs=[
            pl.BlockSpec((1, tq, DA_VDIM), lambda b, h, i: (b, i, COL_DA_Q // DA_VDIM + h)),
            pl.BlockSpec((1, s, DA_KW), lambda b, h, i: (b, 0, COL_DA_K // DA_KW + h)),
            pl.BlockSpec((1, 1, nk, DA_VROWS, tk), lambda b, h, i: (b, h, 0, 0, 0)),
            pl.BlockSpec((1, 4, tk, 2 * tq), lambda b, h, i: (h, 0, 0, 0)),
            pl.BlockSpec((1, 3, 1, 2 * tq), lambda b, h, i: (h, 0, 0, 0)),
            pl.BlockSpec((1, 2 * nvar, 1, DA_VDIM), lambda b, h, i: (h, 0, 0, 0)),
            pl.BlockSpec((4, HEAD_DIM), lambda b, h, i: (0, 0)),
            pl.BlockSpec((DA_VDIM, 1), lambda b, h, i: (0, 0)),
        ],
        out_specs=pl.BlockSpec((1, tq, DA_VDIM), lambda b, h, i: (b, i, h)),
        out_shape=jax.ShapeDtypeStruct((nb, s, DA_HEADS * DA_VDIM), BF16),
        scratch_shapes=[
            pltpu.VMEM((nvar, 2 * tq, DA_KW), BF16),
            pltpu.VMEM((1, 2 * tq), F32),
            pltpu.VMEM((DA_VROWS, 2 * tq), F32),
        ],
        compiler_params=_cparams(("parallel", "parallel", "parallel")),
        name="diff_attn_fixed" if fixed_shift else "diff_attn_runmax",
    )(u3, u3, vt, bias_t, cfar, aug, lam_p, subln_g)


CONV_HALO = 16
CONV_ROWS = 64


def _conv_kernel(ap_ref, ac_ref, an_ref, gp_ref, gc_ref, gn_ref, w_ref, b_ref, lg_ref, lb_ref,
                 o_ref, z_s, zp_s, *, tc, nt):
    i = pl.program_id(1)

    def glu(a_ref, g_ref):
        return a_ref[0].astype(F32) * jax.nn.sigmoid(g_ref[0].astype(F32))

    z_s[0:CONV_HALO, :] = jnp.where(i > 0, glu(ap_ref, gp_ref), 0.0)
    z_s[CONV_HALO:CONV_HALO + tc, :] = glu(ac_ref, gc_ref)
    z_s[CONV_HALO + tc:2 * CONV_HALO + tc, :] = jnp.where(i < nt - 1, glu(an_ref, gn_ref), 0.0)

    pad = CONV_WIDTH // 2
    span = tc + 3 * SUBLANES
    for p in range(SUBLANES):
        zp_s[p, 0:span, :] = z_s[p:p + span, :]

    def rows(r, carry):
        r0 = pl.multiple_of(r * CONV_ROWS, CONV_ROWS)
        acc = jnp.zeros((CONV_ROWS, CONV_CH), F32) + b_ref[...]
        for t in range(CONV_WIDTH):
            off = CONV_HALO - pad + t
            p = off % SUBLANES
            base = pl.multiple_of(r0 + (off - p), SUBLANES)
            acc = acc + w_ref[t:t + 1, :] * zp_s[p, pl.ds(base, CONV_ROWS), :]
        mu = jnp.mean(acc, axis=-1, keepdims=True)
        xc = acc - mu
        var = jnp.mean(xc * xc, axis=-1, keepdims=True)
        y = xc * lax.rsqrt(var + EPS) * lg_ref[...] + lb_ref[...]
        o_ref[0, pl.ds(r0, CONV_ROWS), :] = (y * jax.nn.sigmoid(y)).astype(BF16)
        return carry

    lax.fori_loop(0, tc // CONV_ROWS, rows, 0)


def _conv_module(u3, conv_w, conv_b, ln_g, ln_b, tc):
    nb, s, _ = u3.shape
    nt = s // tc
    hb = tc // CONV_HALO
    nhb = s // CONV_HALO
    ca = COL_CONV // CONV_CH
    cg = ca + 1

    def prev(col):
        return pl.BlockSpec((1, CONV_HALO, CONV_CH),
                            lambda b, i: (b, jnp.maximum(i * hb - 1, 0), col))

    def cur(col):
        return pl.BlockSpec((1, tc, CONV_CH), lambda b, i: (b, i, col))

    def nxt(col):
        return pl.BlockSpec((1, CONV_HALO, CONV_CH),
                            lambda b, i: (b, jnp.minimum((i + 1) * hb, nhb - 1), col))

    vec = pl.BlockSpec((1, CONV_CH), lambda b, i: (0, 0))
    kern = functools.partial(_conv_kernel, tc=tc, nt=nt)
    return pl.pallas_call(
        kern,
        grid=(nb, nt),
        in_specs=[prev(ca), cur(ca), nxt(ca), prev(cg), cur(cg), nxt(cg),
                  pl.BlockSpec((CONV_WIDTH, CONV_CH), lambda b, i: (0, 0)), vec, vec, vec],
        out_specs=pl.BlockSpec((1, tc, CONV_CH), lambda b, i: (b, i, 0)),
        out_shape=jax.ShapeDtypeStruct((nb, s, CONV_CH), BF16),
        scratch_shapes=[pltpu.VMEM((tc + 2 * CONV_HALO, CONV_CH), F32),
                        pltpu.VMEM((SUBLANES, tc + 3 * SUBLANES, CONV_CH), F32)],
        compiler_params=_cparams(("parallel", "parallel")),
        name="conv_module",
    )(u3, u3, u3, u3, u3, u3, conv_w, conv_b, ln_g, ln_b)


def _wa_kernel(sink_ref, q_ref, kp_ref, kc_ref, kn_ref, vp_ref, vc_ref, vn_ref, bias_ref,
               o_ref, k_s, v_s, *, tw, s_len, fixed_shift):
    i = pl.program_id(1)
    nt = pl.num_programs(1)
    k_s[0:BLOCK, :] = kp_ref[0]
    k_s[BLOCK:BLOCK + tw, :] = kc_ref[0]
    k_s[BLOCK + tw:2 * BLOCK + tw, :] = kn_ref[0]

    nkey = 3 * BLOCK
    pair = 2 * HEAD_DIM
    lane_q = lax.broadcasted_iota(jnp.int32, (BLOCK, pair), 1)
    row2 = lax.broadcasted_iota(jnp.int32, (2 * BLOCK, 1), 0)

    if fixed_shift:
        for kvh in range(WA_KV_HEADS):
            cols = slice(kvh * pair, (kvh + 1) * pair)
            parts = ((0, BLOCK, vp_ref, i > 0), (BLOCK, tw, vc_ref, None),
                     (BLOCK + tw, BLOCK, vn_ref, i < nt - 1))
            for r0, n, ref, valid in parts:
                vals = ref[0, :, cols]
                one = jnp.where(lax.broadcasted_iota(jnp.int32, (n, pair), 1) == 0, 1.0, 0.0)
                one = one.astype(BF16)
                if valid is not None:
                    vals = jnp.where(valid, vals, jnp.zeros_like(vals))
                    one = jnp.where(valid, one, jnp.zeros_like(one))
                v_s[kvh, r0:r0 + n, 0:pair] = vals
                v_s[kvh, r0:r0 + n, pair:2 * pair] = one
    else:
        v_s[0:BLOCK, :] = vp_ref[0]
        v_s[BLOCK:BLOCK + tw, :] = vc_ref[0]
        v_s[BLOCK + tw:2 * BLOCK + tw, :] = vn_ref[0]
        lane_v = lax.broadcasted_iota(jnp.int32, (nkey, pair), 1)
        col = lax.broadcasted_iota(jnp.int32, (1, nkey), 1)

    def blk(sb, carry):
        r0 = pl.multiple_of(sb * BLOCK, BLOCK)
        for jg in range(WA_HEADS // 2):
            kvh = (2 * jg) // (WA_HEADS // WA_KV_HEADS)
            qp = q_ref[0, pl.ds(r0, BLOCK), jg * pair:(jg + 1) * pair]
            zq = jnp.zeros_like(qp)
            q2 = jnp.concatenate([jnp.where(lane_q < HEAD_DIM, qp, zq),
                                  jnp.where(lane_q >= HEAD_DIM, qp, zq)], axis=0)
            kd = k_s[pl.ds(r0, nkey), kvh * pair:(kvh + 1) * pair]
            sc = lax.dot_general(q2, kd, (((1,), (1,)), ((), ())),
                                 preferred_element_type=F32)
            snk = jnp.where(row2 < BLOCK, sink_ref[2 * jg], sink_ref[2 * jg + 1])
            if fixed_shift:
                p = jnp.exp2(sc + bias_ref[jg]).astype(BF16)
                oa = jnp.dot(p, v_s[kvh, pl.ds(r0, nkey), :], preferred_element_type=F32)
                den = oa[:, pair:pair + 1] + jnp.exp2(snk)
                o = oa[:, 0:pair] * (1.0 / den)
                o = jnp.where(lane_q < HEAD_DIM, o[0:BLOCK], o[BLOCK:2 * BLOCK])
            else:
                kpos = i * tw + r0 - BLOCK + col
                inside = jnp.logical_and(kpos >= 0, kpos < s_len)
                sc = jnp.where(inside, sc + bias_ref[jg], NEG_INF)
                m = jnp.maximum(jnp.max(sc, axis=-1, keepdims=True), snk)
                e = jnp.exp2(sc - m)
                den = jnp.sum(e, axis=-1, keepdims=True) + jnp.exp2(snk - m)
                p = (e / den).astype(BF16)
                vd = v_s[pl.ds(r0, nkey), kvh * pair:(kvh + 1) * pair]
                zv = jnp.zeros_like(vd)
                o = (jnp.dot(p[0:BLOCK], jnp.where(lane_v < HEAD_DIM, vd, zv),
                             preferred_element_type=F32)
                     + jnp.dot(p[BLOCK:2 * BLOCK], jnp.where(lane_v >= HEAD_DIM, vd, zv),
                               preferred_element_type=F32))
            o_ref[0, pl.ds(r0, BLOCK), jg * pair:(jg + 1) * pair] = o.astype(BF16)
        return carry

    lax.fori_loop(0, tw // BLOCK, blk, 0)


def _window_attention(u3, wa_bias, sink, tw, fixed_shift):
    nb, s, _ = u3.shape
    nt = s // tw
    bpt = tw // BLOCK
    nblk = s // BLOCK
    width = WA_HEADS * HEAD_DIM
    kvw = 2 * WA_KV_HEADS * HEAD_DIM

    def prev(col):
        return pl.BlockSpec((1, BLOCK, kvw), lambda b, i: (b, jnp.maximum(i * bpt - 1, 0), col))

    def cur(col):
        return pl.BlockSpec((1, tw, kvw), lambda b, i: (b, i, col))

    def nxt(col):
        return pl.BlockSpec((1, BLOCK, kvw),
                            lambda b, i: (b, jnp.minimum((i + 1) * bpt, nblk - 1), col))

    ck, cv = COL_WA_K // kvw, COL_WA_V // kvw
    kern = functools.partial(_wa_kernel, tw=tw, s_len=s, fixed_shift=fixed_shift)
    v_scratch = ((WA_KV_HEADS, tw + 2 * BLOCK, 4 * HEAD_DIM) if fixed_shift
                 else (tw + 2 * BLOCK, kvw))
    return pl.pallas_call(
        kern,
        grid=(nb, nt),
        in_specs=[
            pl.BlockSpec(memory_space=pltpu.SMEM),
            pl.BlockSpec((1, tw, width), lambda b, i: (b, i, COL_WA_Q // width)),
            prev(ck), cur(ck), nxt(ck), prev(cv), cur(cv), nxt(cv),
            pl.BlockSpec((WA_HEADS // 2, 2 * BLOCK, 3 * BLOCK), lambda b, i: (0, 0, 0)),
        ],
        out_specs=pl.BlockSpec((1, tw, width), lambda b, i: (b, i, 0)),
        out_shape=jax.ShapeDtypeStruct((nb, s, width), BF16),
        scratch_shapes=[pltpu.VMEM((tw + 2 * BLOCK, kvw), BF16),
                        pltpu.VMEM(v_scratch, BF16)],
        compiler_params=_cparams(("parallel", "parallel")),
        name="window_attn_fixed" if fixed_shift else "window_attn_runmax",
    )(sink, u3, u3, u3, u3, u3, u3, u3, wa_bias)


def _memkv_kernel(mem_ref, g_ref, w_ref, gk_ref, k_ref, v_ref):
    hn = _rms(mem_ref[0], g_ref[0]).astype(BF16)
    kv = jnp.dot(hn, w_ref[0], preferred_element_type=F32)
    width = MA_HEADS * MA_HEAD_DIM
    for h in range(MA_HEADS):
        cols = slice(h * MA_HEAD_DIM, (h + 1) * MA_HEAD_DIM)
        k_ref[0, 0, :, cols] = _rms(kv[:, cols], gk_ref[0]).astype(BF16)
    v_ref[0, 0] = kv[:, width:2 * width].astype(BF16)


def _mem_kv(mem, mem_norm_g, w_mem_kv, gk):
    nb, m, _ = mem.shape
    width = MA_HEADS * MA_HEAD_DIM
    out = jax.ShapeDtypeStruct((DEPTH, nb, m, width), BF16)
    return pl.pallas_call(
        _memkv_kernel,
        grid=(DEPTH, nb),
        in_specs=[
            pl.BlockSpec((1, m, D_MODEL), lambda l, b: (b, 0, 0)),
            pl.BlockSpec((1, 1, D_MODEL), lambda l, b: (l, 0, 0)),
            pl.BlockSpec((1, D_MODEL, 2 * width), lambda l, b: (l, 0, 0)),
            pl.BlockSpec((1, 1, MA_HEAD_DIM), lambda l, b: (l, 0, 0)),
        ],
        out_specs=[pl.BlockSpec((1, 1, m, width), lambda l, b: (l, b, 0, 0)),
                   pl.BlockSpec((1, 1, m, width), lambda l, b: (l, b, 0, 0))],
        out_shape=[out, out],
        compiler_params=_cparams(("parallel", "parallel")),
        name="mem_kv",
    )(mem, mem_norm_g, w_mem_kv, gk)


def _ma_kernel(q_ref, k_ref, v_ref, o_ref):
    for h in range(MA_HEADS):
        cols = slice(h * MA_HEAD_DIM, (h + 1) * MA_HEAD_DIM)
        sc = lax.dot_general(q_ref[0, :, cols], k_ref[0, :, cols], (((1,), (1,)), ((), ())),
                             preferred_element_type=F32)
        e = jnp.exp(sc - jnp.max(sc, axis=-1, keepdims=True))
        p = (e / jnp.sum(e, axis=-1, keepdims=True)).astype(BF16)
        o_ref[0, :, cols] = jnp.dot(p, v_ref[0, :, cols],
                                    preferred_element_type=F32).astype(BF16)


def _memory_attention(u3, kmem, vmem, tq):
    nb, s, _ = u3.shape
    m = kmem.shape[1]
    width = MA_HEADS * MA_HEAD_DIM
    return pl.pallas_call(
        _ma_kernel,
        grid=(nb, s // tq),
        in_specs=[
            pl.BlockSpec((1, tq, width), lambda b, i: (b, i, COL_MA_Q // width)),
            pl.BlockSpec((1, m, width), lambda b, i: (b, 0, 0)),
            pl.BlockSpec((1, m, width), lambda b, i: (b, 0, 0)),
        ],
        out_specs=pl.BlockSpec((1, tq, width), lambda b, i: (b, i, 0)),
        out_shape=jax.ShapeDtypeStruct((nb, s, width), BF16),
        compiler_params=_cparams(("parallel", "parallel")),
        name="mem_attn",
    )(u3, kmem, vmem)


def _merge_kernel(x_ref, g1_ref, da_ref, cv_ref, wa_ref, ma_ref, wg_ref, wb_ref, wo_ref, o_ref):
    x = x_ref[...]
    h = _rms(x, g1_ref[...]).astype(BF16)
    merged = None
    for n, br in enumerate((da_ref, cv_ref, wa_ref, ma_ref)):
        logits = jnp.dot(h, wg_ref[:, n * D_MODEL:(n + 1) * D_MODEL], preferred_element_type=F32)
        t = jax.nn.sigmoid(logits) * jnp.dot(br[...], wb_ref[n], preferred_element_type=F32)
        merged = t if merged is None else merged + t
    o_ref[...] = x + jnp.dot(merged.astype(BF16), wo_ref[...], preferred_element_type=F32)


def _merge(x2d, g1, br_da, br_conv, br_wa, br_ma, w_gate, w_branch, w_out, tm):
    t = x2d.shape[0]
    br = pl.BlockSpec((tm, BRANCH_WIDTH), lambda i: (i, 0))
    return pl.pallas_call(
        _merge_kernel,
        grid=(t // tm,),
        in_specs=[
            pl.BlockSpec((tm, D_MODEL), lambda i: (i, 0)),
            _resident((1, D_MODEL), lambda i: (0, 0)),
            br, br, br, br,
            _resident((D_MODEL, N_BRANCHES * D_MODEL), lambda i: (0, 0)),
            _resident((N_BRANCHES, BRANCH_WIDTH, D_MODEL), lambda i: (0, 0, 0)),
            _resident((D_MODEL, D_MODEL), lambda i: (0, 0)),
        ],
        out_specs=pl.BlockSpec((tm, D_MODEL), lambda i: (i, 0)),
        out_shape=jax.ShapeDtypeStruct((t, D_MODEL), F32),
        compiler_params=_cparams(("parallel",)),
        name="merge",
    )(x2d, g1, br_da, br_conv, br_wa, br_ma, w_gate, w_branch, w_out)


FF_CHUNK = 1024


def _ffn_kernel(x_ref, g2_ref, w1_ref, w2_ref, o_ref):
    x = x_ref[...]
    h = _rms(x, g2_ref[...]).astype(BF16)
    acc = x
    for c in range(D_FF // FF_CHUNK):
        cols = slice(c * FF_CHUNK, (c + 1) * FF_CHUNK)
        f = jnp.maximum(jnp.dot(h, w1_ref[:, cols], preferred_element_type=F32), 0.0)
        acc = acc + jnp.dot((f * f).astype(BF16), w2_ref[cols, :], preferred_element_type=F32)
    o_ref[...] = acc


def _ffn(x2d, g2, w1, w2, tm):
    t = x2d.shape[0]
    return pl.pallas_call(
        _ffn_kernel,
        grid=(t // tm,),
        in_specs=[
            pl.BlockSpec((tm, D_MODEL), lambda i: (i, 0)),
            _resident((1, D_MODEL), lambda i: (0, 0)),
            _resident((D_MODEL, D_FF), lambda i: (0, 0)),
            _resident((D_FF, D_MODEL), lambda i: (0, 0)),
        ],
        out_specs=pl.BlockSpec((tm, D_MODEL), lambda i: (i, 0)),
        out_shape=jax.ShapeDtypeStruct((t, D_MODEL), F32),
        compiler_params=_cparams(("parallel",)),
        name="ffn",
    )(x2d, g2, w1, w2)


def _rel_bucket(rel):
    nb = REL_BUCKETS // 2
    max_exact = nb // 2
    ret = jnp.where(rel > 0, nb, 0)
    n = jnp.abs(rel)
    nf = jnp.maximum(n, 1).astype(F32)
    large = max_exact + (jnp.log(nf / max_exact) / math.log(REL_MAX_DIST / max_exact)
                         * (nb - max_exact)).astype(jnp.int32)
    large = jnp.minimum(large, nb - 1)
    return ret + jnp.where(n < max_exact, n, large)


def _lookup(table, bucket):
    out = jnp.zeros(bucket.shape + (table.shape[1],), F32)
    for b in range(REL_BUCKETS):
        out = jnp.where((bucket == b)[..., None], table[b], out)
    return out


def _da_bias_tables(rel_bias, tq, tk):
    table = rel_bias[:, :2 * DA_HEADS].astype(F32) * LOG2E
    kk = jnp.arange(tk)[None, :, None]
    qq = jnp.arange(tq)[None, None, :]
    dd = (jnp.arange(3) - 1)[:, None, None]
    vals = _lookup(table, _rel_bucket(dd * tk + kk - qq))
    vals = vals.reshape(3, tk, tq, DA_HEADS, 2).transpose(3, 0, 1, 4, 2)
    bias_t = vals.reshape(DA_HEADS, 3, tk, 2 * tq)
    bias_t = jnp.concatenate([jnp.zeros_like(bias_t[:, :1]), bias_t], axis=1)
    nbk = REL_BUCKETS // 2
    far = jnp.stack([table[nbk - 1], jnp.zeros_like(table[0]), table[REL_BUCKETS - 1]])
    far = far.reshape(3, DA_HEADS, 2).transpose(1, 0, 2)
    cfar = jnp.repeat(far, tq, axis=-1).reshape(DA_HEADS, 3, 1, 2 * tq)
    return bias_t, cfar, table


def _split3(v):
    hi = v.astype(BF16).astype(F32)
    mid = (v - hi).astype(BF16).astype(F32)
    lo = (v - hi - mid).astype(BF16).astype(F32)
    return jnp.stack([hi, mid, lo], axis=-1)


def _da_shift_consts(table, gq, gk):
    bound = 1.02 * HEAD_DIM * jnp.max(jnp.abs(gq)) * jnp.max(jnp.abs(gk))
    cmax, cmin = jnp.max(table, axis=0), jnp.min(table, axis=0)
    shift = bound + cmax
    nbk = REL_BUCKETS // 2
    vals = jnp.stack([table[nbk - 1] - shift, -shift, table[REL_BUCKETS - 1] - shift])
    parts = _split3(vals).reshape(3, DA_HEADS, 2, N_AUG).transpose(1, 0, 2, 3)
    aug = jnp.zeros((DA_HEADS, 3, 2, DA_VDIM), F32)
    aug = aug.at[:, :, 0, HEAD_DIM:HEAD_DIM + N_AUG].set(parts[:, :, 0])
    aug = aug.at[:, :, 1, 0:N_AUG].set(parts[:, :, 1])
    safe = 2.0 * bound + jnp.max(cmax - cmin) <= MAX_SHIFT_GAP
    return aug.reshape(DA_HEADS, 6, 1, DA_VDIM), safe


def _wa_bias_table(rel_bias):
    table = rel_bias[:, 2 * DA_HEADS:].astype(F32) * LOG2E
    qoff = jnp.arange(BLOCK)
    koff = jnp.arange(3 * BLOCK) - BLOCK
    rel = koff[None, :] - qoff[:, None]
    bias = _lookup(table, _rel_bucket(rel)).transpose(2, 0, 1)
    return bias, jnp.abs(rel) <= WINDOW, table


def _wa_consts(bias, in_window, table, sink2, gq, gk):
    def tiles(b):
        b = jnp.where(in_window[None], b, NEG_INF)
        return b.reshape(WA_HEADS // 2, 2 * BLOCK, 3 * BLOCK)

    bound = 1.02 * HEAD_DIM * jnp.max(jnp.abs(gq)) * jnp.max(jnp.abs(gk))
    cmax, cmin = jnp.max(table, axis=0), jnp.min(table, axis=0)
    shift = jnp.maximum(bound + cmax, sink2)
    safe = jnp.max(shift - jnp.maximum(cmin - bound, sink2)) <= MAX_SHIFT_GAP
    return (tiles(bias), sink2), (tiles(bias - shift[:, None, None]), sink2 - shift), safe


def _dup_heads(w, heads, dim):
    w = w.reshape(w.shape[0], heads, 1, dim)
    return jnp.broadcast_to(w, (w.shape[0], heads, 2, dim)).reshape(w.shape[0], heads * 2 * dim)


def _layer_params(l, w_in, da_qk_g, wa_qk_g, ma_qk_g):
    w = w_in[l]
    wk = w[:, 3072:3200]
    wv = w[:, 3200:3328]

    def spread(a):
        a = a.reshape(a.shape[0], DA_HEADS, 2, HEAD_DIM)
        z = jnp.zeros_like(a[:, :, 0])
        return jnp.stack([a[:, :, 0], z, z, a[:, :, 1]], axis=2).reshape(a.shape[0], DA_HEADS * DA_KW)

    w_ext = jnp.concatenate([
        w[:, 0:512], spread(w[:, 512:1024]), w[:, 1024:2560], w[:, 2560:3072],
        _dup_heads(wk, WA_KV_HEADS, HEAD_DIM), _dup_heads(wv, WA_KV_HEADS, HEAD_DIM),
        w[:, 3328:3840]], axis=1).astype(BF16)
    ones = lambda n: jnp.ones((n,), F32)
    gq = da_qk_g[l, 0] * (HEAD_DIM ** -0.5 * LOG2E)
    gk = da_qk_g[l, 1]
    wq = wa_qk_g[l, 0] * (HEAD_DIM ** -0.5 * LOG2E)
    gain = jnp.concatenate([
        jnp.tile(gq, 2 * DA_HEADS),
        spread(jnp.tile(gk, 2 * DA_HEADS)[None])[0],
        ones(512 + 1024),
        jnp.tile(wq, WA_HEADS),
        jnp.tile(wa_qk_g[l, 1], 2 * WA_KV_HEADS),
        ones(256),
        jnp.tile(ma_qk_g[l, 0], MA_HEADS) * MA_HEAD_DIM ** -0.5,
    ]).reshape(1, U_WIDTH).astype(F32)
    w_gate = w[:, GATE_START:].astype(BF16)
    return w_ext, gain, w_gate, (gq, gk), (wq, wa_qk_g[l, 1])


def _tile(n, pref):
    return pref if n % pref == 0 else n


def _trunk(x, mem, rel_bias, norm1_g, w_in, da_qk_g, da_lambda, da_subln_g, conv_w, conv_b,
           conv_ln_g, conv_ln_b, wa_qk_g, wa_sink, mem_norm_g, w_mem_kv, ma_qk_g, w_branch, w_out,
           norm2_g, w_ff1, w_ff2, *, tq=512, tm=512):
    nb, s, _ = x.shape
    tq = _tile(s, tq)
    tk = tq
    tm = _tile(nb * s, tm)
    nk = s // tk
    bias_t, cfar, da_table = _da_bias_tables(rel_bias, tq, tk)
    aug_zero = jnp.zeros((DA_HEADS, 2, 1, DA_VDIM), F32)
    wa_tables = _wa_bias_table(rel_bias)
    kmem, vmem = _mem_kv(mem, mem_norm_g.reshape(DEPTH, 1, D_MODEL), w_mem_kv.astype(BF16),
                         ma_qk_g[:, 1].reshape(DEPTH, 1, MA_HEAD_DIM))
    x2d = x.reshape(nb * s, D_MODEL)
    for l in range(DEPTH):
        w_ext, gain, w_gate, da_g, wa_g = _layer_params(l, w_in, da_qk_g, wa_qk_g, ma_qk_g)
        g1 = norm1_g[l].reshape(1, D_MODEL)
        u, vt = _inproj(x2d, g1, w_ext, gain, tm, nb, tk)
        u3 = u.reshape(nb, s, U_WIDTH)
        lam_init = 0.8 - 0.6 * math.exp(-0.3 * l)
        aug, safe = _da_shift_consts(da_table, *da_g)
        da_args = (da_lambda[l], da_subln_g[l].reshape(DA_VDIM, 1), lam_init, tq, tk)
        br_da = lax.cond(
            safe,
            lambda u3, vt: _diff_attention(u3, vt, bias_t, cfar, aug, *da_args, True),
            lambda u3, vt: _diff_attention(u3, vt, bias_t, cfar, aug_zero, *da_args, False),
            u3, vt)
        br_conv = _conv_module(u3, conv_w[l], conv_b[l].reshape(1, CONV_CH),
                               conv_ln_g[l].reshape(1, CONV_CH), conv_ln_b[l].reshape(1, CONV_CH),
                               _tile(s, 512))
        wa_plain, wa_shifted, wa_safe = _wa_consts(*wa_tables, wa_sink[l].astype(F32) * LOG2E,
                                                   *wa_g)
        tw = _tile(s, 512)
        br_wa = lax.cond(
            wa_safe,
            lambda u3: _window_attention(u3, *wa_shifted, tw, True),
            lambda u3: _window_attention(u3, *wa_plain, tw, False),
            u3)
        br_ma = _memory_attention(u3, kmem[l], vmem[l], _tile(s, 1024))
        flat = lambda a: a.reshape(nb * s, BRANCH_WIDTH)
        x2d = _merge(x2d, g1, flat(br_da), flat(br_conv), flat(br_wa), flat(br_ma), w_gate,
                     w_branch[l].astype(BF16), w_out[l].astype(BF16), tm)
        x2d = _ffn(x2d, norm2_g[l].reshape(1, D_MODEL), w_ff1[l].astype(BF16),
                   w_ff2[l].astype(BF16), tm)
    return x2d.reshape(nb, s, D_MODEL)


def kernel(x_prompt, x_sample, mem_prompt, mem_sample, rel_bias, norm1_g, w_in, da_qk_g, da_lambda, da_subln_g, conv_w, conv_b, conv_ln_g, conv_ln_b, wa_qk_g, wa_sink, mem_norm_g, w_mem_kv, ma_qk_g, w_branch, w_out, norm2_g, w_ff1, w_ff2):
    params = (rel_bias, norm1_g, w_in, da_qk_g, da_lambda, da_subln_g, conv_w, conv_b, conv_ln_g,
              conv_ln_b, wa_qk_g, wa_sink, mem_norm_g, w_mem_kv, ma_qk_g, w_branch, w_out,
              norm2_g, w_ff1, w_ff2)
    return (_trunk(x_prompt, mem_prompt, *params), _trunk(x_sample, mem_sample, *params))
```

```python
import functools
import math

import jax
import jax.numpy as jnp
from jax import lax
from jax.experimental import pallas as pl
from jax.experimental.pallas import tpu as pltpu

F32 = jnp.float32
BF16 = jnp.bfloat16

D_MODEL = 1024
DEPTH = 4
HEAD_DIM = 64
DA_HEADS = 4
DA_VDIM = 2 * HEAD_DIM
CONV_CH = 512
CONV_WIDTH = 31
WA_HEADS = 8
WA_KV_HEADS = 2
WINDOW = 128
BLOCK = 128
MEM_TOKENS = 256
MA_HEADS = 4
MA_HEAD_DIM = 128
BRANCH_WIDTH = 512
N_BRANCHES = 4
D_FF = 4 * D_MODEL
REL_BUCKETS = 32
REL_MAX_DIST = 128
EPS = 1e-6
NEG_INF = -1e30

U_WIDTH = 4096
COL_DA_Q, COL_DA_K, COL_DA_V, COL_CONV, COL_WA_Q, COL_WA_K, COL_WA_V, COL_MA_Q = (
    0, 512, 1024, 1536, 2560, 3072, 3328, 3584)
SUB = 256
MXU_COUNT = 2
SUB_SEG = (64, 64, 64, 64, 0, 0, 0, 0, 0, 0, 64, 64, 64, 0, 128, 128)
GATE_START = 3840
SUBLANES = 8
DA_VROWS = DA_VDIM + SUBLANES
DA_GROUP = 16
LOG2E = 1.4426950408889634
MAX_SHIFT_GAP = 100.0
MAX_LOGIT = 60.0

VMEM_LIMIT = 56 * 1024 * 1024


def _cparams(sem):
    return pltpu.CompilerParams(dimension_semantics=sem, vmem_limit_bytes=VMEM_LIMIT)


def _resident(shape, index_map):
    return pl.BlockSpec(shape, index_map, pipeline_mode=pl.Buffered(1))


def _rms(x, g):
    return x * lax.rsqrt(jnp.mean(x * x, axis=-1, keepdims=True) + EPS) * g


def _inproj_kernel(x_ref, g1_ref, w_ref, gain_ref, o_ref, vt_ref):
    tk = vt_ref.shape[-1]
    ones_rows = jnp.where(lax.broadcasted_iota(jnp.int32, (SUBLANES, tk), 0) == 0, 1.0, 0.0)
    h = _rms(x_ref[...], g1_ref[...]).astype(BF16)
    r = lax.broadcasted_iota(jnp.int32, (SUB, SUB), 0)
    c = lax.broadcasted_iota(jnp.int32, (SUB, SUB), 1)
    for s, seg in enumerate(SUB_SEG):
        cols = slice(s * SUB, (s + 1) * SUB)
        if s % MXU_COUNT == 0:
            wide = jnp.dot(h, w_ref[:, s * SUB:(s + MXU_COUNT) * SUB], preferred_element_type=F32)
        u = wide[:, (s % MXU_COUNT) * SUB:(s % MXU_COUNT + 1) * SUB]
        if seg:
            e = jnp.where(r // seg == c // seg, 1.0 / seg, 0.0).astype(BF16)
            ms = jnp.dot((u * u).astype(BF16), e, preferred_element_type=F32)
            u = u * lax.rsqrt(ms + EPS) * gain_ref[:, cols]
        o_ref[:, cols] = u.astype(BF16)
        if s * SUB in range(COL_DA_V, COL_DA_V + DA_HEADS * DA_VDIM, SUB):
            for hh in range(SUB // DA_VDIM):
                head = (s * SUB - COL_DA_V) // DA_VDIM + hh
                for ch in range(vt_ref.shape[2]):
                    blk = u[ch * tk:(ch + 1) * tk, hh * DA_VDIM:(hh + 1) * DA_VDIM]
                    vt_ref[0, head, ch, 0:DA_VDIM, :] = blk.T.astype(BF16)
                    vt_ref[0, head, ch, DA_VDIM:DA_VROWS, :] = ones_rows.astype(BF16)


def _inproj(x2d, g1, w_ext, gain, tm, nb, tk):
    t = x2d.shape[0]
    s = t // nb
    assert tm % tk == 0 and s % tm == 0
    tiles, cpt = s // tm, tm // tk
    return pl.pallas_call(
        _inproj_kernel,
        grid=(t // tm,),
        in_specs=[
            pl.BlockSpec((tm, D_MODEL), lambda i: (i, 0)),
            _resident((1, D_MODEL), lambda i: (0, 0)),
            _resident((D_MODEL, U_WIDTH), lambda i: (0, 0)),
            _resident((1, U_WIDTH), lambda i: (0, 0)),
        ],
        out_specs=[
            pl.BlockSpec((tm, U_WIDTH), lambda i: (i, 0)),
            pl.BlockSpec((1, DA_HEADS, cpt, DA_VROWS, tk),
                         lambda i: (i // tiles, 0, i % tiles, 0, 0)),
        ],
        out_shape=[jax.ShapeDtypeStruct((t, U_WIDTH), BF16),
                   jax.ShapeDtypeStruct((nb, DA_HEADS, s // tk, DA_VROWS, tk), BF16)],
        compiler_params=_cparams(("parallel",)),
        name="inproj",
    )(x2d, g1, w_ext, gain)


def _da_kernel(q_ref, k_ref, vt_ref, bias_ref, cfar_ref, lam_ref, g_ref, o_ref,
               q2_s, m_s, acc_s, *, tq, tk, nk, lam_init, bounded):
    i = pl.program_id(2)
    q = q_ref[0]
    first = lax.broadcasted_iota(jnp.int32, (tq, 2 * HEAD_DIM), 1) < HEAD_DIM
    zero = jnp.zeros_like(q)
    q2_s[0:tq, :] = jnp.where(first, q, zero)
    q2_s[tq:2 * tq, :] = jnp.where(first, zero, q)
    m_s[...] = jnp.full(m_s.shape, NEG_INF, F32)
    acc_s[...] = jnp.zeros(acc_s.shape, F32)

    def group(g, carry):
        pv = None
        for c in range(group_size):
            j = g * group_size + c
            d = j - i
            kind = jnp.where(d < -1, 0, jnp.where(d > 1, 2, 1))
            tile = jnp.where(kind == 1, d + 2, 0)
            kc = k_ref[0, pl.ds(pl.multiple_of(j * tk, tk), tk), :]
            st = lax.dot_general(kc, q2_s[...], (((1,), (1,)), ((), ())),
                                 preferred_element_type=F32)
            st = st + bias_ref[0, tile]
            cvec = cfar_ref[0, kind]
            if bounded:
                p = jnp.exp2(st).astype(BF16)
                t = jnp.dot(vt_ref[0, 0, j], p, preferred_element_type=F32) * cvec
                pv = t if pv is None else pv + t
            else:
                m_old = m_s[...]
                m_new = jnp.maximum(m_old, jnp.max(st, axis=0, keepdims=True) + cvec)
                alpha = jnp.exp2(m_old - m_new)
                p = jnp.exp2(st - (m_new - cvec)).astype(BF16)
                m_s[...] = m_new
                acc_s[...] = (alpha * acc_s[...]
                              + jnp.dot(vt_ref[0, 0, j], p, preferred_element_type=F32))
        if bounded:
            acc_s[...] += pv
        return carry

    group_size = max(c for c in (1, 2, DA_GROUP) if nk % c == 0) if bounded else 1
    lax.fori_loop(0, nk // group_size, group, 0)

    lp = lam_ref[...]
    lam = (jnp.exp(jnp.sum(lp[0:1] * lp[1:2], keepdims=True))
           - jnp.exp(jnp.sum(lp[2:3] * lp[3:4], keepdims=True)) + lam_init)
    acc = acc_s[...]
    o = acc[0:DA_VDIM] * (1.0 / acc[DA_VDIM:DA_VDIM + 1])
    dd = o[:, 0:tq] - lam * o[:, tq:2 * tq]
    ms = jnp.mean(dd * dd, axis=0, keepdims=True)
    y = dd * lax.rsqrt(ms + EPS) * g_ref[...] * (1.0 - lam_init)
    o_ref[0] = y.T.astype(BF16)


def _diff_attention(u3, vt, bias_t, cfar, lam_p, subln_g, lam_init, tq, tk, bounded):
    nb, s, _ = u3.shape
    nq, nk = s // tq, s // tk
    kern = functools.partial(_da_kernel, tq=tq, tk=tk, nk=nk, lam_init=lam_init, bounded=bounded)
    return pl.pallas_call(
        kern,
        grid=(nb, DA_HEADS, nq),
        in_specs=[
            pl.BlockSpec((1, tq, DA_VDIM), lambda b, h, i: (b, i, COL_DA_Q // DA_VDIM + h)),
            pl.BlockSpec((1, s, DA_VDIM), lambda b, h, i: (b, 0, COL_DA_K // DA_VDIM + h)),
            pl.BlockSpec((1, 1, nk, DA_VROWS, tk), lambda b, h, i: (b, h, 0, 0, 0)),
            pl.BlockSpec((1, 4, tk, 2 * tq), lambda b, h, i: (h, 0, 0, 0)),
            pl.BlockSpec((1, 3, 1, 2 * tq), lambda b, h, i: (h, 0, 0, 0)),
            pl.BlockSpec((4, HEAD_DIM), lambda b, h, i: (0, 0)),
            pl.BlockSpec((DA_VDIM, 1), lambda b, h, i: (0, 0)),
        ],
        out_specs=pl.BlockSpec((1, tq, DA_VDIM), lambda b, h, i: (b, i, h)),
        out_shape=jax.ShapeDtypeStruct((nb, s, DA_HEADS * DA_VDIM), BF16),
        scratch_shapes=[
            pltpu.VMEM((2 * tq, DA_VDIM), BF16),
            pltpu.VMEM((1, 2 * tq), F32),
            pltpu.VMEM((DA_VROWS, 2 * tq), F32),
        ],
        compiler_params=_cparams(("parallel", "parallel", "parallel")),
        name="diff_attn_bounded" if bounded else "diff_attn_runmax",
    )(u3, u3, vt, bias_t, cfar, lam_p, subln_g)


CONV_HALO = 16
CONV_ROWS = 64


def _conv_kernel(ap_ref, ac_ref, an_ref, gp_ref, gc_ref, gn_ref, w_ref, b_ref, lg_ref, lb_ref,
                 o_ref, z_s, zp_s, *, tc, nt):
    i = pl.program_id(1)

    def glu(a_ref, g_ref):
        return a_ref[0].astype(F32) * jax.nn.sigmoid(g_ref[0].astype(F32))

    z_s[0:CONV_HALO, :] = jnp.where(i > 0, glu(ap_ref, gp_ref), 0.0)
    z_s[CONV_HALO:CONV_HALO + tc, :] = glu(ac_ref, gc_ref)
    z_s[CONV_HALO + tc:2 * CONV_HALO + tc, :] = jnp.where(i < nt - 1, glu(an_ref, gn_ref), 0.0)

    pad = CONV_WIDTH // 2
    span = tc + 3 * SUBLANES
    for p in range(SUBLANES):
        zp_s[p, 0:span, :] = z_s[p:p + span, :]

    def rows(r, carry):
        r0 = pl.multiple_of(r * CONV_ROWS, CONV_ROWS)
        acc = jnp.zeros((CONV_ROWS, CONV_CH), F32) + b_ref[...]
        for t in range(CONV_WIDTH):
            off = CONV_HALO - pad + t
            p = off % SUBLANES
            base = pl.multiple_of(r0 + (off - p), SUBLANES)
            acc = acc + w_ref[t:t + 1, :] * zp_s[p, pl.ds(base, CONV_ROWS), :]
        mu = jnp.mean(acc, axis=-1, keepdims=True)
        xc = acc - mu
        var = jnp.mean(xc * xc, axis=-1, keepdims=True)
        y = xc * lax.rsqrt(var + EPS) * lg_ref[...] + lb_ref[...]
        o_ref[0, pl.ds(r0, CONV_ROWS), :] = (y * jax.nn.sigmoid(y)).astype(BF16)
        return carry

    lax.fori_loop(0, tc // CONV_ROWS, rows, 0)


def _conv_module(u3, conv_w, conv_b, ln_g, ln_b, tc):
    nb, s, _ = u3.shape
    nt = s // tc
    hb = tc // CONV_HALO
    nhb = s // CONV_HALO
    ca = COL_CONV // CONV_CH
    cg = ca + 1

    def prev(col):
        return pl.BlockSpec((1, CONV_HALO, CONV_CH),
                            lambda b, i: (b, jnp.maximum(i * hb - 1, 0), col))

    def cur(col):
        return pl.BlockSpec((1, tc, CONV_CH), lambda b, i: (b, i, col))

    def nxt(col):
        return pl.BlockSpec((1, CONV_HALO, CONV_CH),
                            lambda b, i: (b, jnp.minimum((i + 1) * hb, nhb - 1), col))

    vec = pl.BlockSpec((1, CONV_CH), lambda b, i: (0, 0))
    kern = functools.partial(_conv_kernel, tc=tc, nt=nt)
    return pl.pallas_call(
        kern,
        grid=(nb, nt),
        in_specs=[prev(ca), cur(ca), nxt(ca), prev(cg), cur(cg), nxt(cg),
                  pl.BlockSpec((CONV_WIDTH, CONV_CH), lambda b, i: (0, 0)), vec, vec, vec],
        out_specs=pl.BlockSpec((1, tc, CONV_CH), lambda b, i: (b, i, 0)),
        out_shape=jax.ShapeDtypeStruct((nb, s, CONV_CH), BF16),
        scratch_shapes=[pltpu.VMEM((tc + 2 * CONV_HALO, CONV_CH), F32),
                        pltpu.VMEM((SUBLANES, tc + 3 * SUBLANES, CONV_CH), F32)],
        compiler_params=_cparams(("parallel", "parallel")),
        name="conv_module",
    )(u3, u3, u3, u3, u3, u3, conv_w, conv_b, ln_g, ln_b)


def _wa_kernel(sink_ref, q_ref, kp_ref, kc_ref, kn_ref, vp_ref, vc_ref, vn_ref, bias_ref,
               o_ref, k_s, v_s, *, tw, s_len, fixed_shift):
    i = pl.program_id(1)
    nt = pl.num_programs(1)
    k_s[0:BLOCK, :] = kp_ref[0]
    k_s[BLOCK:BLOCK + tw, :] = kc_ref[0]
    k_s[BLOCK + tw:2 * BLOCK + tw, :] = kn_ref[0]

    nkey = 3 * BLOCK
    pair = 2 * HEAD_DIM
    lane_q = lax.broadcasted_iota(jnp.int32, (BLOCK, pair), 1)
    row2 = lax.broadcasted_iota(jnp.int32, (2 * BLOCK, 1), 0)

    if fixed_shift:
        for kvh in range(WA_KV_HEADS):
            cols = slice(kvh * pair, (kvh + 1) * pair)
            parts = ((0, BLOCK, vp_ref, i > 0), (BLOCK, tw, vc_ref, None),
                     (BLOCK + tw, BLOCK, vn_ref, i < nt - 1))
            for r0, n, ref, valid in parts:
                vals = ref[0, :, cols]
                one = jnp.where(lax.broadcasted_iota(jnp.int32, (n, pair), 1) == 0, 1.0, 0.0)
                one = one.astype(BF16)
                if valid is not None:
                    vals = jnp.where(valid, vals, jnp.zeros_like(vals))
                    one = jnp.where(valid, one, jnp.zeros_like(one))
                v_s[kvh, r0:r0 + n, 0:pair] = vals
                v_s[kvh, r0:r0 + n, pair:2 * pair] = one
    else:
        v_s[0:BLOCK, :] = vp_ref[0]
        v_s[BLOCK:BLOCK + tw, :] = vc_ref[0]
        v_s[BLOCK + tw:2 * BLOCK + tw, :] = vn_ref[0]
        lane_v = lax.broadcasted_iota(jnp.int32, (nkey, pair), 1)
        col = lax.broadcasted_iota(jnp.int32, (1, nkey), 1)

    def blk(sb, carry):
        r0 = pl.multiple_of(sb * BLOCK, BLOCK)
        for jg in range(WA_HEADS // 2):
            kvh = (2 * jg) // (WA_HEADS // WA_KV_HEADS)
            qp = q_ref[0, pl.ds(r0, BLOCK), jg * pair:(jg + 1) * pair]
            zq = jnp.zeros_like(qp)
            q2 = jnp.concatenate([jnp.where(lane_q < HEAD_DIM, qp, zq),
                                  jnp.where(lane_q >= HEAD_DIM, qp, zq)], axis=0)
            kd = k_s[pl.ds(r0, nkey), kvh * pair:(kvh + 1) * pair]
            sc = lax.dot_general(q2, kd, (((1,), (1,)), ((), ())),
                                 preferred_element_type=F32)
            snk = jnp.where(row2 < BLOCK, sink_ref[2 * jg], sink_ref[2 * jg + 1])
            if fixed_shift:
                p = jnp.exp2(sc + bias_ref[jg]).astype(BF16)
                oa = jnp.dot(p, v_s[kvh, pl.ds(r0, nkey), :], preferred_element_type=F32)
                den = oa[:, pair:pair + 1] + jnp.exp2(snk)
                o = oa[:, 0:pair] * (1.0 / den)
                o = jnp.where(lane_q < HEAD_DIM, o[0:BLOCK], o[BLOCK:2 * BLOCK])
            else:
                kpos = i * tw + r0 - BLOCK + col
                inside = jnp.logical_and(kpos >= 0, kpos < s_len)
                sc = jnp.where(inside, sc + bias_ref[jg], NEG_INF)
                m = jnp.maximum(jnp.max(sc, axis=-1, keepdims=True), snk)
                e = jnp.exp2(sc - m)
                den = jnp.sum(e, axis=-1, keepdims=True) + jnp.exp2(snk - m)
                p = (e / den).astype(BF16)
                vd = v_s[pl.ds(r0, nkey), kvh * pair:(kvh + 1) * pair]
                zv = jnp.zeros_like(vd)
                o = (jnp.dot(p[0:BLOCK], jnp.where(lane_v < HEAD_DIM, vd, zv),
                             preferred_element_type=F32)
                     + jnp.dot(p[BLOCK:2 * BLOCK], jnp.where(lane_v >= HEAD_DIM, vd, zv),
                               preferred_element_type=F32))
            o_ref[0, pl.ds(r0, BLOCK), jg * pair:(jg + 1) * pair] = o.astype(BF16)
        return carry

    lax.fori_loop(0, tw // BLOCK, blk, 0)


def _window_attention(u3, wa_bias, sink, tw, fixed_shift):
    nb, s, _ = u3.shape
    nt = s // tw
    bpt = tw // BLOCK
    nblk = s // BLOCK
    width = WA_HEADS * HEAD_DIM
    kvw = 2 * WA_KV_HEADS * HEAD_DIM

    def prev(col):
        return pl.BlockSpec((1, BLOCK, kvw), lambda b, i: (b, jnp.maximum(i * bpt - 1, 0), col))

    def cur(col):
        return pl.BlockSpec((1, tw, kvw), lambda b, i: (b, i, col))

    def nxt(col):
        return pl.BlockSpec((1, BLOCK, kvw),
                            lambda b, i: (b, jnp.minimum((i + 1) * bpt, nblk - 1), col))

    ck, cv = COL_WA_K // kvw, COL_WA_V // kvw
    kern = functools.partial(_wa_kernel, tw=tw, s_len=s, fixed_shift=fixed_shift)
    v_scratch = ((WA_KV_HEADS, tw + 2 * BLOCK, 4 * HEAD_DIM) if fixed_shift
                 else (tw + 2 * BLOCK, kvw))
    return pl.pallas_call(
        kern,
        grid=(nb, nt),
        in_specs=[
            pl.BlockSpec(memory_space=pltpu.SMEM),
            pl.BlockSpec((1, tw, width), lambda b, i: (b, i, COL_WA_Q // width)),
            prev(ck), cur(ck), nxt(ck), prev(cv), cur(cv), nxt(cv),
            pl.BlockSpec((WA_HEADS // 2, 2 * BLOCK, 3 * BLOCK), lambda b, i: (0, 0, 0)),
        ],
        out_specs=pl.BlockSpec((1, tw, width), lambda b, i: (b, i, 0)),
        out_shape=jax.ShapeDtypeStruct((nb, s, width), BF16),
        scratch_shapes=[pltpu.VMEM((tw + 2 * BLOCK, kvw), BF16),
                        pltpu.VMEM(v_scratch, BF16)],
        compiler_params=_cparams(("parallel", "parallel")),
        name="window_attn_fixed" if fixed_shift else "window_attn_runmax",
    )(sink, u3, u3, u3, u3, u3, u3, u3, wa_bias)


def _memkv_kernel(mem_ref, g_ref, w_ref, gk_ref, k_ref, v_ref):
    hn = _rms(mem_ref[0], g_ref[0]).astype(BF16)
    kv = jnp.dot(hn, w_ref[0], preferred_element_type=F32)
    width = MA_HEADS * MA_HEAD_DIM
    for h in range(MA_HEADS):
        cols = slice(h * MA_HEAD_DIM, (h + 1) * MA_HEAD_DIM)
        k_ref[0, 0, :, cols] = _rms(kv[:, cols], gk_ref[0]).astype(BF16)
    v_ref[0, 0] = kv[:, width:2 * width].astype(BF16)


def _mem_kv(mem, mem_norm_g, w_mem_kv, gk):
    nb, m, _ = mem.shape
    width = MA_HEADS * MA_HEAD_DIM
    out = jax.ShapeDtypeStruct((DEPTH, nb, m, width), BF16)
    return pl.pallas_call(
        _memkv_kernel,
        grid=(DEPTH, nb),
        in_specs=[
            pl.BlockSpec((1, m, D_MODEL), lambda l, b: (b, 0, 0)),
            pl.BlockSpec((1, 1, D_MODEL), lambda l, b: (l, 0, 0)),
            pl.BlockSpec((1, D_MODEL, 2 * width), lambda l, b: (l, 0, 0)),
            pl.BlockSpec((1, 1, MA_HEAD_DIM), lambda l, b: (l, 0, 0)),
        ],
        out_specs=[pl.BlockSpec((1, 1, m, width), lambda l, b: (l, b, 0, 0)),
                   pl.BlockSpec((1, 1, m, width), lambda l, b: (l, b, 0, 0))],
        out_shape=[out, out],
        compiler_params=_cparams(("parallel", "parallel")),
        name="mem_kv",
    )(mem, mem_norm_g, w_mem_kv, gk)


def _ma_kernel(q_ref, k_ref, v_ref, o_ref):
    for h in range(MA_HEADS):
        cols = slice(h * MA_HEAD_DIM, (h + 1) * MA_HEAD_DIM)
        sc = lax.dot_general(q_ref[0, :, cols], k_ref[0, :, cols], (((1,), (1,)), ((), ())),
                             preferred_element_type=F32)
        e = jnp.exp(sc - jnp.max(sc, axis=-1, keepdims=True))
        p = (e / jnp.sum(e, axis=-1, keepdims=True)).astype(BF16)
        o_ref[0, :, cols] = jnp.dot(p, v_ref[0, :, cols],
                                    preferred_element_type=F32).astype(BF16)


def _memory_attention(u3, kmem, vmem, tq):
    nb, s, _ = u3.shape
    m = kmem.shape[1]
    width = MA_HEADS * MA_HEAD_DIM
    return pl.pallas_call(
        _ma_kernel,
        grid=(nb, s // tq),
        in_specs=[
            pl.BlockSpec((1, tq, width), lambda b, i: (b, i, COL_MA_Q // width)),
            pl.BlockSpec((1, m, width), lambda b, i: (b, 0, 0)),
            pl.BlockSpec((1, m, width), lambda b, i: (b, 0, 0)),
        ],
        out_specs=pl.BlockSpec((1, tq, width), lambda b, i: (b, i, 0)),
        out_shape=jax.ShapeDtypeStruct((nb, s, width), BF16),
        compiler_params=_cparams(("parallel", "parallel")),
        name="mem_attn",
    )(u3, kmem, vmem)


def _merge_kernel(x_ref, g1_ref, da_ref, cv_ref, wa_ref, ma_ref, wg_ref, wb_ref, wo_ref, o_ref):
    x = x_ref[...]
    h = _rms(x, g1_ref[...]).astype(BF16)
    merged = None
    for n, br in enumerate((da_ref, cv_ref, wa_ref, ma_ref)):
        logits = jnp.dot(h, wg_ref[:, n * D_MODEL:(n + 1) * D_MODEL], preferred_element_type=F32)
        t = jax.nn.sigmoid(logits) * jnp.dot(br[...], wb_ref[n], preferred_element_type=F32)
        merged = t if merged is None else merged + t
    o_ref[...] = x + jnp.dot(merged.astype(BF16), wo_ref[...], preferred_element_type=F32)


def _merge(x2d, g1, br_da, br_conv, br_wa, br_ma, w_gate, w_branch, w_out, tm):
    t = x2d.shape[0]
    br = pl.BlockSpec((tm, BRANCH_WIDTH), lambda i: (i, 0))
    return pl.pallas_call(
        _merge_kernel,
        grid=(t // tm,),
        in_specs=[
            pl.BlockSpec((tm, D_MODEL), lambda i: (i, 0)),
            _resident((1, D_MODEL), lambda i: (0, 0)),
            br, br, br, br,
            _resident((D_MODEL, N_BRANCHES * D_MODEL), lambda i: (0, 0)),
            _resident((N_BRANCHES, BRANCH_WIDTH, D_MODEL), lambda i: (0, 0, 0)),
            _resident((D_MODEL, D_MODEL), lambda i: (0, 0)),
        ],
        out_specs=pl.BlockSpec((tm, D_MODEL), lambda i: (i, 0)),
        out_shape=jax.ShapeDtypeStruct((t, D_MODEL), F32),
        compiler_params=_cparams(("parallel",)),
        name="merge",
    )(x2d, g1, br_da, br_conv, br_wa, br_ma, w_gate, w_branch, w_out)


FF_CHUNK = 1024


def _ffn_kernel(x_ref, g2_ref, w1_ref, w2_ref, o_ref):
    x = x_ref[...]
    h = _rms(x, g2_ref[...]).astype(BF16)
    acc = x
    for c in range(D_FF // FF_CHUNK):
        cols = slice(c * FF_CHUNK, (c + 1) * FF_CHUNK)
        f = jnp.maximum(jnp.dot(h, w1_ref[:, cols], preferred_element_type=F32), 0.0)
        acc = acc + jnp.dot((f * f).astype(BF16), w2_ref[cols, :], preferred_element_type=F32)
    o_ref[...] = acc


def _ffn(x2d, g2, w1, w2, tm):
    t = x2d.shape[0]
    return pl.pallas_call(
        _ffn_kernel,
        grid=(t // tm,),
        in_specs=[
            pl.BlockSpec((tm, D_MODEL), lambda i: (i, 0)),
            _resident((1, D_MODEL), lambda i: (0, 0)),
            _resident((D_MODEL, D_FF), lambda i: (0, 0)),
            _resident((D_FF, D_MODEL), lambda i: (0, 0)),
        ],
        out_specs=pl.BlockSpec((tm, D_MODEL), lambda i: (i, 0)),
        out_shape=jax.ShapeDtypeStruct((t, D_MODEL), F32),
        compiler_params=_cparams(("parallel",)),
        name="ffn",
    )(x2d, g2, w1, w2)


def _rel_bucket(rel):
    nb = REL_BUCKETS // 2
    max_exact = nb // 2
    ret = jnp.where(rel > 0, nb, 0)
    n = jnp.abs(rel)
    nf = jnp.maximum(n, 1).astype(F32)
    large = max_exact + (jnp.log(nf / max_exact) / math.log(REL_MAX_DIST / max_exact)
                         * (nb - max_exact)).astype(jnp.int32)
    large = jnp.minimum(large, nb - 1)
    return ret + jnp.where(n < max_exact, n, large)


def _lookup(table, bucket):
    out = jnp.zeros(bucket.shape + (table.shape[1],), F32)
    for b in range(REL_BUCKETS):
        out = jnp.where((bucket == b)[..., None], table[b], out)
    return out


def _da_bias_tables(rel_bias, tq, tk):
    table = rel_bias[:, :2 * DA_HEADS].astype(F32) * LOG2E
    kk = jnp.arange(tk)[None, :, None]
    qq = jnp.arange(tq)[None, None, :]
    dd = (jnp.arange(3) - 1)[:, None, None]
    vals = _lookup(table, _rel_bucket(dd * tk + kk - qq))
    vals = vals.reshape(3, tk, tq, DA_HEADS, 2).transpose(3, 0, 1, 4, 2)
    bias_t = vals.reshape(DA_HEADS, 3, tk, 2 * tq)
    bias_t = jnp.concatenate([jnp.zeros_like(bias_t[:, :1]), bias_t], axis=1)
    nbk = REL_BUCKETS // 2
    far = jnp.stack([table[nbk - 1], jnp.zeros_like(table[0]), table[REL_BUCKETS - 1]])
    far = far.reshape(3, DA_HEADS, 2).transpose(1, 0, 2)
    cfar = jnp.repeat(far, tq, axis=-1).reshape(DA_HEADS, 3, 1, 2 * tq)
    return bias_t, cfar, table


def _da_logits_bounded(table, gq, gk):
    bound = 1.02 * HEAD_DIM * jnp.max(jnp.abs(gq)) * jnp.max(jnp.abs(gk))
    return bound + jnp.max(jnp.abs(table)) <= MAX_LOGIT


def _wa_bias_table(rel_bias):
    table = rel_bias[:, 2 * DA_HEADS:].astype(F32) * LOG2E
    qoff = jnp.arange(BLOCK)
    koff = jnp.arange(3 * BLOCK) - BLOCK
    rel = koff[None, :] - qoff[:, None]
    bias = _lookup(table, _rel_bucket(rel)).transpose(2, 0, 1)
    return bias, jnp.abs(rel) <= WINDOW, table


def _wa_consts(bias, in_window, table, sink2, gq, gk):
    def tiles(b):
        b = jnp.where(in_window[None], b, NEG_INF)
        return b.reshape(WA_HEADS // 2, 2 * BLOCK, 3 * BLOCK)

    bound = 1.02 * HEAD_DIM * jnp.max(jnp.abs(gq)) * jnp.max(jnp.abs(gk))
    cmax, cmin = jnp.max(table, axis=0), jnp.min(table, axis=0)
    shift = jnp.maximum(bound + cmax, sink2)
    safe = jnp.max(shift - jnp.maximum(cmin - bound, sink2)) <= MAX_SHIFT_GAP
    return (tiles(bias), sink2), (tiles(bias - shift[:, None, None]), sink2 - shift), safe


def _dup_heads(w, heads, dim):
    w = w.reshape(w.shape[0], heads, 1, dim)
    return jnp.broadcast_to(w, (w.shape[0], heads, 2, dim)).reshape(w.shape[0], heads * 2 * dim)


def _layer_params(l, w_in, da_qk_g, wa_qk_g, ma_qk_g):
    w = w_in[l]
    wk = w[:, 3072:3200]
    wv = w[:, 3200:3328]
    w_ext = jnp.concatenate([
        w[:, 0:3072],
        _dup_heads(wk, WA_KV_HEADS, HEAD_DIM), _dup_heads(wv, WA_KV_HEADS, HEAD_DIM),
        w[:, 3328:3840]], axis=1).astype(BF16)
    ones = lambda n: jnp.ones((n,), F32)
    gq = da_qk_g[l, 0] * (HEAD_DIM ** -0.5 * LOG2E)
    gk = da_qk_g[l, 1]
    wq = wa_qk_g[l, 0] * (HEAD_DIM ** -0.5 * LOG2E)
    gain = jnp.concatenate([
        jnp.tile(gq, 2 * DA_HEADS),
        jnp.tile(gk, 2 * DA_HEADS),
        ones(512 + 1024),
        jnp.tile(wq, WA_HEADS),
        jnp.tile(wa_qk_g[l, 1], 2 * WA_KV_HEADS),
        ones(256),
        jnp.tile(ma_qk_g[l, 0], MA_HEADS) * MA_HEAD_DIM ** -0.5,
    ]).reshape(1, U_WIDTH).astype(F32)
    w_gate = w[:, GATE_START:].astype(BF16)
    return w_ext, gain, w_gate, (gq, gk), (wq, wa_qk_g[l, 1])


def _tile(n, pref):
    return pref if n % pref == 0 else n


def _trunk(x, mem, rel_bias, norm1_g, w_in, da_qk_g, da_lambda, da_subln_g, conv_w, conv_b,
           conv_ln_g, conv_ln_b, wa_qk_g, wa_sink, mem_norm_g, w_mem_kv, ma_qk_g, w_branch, w_out,
           norm2_g, w_ff1, w_ff2, *, tq=512, tm=512):
    nb, s, _ = x.shape
    tq = _tile(s, tq)
    tk = tq
    tm = _tile(nb * s, tm)
    nk = s // tk
    bias_t, cfar, da_table = _da_bias_tables(rel_bias, tq, tk)
    wa_tables = _wa_bias_table(rel_bias)
    kmem, vmem = _mem_kv(mem, mem_norm_g.reshape(DEPTH, 1, D_MODEL), w_mem_kv.astype(BF16),
                         ma_qk_g[:, 1].reshape(DEPTH, 1, MA_HEAD_DIM))
    x2d = x.reshape(nb * s, D_MODEL)
    for l in range(DEPTH):
        w_ext, gain, w_gate, da_g, wa_g = _layer_params(l, w_in, da_qk_g, wa_qk_g, ma_qk_g)
        g1 = norm1_g[l].reshape(1, D_MODEL)
        u, vt = _inproj(x2d, g1, w_ext, gain, tm, nb, tk)
        u3 = u.reshape(nb, s, U_WIDTH)
        lam_init = 0.8 - 0.6 * math.exp(-0.3 * l)
        da_args = (da_lambda[l], da_subln_g[l].reshape(DA_VDIM, 1), lam_init, tq, tk)
        br_da = lax.cond(
            _da_logits_bounded(da_table, *da_g),
            lambda u3, vt: _diff_attention(u3, vt, bias_t, jnp.exp2(cfar), *da_args, True),
            lambda u3, vt: _diff_attention(u3, vt, bias_t, cfar, *da_args, False),
            u3, vt)
        br_conv = _conv_module(u3, conv_w[l], conv_b[l].reshape(1, CONV_CH),
                               conv_ln_g[l].reshape(1, CONV_CH), conv_ln_b[l].reshape(1, CONV_CH),
                               _tile(s, 512))
        wa_plain, wa_shifted, wa_safe = _wa_consts(*wa_tables, wa_sink[l].astype(F32) * LOG2E,
                                                   *wa_g)
        tw = _tile(s, 512)
        br_wa = lax.cond(
            wa_safe,
            lambda u3: _window_attention(u3, *wa_shifted, tw, True),
            lambda u3: _window_attention(u3, *wa_plain, tw, False),
            u3)
        br_ma = _memory_attention(u3, kmem[l], vmem[l], _tile(s, 1024))
        flat = lambda a: a.reshape(nb * s, BRANCH_WIDTH)
        x2d = _merge(x2d, g1, flat(br_da), flat(br_conv), flat(br_wa), flat(br_ma), w_gate,
                     w_branch[l].astype(BF16), w_out[l].astype(BF16), tm)
        x2d = _ffn(x2d, norm2_g[l].reshape(1, D_MODEL), w_ff1[l].astype(BF16),
                   w_ff2[l].astype(BF16), tm)
    return x2d.reshape(nb, s, D_MODEL)


def kernel(x_prompt, x_sample, mem_prompt, mem_sample, rel_bias, norm1_g, w_in, da_qk_g, da_lambda, da_subln_g, conv_w, conv_b, conv_ln_g, conv_ln_b, wa_qk_g, wa_sink, mem_norm_g, w_mem_kv, ma_qk_g, w_branch, w_out, norm2_g, w_ff1, w_ff2):
    params = (rel_bias, norm1_g, w_in, da_qk_g, da_lambda, da_subln_g, conv_w, conv_b, conv_ln_g,
              conv_ln_b, wa_qk_g, wa_sink, mem_norm_g, w_mem_kv, ma_qk_g, w_branch, w_out,
              norm2_g, w_ff1, w_ff2)
    return (_trunk(x_prompt, mem_prompt, *params), _trunk(x_sample, mem_sample, *params))
```

```python
import functools
import math

import jax
import jax.numpy as jnp
from jax import lax
from jax.experimental import pallas as pl
from jax.experimental.pallas import tpu as pltpu

F32 = jnp.float32
BF16 = jnp.bfloat16

D_MODEL = 1024
DEPTH = 4
HEAD_DIM = 64
DA_HEADS = 4
DA_VDIM = 2 * HEAD_DIM
CONV_CH = 512
CONV_WIDTH = 31
WA_HEADS = 8
WA_KV_HEADS = 2
WINDOW = 128
BLOCK = 128
MEM_TOKENS = 256
MA_HEADS = 4
MA_HEAD_DIM = 128
BRANCH_WIDTH = 512
N_BRANCHES = 4
D_FF = 4 * D_MODEL
REL_BUCKETS = 32
REL_MAX_DIST = 128
EPS = 1e-6
NEG_INF = -1e30

U_WIDTH = 4096
COL_DA_Q, COL_DA_K, COL_DA_V, COL_CONV, COL_WA_Q, COL_WA_K, COL_WA_V, COL_MA_Q = (
    0, 512, 1024, 1536, 2560, 3072, 3328, 3584)
SUB = 256
MXU_COUNT = 2
SUB_SEG = (64, 64, 64, 64, 0, 0, 0, 0, 0, 0, 64, 64, 64, 0, 128, 128)
GATE_START = 3840
SUBLANES = 8
DA_VROWS = DA_VDIM + SUBLANES
DA_GROUP = 16
LOG2E = 1.4426950408889634
MAX_SHIFT_GAP = 100.0
MAX_LOGIT = 60.0

VMEM_LIMIT = 56 * 1024 * 1024


def _cparams(sem):
    return pltpu.CompilerParams(dimension_semantics=sem, vmem_limit_bytes=VMEM_LIMIT)


def _resident(shape, index_map):
    return pl.BlockSpec(shape, index_map, pipeline_mode=pl.Buffered(1))


def _rms(x, g):
    return x * lax.rsqrt(jnp.mean(x * x, axis=-1, keepdims=True) + EPS) * g


def _inproj_kernel(x_ref, g1_ref, w_ref, gain_ref, o_ref, vt_ref):
    tk = vt_ref.shape[-1]
    ones_rows = jnp.where(lax.broadcasted_iota(jnp.int32, (SUBLANES, tk), 0) == 0, 1.0, 0.0)
    h = _rms(x_ref[...], g1_ref[...]).astype(BF16)
    r = lax.broadcasted_iota(jnp.int32, (SUB, SUB), 0)
    c = lax.broadcasted_iota(jnp.int32, (SUB, SUB), 1)
    for s, seg in enumerate(SUB_SEG):
        cols = slice(s * SUB, (s + 1) * SUB)
        if s % MXU_COUNT == 0:
            wide = jnp.dot(h, w_ref[:, s * SUB:(s + MXU_COUNT) * SUB], preferred_element_type=F32)
        u = wide[:, (s % MXU_COUNT) * SUB:(s % MXU_COUNT + 1) * SUB]
        if seg:
            e = jnp.where(r // seg == c // seg, 1.0 / seg, 0.0).astype(BF16)
            ms = jnp.dot((u * u).astype(BF16), e, preferred_element_type=F32)
            u = u * lax.rsqrt(ms + EPS) * gain_ref[:, cols]
        o_ref[:, cols] = u.astype(BF16)
        if s * SUB in range(COL_DA_V, COL_DA_V + DA_HEADS * DA_VDIM, SUB):
            for hh in range(SUB // DA_VDIM):
                head = (s * SUB - COL_DA_V) // DA_VDIM + hh
                for ch in range(vt_ref.shape[2]):
                    blk = u[ch * tk:(ch + 1) * tk, hh * DA_VDIM:(hh + 1) * DA_VDIM]
                    vt_ref[0, head, ch, 0:DA_VDIM, :] = blk.T.astype(BF16)
                    vt_ref[0, head, ch, DA_VDIM:DA_VROWS, :] = ones_rows.astype(BF16)


def _inproj(x2d, g1, w_ext, gain, tm, nb, tk):
    t = x2d.shape[0]
    s = t // nb
    assert tm % tk == 0 and s % tm == 0
    tiles, cpt = s // tm, tm // tk
    return pl.pallas_call(
        _inproj_kernel,
        grid=(t // tm,),
        in_specs=[
            pl.BlockSpec((tm, D_MODEL), lambda i: (i, 0)),
            _resident((1, D_MODEL), lambda i: (0, 0)),
            _resident((D_MODEL, U_WIDTH), lambda i: (0, 0)),
            _resident((1, U_WIDTH), lambda i: (0, 0)),
        ],
        out_specs=[
            pl.BlockSpec((tm, U_WIDTH), lambda i: (i, 0)),
            pl.BlockSpec((1, DA_HEADS, cpt, DA_VROWS, tk),
                         lambda i: (i // tiles, 0, i % tiles, 0, 0)),
        ],
        out_shape=[jax.ShapeDtypeStruct((t, U_WIDTH), BF16),
                   jax.ShapeDtypeStruct((nb, DA_HEADS, s // tk, DA_VROWS, tk), BF16)],
        compiler_params=_cparams(("parallel",)),
        name="inproj",
    )(x2d, g1, w_ext, gain)


def _da_kernel(q_ref, k_ref, vt_ref, bias_ref, cfar_ref, lam_ref, g_ref, *rest,
               tq, tk, nk, lam_init, bounded):
    zero_ref, conv_refs = rest[0], rest[1:11]
    o_ref, oc_ref, q2_s, m_s, acc_s, z_s, zp_s = rest[11:]
    i = pl.program_id(2)
    row_block = i * DA_HEADS + pl.program_id(1)
    conv = _conv_pieces(conv_refs, oc_ref, z_s, zp_s, tq // DA_HEADS,
                        row_block == 0, row_block == pl.num_programs(2) * DA_HEADS - 1)
    group_size = max(c for c in (1, 2, DA_GROUP) if nk % c == 0) if bounded else 1
    if nk != group_size:
        for _ in conv:
            pass
    q = q_ref[0]
    first = lax.broadcasted_iota(jnp.int32, (tq, 2 * HEAD_DIM), 1) < HEAD_DIM
    zero = jnp.zeros_like(q)
    q2_s[0:tq, :] = jnp.where(first, q, zero)
    q2_s[tq:2 * tq, :] = jnp.where(first, zero, q)
    m_s[...] = jnp.full(m_s.shape, NEG_INF, F32)
    acc_s[...] = jnp.zeros(acc_s.shape, F32)

    def group(g, carry):
        pv = None
        for c in range(group_size):
            j = g * group_size + c
            d = j - i
            kind = jnp.where(d < -1, 0, jnp.where(d > 1, 2, 1))
            tile = jnp.where(kind == 1, d + 2, 0)
            start = j * tk if isinstance(j, int) else pl.multiple_of(j * tk, tk)
            kc = k_ref[0, pl.ds(start, tk), :]
            st = lax.dot_general(kc, q2_s[...], (((1,), (1,)), ((), ())),
                                 preferred_element_type=F32)
            st = st + bias_ref[0, tile]
            cvec = cfar_ref[0, kind]
            if bounded:
                p = jnp.exp2(st).astype(BF16)
                tok = next(conv, None)
                if tok is not None:
                    cvec = _order_after(cvec, tok, zero_ref)
                t = jnp.dot(vt_ref[0, 0, j], p, preferred_element_type=F32) * cvec
                pv = t if pv is None else pv + t
            else:
                m_old = m_s[...]
                m_new = jnp.maximum(m_old, jnp.max(st, axis=0, keepdims=True) + cvec)
                alpha = jnp.exp2(m_old - m_new)
                p = jnp.exp2(st - (m_new - cvec)).astype(BF16)
                m_s[...] = m_new
                acc_s[...] = (alpha * acc_s[...]
                              + jnp.dot(vt_ref[0, 0, j], p, preferred_element_type=F32))
        if bounded:
            acc_s[...] += pv
        return carry

    if nk == group_size:
        group(0, 0)
        for _ in conv:
            pass
    else:
        lax.fori_loop(0, nk // group_size, group, 0)

    lp = lam_ref[...]
    lam = (jnp.exp(jnp.sum(lp[0:1] * lp[1:2], keepdims=True))
           - jnp.exp(jnp.sum(lp[2:3] * lp[3:4], keepdims=True)) + lam_init)
    acc = acc_s[...]
    o = acc[0:DA_VDIM] * (1.0 / acc[DA_VDIM:DA_VDIM + 1])
    dd = o[:, 0:tq] - lam * o[:, tq:2 * tq]
    ms = jnp.mean(dd * dd, axis=0, keepdims=True)
    y = dd * lax.rsqrt(ms + EPS) * g_ref[...] * (1.0 - lam_init)
    o_ref[0] = y.T.astype(BF16)


def _diff_attention_and_conv(u3, vt, bias_t, cfar, lam_p, subln_g, conv_params, lam_init, tq, tk,
                             bounded):
    nb, s, _ = u3.shape
    nq, nk = s // tq, s // tk
    assert tq == tk and tq % (DA_HEADS * CONV_HALO) == 0
    kern = functools.partial(_da_kernel, tq=tq, tk=tk, nk=nk, lam_init=lam_init, bounded=bounded)
    conv_in, conv_out, conv_scratch = _conv_specs(tq // DA_HEADS, s, lambda h, i: i * DA_HEADS + h)
    return pl.pallas_call(
        kern,
        grid=(nb, DA_HEADS, nq),
        in_specs=[
            pl.BlockSpec((1, tq, DA_VDIM), lambda b, h, i: (b, i, COL_DA_Q // DA_VDIM + h)),
            pl.BlockSpec((1, s, DA_VDIM), lambda b, h, i: (b, 0, COL_DA_K // DA_VDIM + h)),
            pl.BlockSpec((1, 1, nk, DA_VROWS, tk), lambda b, h, i: (b, h, 0, 0, 0)),
            pl.BlockSpec((1, 4, tk, 2 * tq), lambda b, h, i: (h, 0, 0, 0)),
            pl.BlockSpec((1, 3, 1, 2 * tq), lambda b, h, i: (h, 0, 0, 0)),
            pl.BlockSpec((4, HEAD_DIM), lambda b, h, i: (0, 0)),
            pl.BlockSpec((DA_VDIM, 1), lambda b, h, i: (0, 0)),
            pl.BlockSpec((1, 2 * tq), lambda b, h, i: (0, 0)),
        ] + conv_in,
        out_specs=[pl.BlockSpec((1, tq, DA_VDIM), lambda b, h, i: (b, i, h)), conv_out],
        out_shape=[jax.ShapeDtypeStruct((nb, s, DA_HEADS * DA_VDIM), BF16),
                   jax.ShapeDtypeStruct((nb, s, CONV_CH), BF16)],
        scratch_shapes=[
            pltpu.VMEM((2 * tq, DA_VDIM), BF16),
            pltpu.VMEM((1, 2 * tq), F32),
            pltpu.VMEM((DA_VROWS, 2 * tq), F32),
        ] + conv_scratch,
        compiler_params=_cparams(("parallel", "parallel", "parallel")),
        name="diff_attn_bounded" if bounded else "diff_attn_runmax",
    )(u3, u3, vt, bias_t, cfar, lam_p, subln_g, jnp.zeros((1, 2 * tq), jnp.int32),
      *([u3] * 6), *conv_params)


CONV_HALO = 16
CONV_ROWS = 64
CONV_TAPS_PER_PIECE = 11


def _conv_pieces(refs, o_ref, z_s, zp_s, tc, first, last):
    ap_ref, ac_ref, an_ref, gp_ref, gc_ref, gn_ref, w_ref, b_ref, lg_ref, lb_ref = refs

    def glu(a_ref, g_ref):
        return a_ref[0].astype(F32) * jax.nn.sigmoid(g_ref[0].astype(F32))

    def token(v):
        return jnp.max(v, axis=0, keepdims=True)

    zp, zc, zn = (jnp.where(first, 0.0, glu(ap_ref, gp_ref)), glu(ac_ref, gc_ref),
                  jnp.where(last, 0.0, glu(an_ref, gn_ref)))
    z_s[0:CONV_HALO, :] = zp
    z_s[CONV_HALO:CONV_HALO + tc, :] = zc
    z_s[CONV_HALO + tc:2 * CONV_HALO + tc, :] = zn
    yield jnp.maximum(token(zc), token(zp) + token(zn))

    pad = CONV_WIDTH // 2
    span = tc + 3 * SUBLANES
    for p in range(SUBLANES):
        shifted = z_s[p:p + span, :]
        zp_s[p, 0:span, :] = shifted
        yield token(shifted)

    rows = min(CONV_ROWS, tc)
    for r0 in range(0, tc, rows):
        acc = jnp.zeros((rows, CONV_CH), F32) + b_ref[...]
        for t in range(CONV_WIDTH):
            off = CONV_HALO - pad + t
            p = off % SUBLANES
            acc = acc + w_ref[t:t + 1, :] * zp_s[p, r0 + off - p:r0 + off - p + rows, :]
            if t % CONV_TAPS_PER_PIECE == CONV_TAPS_PER_PIECE - 1:
                yield token(acc)
        mu = jnp.mean(acc, axis=-1, keepdims=True)
        xc = acc - mu
        var = jnp.mean(xc * xc, axis=-1, keepdims=True)
        y = xc * lax.rsqrt(var + EPS) * lg_ref[...] + lb_ref[...]
        y = y * jax.nn.sigmoid(y)
        o_ref[0, r0:r0 + rows, :] = y.astype(BF16)
        yield token(y)


def _order_after(x, tok, zero_ref):
    z = lax.bitcast_convert_type(tok, jnp.int32)
    assert x.shape[-1] % z.shape[-1] == 0
    z = jnp.concatenate([z] * (x.shape[-1] // z.shape[-1]), axis=-1) & zero_ref[...]
    return x + lax.bitcast_convert_type(z, F32)


def _conv_specs(tc, s, row_block):
    hb = tc // CONV_HALO
    nhb = s // CONV_HALO
    ca = COL_CONV // CONV_CH

    def prev(col):
        return pl.BlockSpec((1, CONV_HALO, CONV_CH),
                            lambda b, h, i: (b, jnp.maximum(row_block(h, i) * hb - 1, 0), col))

    def cur(col):
        return pl.BlockSpec((1, tc, CONV_CH), lambda b, h, i: (b, row_block(h, i), col))

    def nxt(col):
        return pl.BlockSpec((1, CONV_HALO, CONV_CH),
                            lambda b, h, i: (b, jnp.minimum((row_block(h, i) + 1) * hb, nhb - 1), col))

    vec = pl.BlockSpec((1, CONV_CH), lambda b, h, i: (0, 0))
    ins = [prev(ca), cur(ca), nxt(ca), prev(ca + 1), cur(ca + 1), nxt(ca + 1),
           pl.BlockSpec((CONV_WIDTH, CONV_CH), lambda b, h, i: (0, 0)), vec, vec, vec]
    out = pl.BlockSpec((1, tc, CONV_CH), lambda b, h, i: (b, row_block(h, i), 0))
    scratch = [pltpu.VMEM((tc + 2 * CONV_HALO, CONV_CH), F32),
               pltpu.VMEM((SUBLANES, tc + 3 * SUBLANES, CONV_CH), F32)]
    return ins, out, scratch


def _wa_kernel(sink_ref, q_ref, kp_ref, kc_ref, kn_ref, vp_ref, vc_ref, vn_ref, bias_ref,
               o_ref, k_s, v_s, *, tw, s_len, fixed_shift):
    i = pl.program_id(1)
    nt = pl.num_programs(1)
    k_s[0:BLOCK, :] = kp_ref[0]
    k_s[BLOCK:BLOCK + tw, :] = kc_ref[0]
    k_s[BLOCK + tw:2 * BLOCK + tw, :] = kn_ref[0]

    nkey = 3 * BLOCK
    pair = 2 * HEAD_DIM
    lane_q = lax.broadcasted_iota(jnp.int32, (BLOCK, pair), 1)
    row2 = lax.broadcasted_iota(jnp.int32, (2 * BLOCK, 1), 0)

    if fixed_shift:
        for kvh in range(WA_KV_HEADS):
            cols = slice(kvh * pair, (kvh + 1) * pair)
            parts = ((0, BLOCK, vp_ref, i > 0), (BLOCK, tw, vc_ref, None),
                     (BLOCK + tw, BLOCK, vn_ref, i < nt - 1))
            for r0, n, ref, valid in parts:
                vals = ref[0, :, cols]
                one = jnp.where(lax.broadcasted_iota(jnp.int32, (n, pair), 1) == 0, 1.0, 0.0)
                one = one.astype(BF16)
                if valid is not None:
                    vals = jnp.where(valid, vals, jnp.zeros_like(vals))
                    one = jnp.where(valid, one, jnp.zeros_like(one))
                v_s[kvh, r0:r0 + n, 0:pair] = vals
                v_s[kvh, r0:r0 + n, pair:2 * pair] = one
    else:
        v_s[0:BLOCK, :] = vp_ref[0]
        v_s[BLOCK:BLOCK + tw, :] = vc_ref[0]
        v_s[BLOCK + tw:2 * BLOCK + tw, :] = vn_ref[0]
        lane_v = lax.broadcasted_iota(jnp.int32, (nkey, pair), 1)
        col = lax.broadcasted_iota(jnp.int32, (1, nkey), 1)

    def blk(sb, carry):
        r0 = pl.multiple_of(sb * BLOCK, BLOCK)
        for jg in range(WA_HEADS // 2):
            kvh = (2 * jg) // (WA_HEADS // WA_KV_HEADS)
            qp = q_ref[0, pl.ds(r0, BLOCK), jg * pair:(jg + 1) * pair]
            zq = jnp.zeros_like(qp)
            q2 = jnp.concatenate([jnp.where(lane_q < HEAD_DIM, qp, zq),
                                  jnp.where(lane_q >= HEAD_DIM, qp, zq)], axis=0)
            kd = k_s[pl.ds(r0, nkey), kvh * pair:(kvh + 1) * pair]
            sc = lax.dot_general(q2, kd, (((1,), (1,)), ((), ())),
                                 preferred_element_type=F32)
            snk = jnp.where(row2 < BLOCK, sink_ref[2 * jg], sink_ref[2 * jg + 1])
            if fixed_shift:
                p = jnp.exp2(sc + bias_ref[jg]).astype(BF16)
                oa = jnp.dot(p, v_s[kvh, pl.ds(r0, nkey), :], preferred_element_type=F32)
                den = oa[:, pair:pair + 1] + jnp.exp2(snk)
                o = oa[:, 0:pair] * (1.0 / den)
                o = jnp.where(lane_q < HEAD_DIM, o[0:BLOCK], o[BLOCK:2 * BLOCK])
            else:
                kpos = i * tw + r0 - BLOCK + col
                inside = jnp.logical_and(kpos >= 0, kpos < s_len)
                sc = jnp.where(inside, sc + bias_ref[jg], NEG_INF)
                m = jnp.maximum(jnp.max(sc, axis=-1, keepdims=True), snk)
                e = jnp.exp2(sc - m)
                den = jnp.sum(e, axis=-1, keepdims=True) + jnp.exp2(snk - m)
                p = (e / den).astype(BF16)
                vd = v_s[pl.ds(r0, nkey), kvh * pair:(kvh + 1) * pair]
                zv = jnp.zeros_like(vd)
                o = (jnp.dot(p[0:BLOCK], jnp.where(lane_v < HEAD_DIM, vd, zv),
                             preferred_element_type=F32)
                     + jnp.dot(p[BLOCK:2 * BLOCK], jnp.where(lane_v >= HEAD_DIM, vd, zv),
                               preferred_element_type=F32))
            o_ref[0, pl.ds(r0, BLOCK), jg * pair:(jg + 1) * pair] = o.astype(BF16)
        return carry

    lax.fori_loop(0, tw // BLOCK, blk, 0)


def _window_attention(u3, wa_bias, sink, tw, fixed_shift):
    nb, s, _ = u3.shape
    nt = s // tw
    bpt = tw // BLOCK
    nblk = s // BLOCK
    width = WA_HEADS * HEAD_DIM
    kvw = 2 * WA_KV_HEADS * HEAD_DIM

    def prev(col):
        return pl.BlockSpec((1, BLOCK, kvw), lambda b, i: (b, jnp.maximum(i * bpt - 1, 0), col))

    def cur(col):
        return pl.BlockSpec((1, tw, kvw), lambda b, i: (b, i, col))

    def nxt(col):
        return pl.BlockSpec((1, BLOCK, kvw),
                            lambda b, i: (b, jnp.minimum((i + 1) * bpt, nblk - 1), col))

    ck, cv = COL_WA_K // kvw, COL_WA_V // kvw
    kern = functools.partial(_wa_kernel, tw=tw, s_len=s, fixed_shift=fixed_shift)
    v_scratch = ((WA_KV_HEADS, tw + 2 * BLOCK, 4 * HEAD_DIM) if fixed_shift
                 else (tw + 2 * BLOCK, kvw))
    return pl.pallas_call(
        kern,
        grid=(nb, nt),
        in_specs=[
            pl.BlockSpec(memory_space=pltpu.SMEM),
            pl.BlockSpec((1, tw, width), lambda b, i: (b, i, COL_WA_Q // width)),
            prev(ck), cur(ck), nxt(ck), prev(cv), cur(cv), nxt(cv),
            pl.BlockSpec((WA_HEADS // 2, 2 * BLOCK, 3 * BLOCK), lambda b, i: (0, 0, 0)),
        ],
        out_specs=pl.BlockSpec((1, tw, width), lambda b, i: (b, i, 0)),
        out_shape=jax.ShapeDtypeStruct((nb, s, width), BF16),
        scratch_shapes=[pltpu.VMEM((tw + 2 * BLOCK, kvw), BF16),
                        pltpu.VMEM(v_scratch, BF16)],
        compiler_params=_cparams(("parallel", "parallel")),
        name="window_attn_fixed" if fixed_shift else "window_attn_runmax",
    )(sink, u3, u3, u3, u3, u3, u3, u3, wa_bias)


def _memkv_kernel(mem_ref, g_ref, w_ref, gk_ref, k_ref, v_ref):
    hn = _rms(mem_ref[0], g_ref[0]).astype(BF16)
    kv = jnp.dot(hn, w_ref[0], preferred_element_type=F32)
    width = MA_HEADS * MA_HEAD_DIM
    for h in range(MA_HEADS):
        cols = slice(h * MA_HEAD_DIM, (h + 1) * MA_HEAD_DIM)
        k_ref[0, 0, :, cols] = _rms(kv[:, cols], gk_ref[0]).astype(BF16)
    v_ref[0, 0] = kv[:, width:2 * width].astype(BF16)


def _mem_kv(mem, mem_norm_g, w_mem_kv, gk):
    nb, m, _ = mem.shape
    width = MA_HEADS * MA_HEAD_DIM
    out = jax.ShapeDtypeStruct((DEPTH, nb, m, width), BF16)
    return pl.pallas_call(
        _memkv_kernel,
        grid=(DEPTH, nb),
        in_specs=[
            pl.BlockSpec((1, m, D_MODEL), lambda l, b: (b, 0, 0)),
            pl.BlockSpec((1, 1, D_MODEL), lambda l, b: (l, 0, 0)),
            pl.BlockSpec((1, D_MODEL, 2 * width), lambda l, b: (l, 0, 0)),
            pl.BlockSpec((1, 1, MA_HEAD_DIM), lambda l, b: (l, 0, 0)),
        ],
        out_specs=[pl.BlockSpec((1, 1, m, width), lambda l, b: (l, b, 0, 0)),
                   pl.BlockSpec((1, 1, m, width), lambda l, b: (l, b, 0, 0))],
        out_shape=[out, out],
        compiler_params=_cparams(("parallel", "parallel")),
        name="mem_kv",
    )(mem, mem_norm_g, w_mem_kv, gk)


def _ma_kernel(q_ref, k_ref, v_ref, o_ref):
    for h in range(MA_HEADS):
        cols = slice(h * MA_HEAD_DIM, (h + 1) * MA_HEAD_DIM)
        sc = lax.dot_general(q_ref[0, :, cols], k_ref[0, :, cols], (((1,), (1,)), ((), ())),
                             preferred_element_type=F32)
        e = jnp.exp(sc - jnp.max(sc, axis=-1, keepdims=True))
        p = (e / jnp.sum(e, axis=-1, keepdims=True)).astype(BF16)
        o_ref[0, :, cols] = jnp.dot(p, v_ref[0, :, cols],
                                    preferred_element_type=F32).astype(BF16)


def _memory_attention(u3, kmem, vmem, tq):
    nb, s, _ = u3.shape
    m = kmem.shape[1]
    width = MA_HEADS * MA_HEAD_DIM
    return pl.pallas_call(
        _ma_kernel,
        grid=(nb, s // tq),
        in_specs=[
            pl.BlockSpec((1, tq, width), lambda b, i: (b, i, COL_MA_Q // width)),
            pl.BlockSpec((1, m, width), lambda b, i: (b, 0, 0)),
            pl.BlockSpec((1, m, width), lambda b, i: (b, 0, 0)),
        ],
        out_specs=pl.BlockSpec((1, tq, width), lambda b, i: (b, i, 0)),
        out_shape=jax.ShapeDtypeStruct((nb, s, width), BF16),
        compiler_params=_cparams(("parallel", "parallel")),
        name="mem_attn",
    )(u3, kmem, vmem)


def _merge_kernel(x_ref, g1_ref, da_ref, cv_ref, wa_ref, ma_ref, wg_ref, wb_ref, wo_ref, o_ref):
    x = x_ref[...]
    h = _rms(x, g1_ref[...]).astype(BF16)
    merged = None
    for n, br in enumerate((da_ref, cv_ref, wa_ref, ma_ref)):
        logits = jnp.dot(h, wg_ref[:, n * D_MODEL:(n + 1) * D_MODEL], preferred_element_type=F32)
        t = jax.nn.sigmoid(logits) * jnp.dot(br[...], wb_ref[n], preferred_element_type=F32)
        merged = t if merged is None else merged + t
    o_ref[...] = x + jnp.dot(merged.astype(BF16), wo_ref[...], preferred_element_type=F32)


def _merge(x2d, g1, br_da, br_conv, br_wa, br_ma, w_gate, w_branch, w_out, tm):
    t = x2d.shape[0]
    br = pl.BlockSpec((tm, BRANCH_WIDTH), lambda i: (i, 0))
    return pl.pallas_call(
        _merge_kernel,
        grid=(t // tm,),
        in_specs=[
            pl.BlockSpec((tm, D_MODEL), lambda i: (i, 0)),
            _resident((1, D_MODEL), lambda i: (0, 0)),
            br, br, br, br,
            _resident((D_MODEL, N_BRANCHES * D_MODEL), lambda i: (0, 0)),
            _resident((N_BRANCHES, BRANCH_WIDTH, D_MODEL), lambda i: (0, 0, 0)),
            _resident((D_MODEL, D_MODEL), lambda i: (0, 0)),
        ],
        out_specs=pl.BlockSpec((tm, D_MODEL), lambda i: (i, 0)),
        out_shape=jax.ShapeDtypeStruct((t, D_MODEL), F32),
        compiler_params=_cparams(("parallel",)),
        name="merge",
    )(x2d, g1, br_da, br_conv, br_wa, br_ma, w_gate, w_branch, w_out)


FF_CHUNK = 1024


def _ffn_kernel(x_ref, g2_ref, w1_ref, w2_ref, o_ref):
    x = x_ref[...]
    h = _rms(x, g2_ref[...]).astype(BF16)
    acc = x
    for c in range(D_FF // FF_CHUNK):
        cols = slice(c * FF_CHUNK, (c + 1) * FF_CHUNK)
        f = jnp.maximum(jnp.dot(h, w1_ref[:, cols], preferred_element_type=F32), 0.0)
        acc = acc + jnp.dot((f * f).astype(BF16), w2_ref[cols, :], preferred_element_type=F32)
    o_ref[...] = acc


def _ffn(x2d, g2, w1, w2, tm):
    t = x2d.shape[0]
    return pl.pallas_call(
        _ffn_kernel,
        grid=(t // tm,),
        in_specs=[
            pl.BlockSpec((tm, D_MODEL), lambda i: (i, 0)),
            _resident((1, D_MODEL), lambda i: (0, 0)),
            _resident((D_MODEL, D_FF), lambda i: (0, 0)),
            _resident((D_FF, D_MODEL), lambda i: (0, 0)),
        ],
        out_specs=pl.BlockSpec((tm, D_MODEL), lambda i: (i, 0)),
        out_shape=jax.ShapeDtypeStruct((t, D_MODEL), F32),
        compiler_params=_cparams(("parallel",)),
        name="ffn",
    )(x2d, g2, w1, w2)


def _rel_bucket(rel):
    nb = REL_BUCKETS // 2
    max_exact = nb // 2
    ret = jnp.where(rel > 0, nb, 0)
    n = jnp.abs(rel)
    nf = jnp.maximum(n, 1).astype(F32)
    large = max_exact + (jnp.log(nf / max_exact) / math.log(REL_MAX_DIST / max_exact)
                         * (nb - max_exact)).astype(jnp.int32)
    large = jnp.minimum(large, nb - 1)
    return ret + jnp.where(n < max_exact, n, large)


def _lookup(table, bucket):
    out = jnp.zeros(bucket.shape + (table.shape[1],), F32)
    for b in range(REL_BUCKETS):
        out = jnp.where((bucket == b)[..., None], table[b], out)
    return out


def _da_bias_tables(rel_bias, tq, tk):
    table = rel_bias[:, :2 * DA_HEADS].astype(F32) * LOG2E
    kk = jnp.arange(tk)[None, :, None]
    qq = jnp.arange(tq)[None, None, :]
    dd = (jnp.arange(3) - 1)[:, None, None]
    vals = _lookup(table, _rel_bucket(dd * tk + kk - qq))
    vals = vals.reshape(3, tk, tq, DA_HEADS, 2).transpose(3, 0, 1, 4, 2)
    bias_t = vals.reshape(DA_HEADS, 3, tk, 2 * tq)
    bias_t = jnp.concatenate([jnp.zeros_like(bias_t[:, :1]), bias_t], axis=1)
    nbk = REL_BUCKETS // 2
    far = jnp.stack([table[nbk - 1], jnp.zeros_like(table[0]), table[REL_BUCKETS - 1]])
    far = far.reshape(3, DA_HEADS, 2).transpose(1, 0, 2)
    cfar = jnp.repeat(far, tq, axis=-1).reshape(DA_HEADS, 3, 1, 2 * tq)
    return bias_t, cfar, table


def _da_logits_bounded(table, gq, gk):
    bound = 1.02 * HEAD_DIM * jnp.max(jnp.abs(gq)) * jnp.max(jnp.abs(gk))
    return bound + jnp.max(jnp.abs(table)) <= MAX_LOGIT


def _wa_bias_table(rel_bias):
    table = rel_bias[:, 2 * DA_HEADS:].astype(F32) * LOG2E
    qoff = jnp.arange(BLOCK)
    koff = jnp.arange(3 * BLOCK) - BLOCK
    rel = koff[None, :] - qoff[:, None]
    bias = _lookup(table, _rel_bucket(rel)).transpose(2, 0, 1)
    return bias, jnp.abs(rel) <= WINDOW, table


def _wa_consts(bias, in_window, table, sink2, gq, gk):
    def tiles(b):
        b = jnp.where(in_window[None], b, NEG_INF)
        return b.reshape(WA_HEADS // 2, 2 * BLOCK, 3 * BLOCK)

    bound = 1.02 * HEAD_DIM * jnp.max(jnp.abs(gq)) * jnp.max(jnp.abs(gk))
    cmax, cmin = jnp.max(table, axis=0), jnp.min(table, axis=0)
    shift = jnp.maximum(bound + cmax, sink2)
    safe = jnp.max(shift - jnp.maximum(cmin - bound, sink2)) <= MAX_SHIFT_GAP
    return (tiles(bias), sink2), (tiles(bias - shift[:, None, None]), sink2 - shift), safe


def _dup_heads(w, heads, dim):
    w = w.reshape(w.shape[0], heads, 1, dim)
    return jnp.broadcast_to(w, (w.shape[0], heads, 2, dim)).reshape(w.shape[0], heads * 2 * dim)


def _layer_params(l, w_in, da_qk_g, wa_qk_g, ma_qk_g):
    w = w_in[l]
    wk = w[:, 3072:3200]
    wv = w[:, 3200:3328]
    w_ext = jnp.concatenate([
        w[:, 0:3072],
        _dup_heads(wk, WA_KV_HEADS, HEAD_DIM), _dup_heads(wv, WA_KV_HEADS, HEAD_DIM),
        w[:, 3328:3840]], axis=1).astype(BF16)
    ones = lambda n: jnp.ones((n,), F32)
    gq = da_qk_g[l, 0] * (HEAD_DIM ** -0.5 * LOG2E)
    gk = da_qk_g[l, 1]
    wq = wa_qk_g[l, 0] * (HEAD_DIM ** -0.5 * LOG2E)
    gain = jnp.concatenate([
        jnp.tile(gq, 2 * DA_HEADS),
        jnp.tile(gk, 2 * DA_HEADS),
        ones(512 + 1024),
        jnp.tile(wq, WA_HEADS),
        jnp.tile(wa_qk_g[l, 1], 2 * WA_KV_HEADS),
        ones(256),
        jnp.tile(ma_qk_g[l, 0], MA_HEADS) * MA_HEAD_DIM ** -0.5,
    ]).reshape(1, U_WIDTH).astype(F32)
    w_gate = w[:, GATE_START:].astype(BF16)
    return w_ext, gain, w_gate, (gq, gk), (wq, wa_qk_g[l, 1])


def _tile(n, pref):
    return pref if n % pref == 0 else n


def _trunk(x, mem, rel_bias, norm1_g, w_in, da_qk_g, da_lambda, da_subln_g, conv_w, conv_b,
           conv_ln_g, conv_ln_b, wa_qk_g, wa_sink, mem_norm_g, w_mem_kv, ma_qk_g, w_branch, w_out,
           norm2_g, w_ff1, w_ff2, *, tq=512, tm=512):
    nb, s, _ = x.shape
    tq = _tile(s, tq)
    tk = tq
    tm = _tile(nb * s, tm)
    nk = s // tk
    bias_t, cfar, da_table = _da_bias_tables(rel_bias, tq, tk)
    wa_tables = _wa_bias_table(rel_bias)
    kmem, vmem = _mem_kv(mem, mem_norm_g.reshape(DEPTH, 1, D_MODEL), w_mem_kv.astype(BF16),
                         ma_qk_g[:, 1].reshape(DEPTH, 1, MA_HEAD_DIM))
    x2d = x.reshape(nb * s, D_MODEL)
    for l in range(DEPTH):
        w_ext, gain, w_gate, da_g, wa_g = _layer_params(l, w_in, da_qk_g, wa_qk_g, ma_qk_g)
        g1 = norm1_g[l].reshape(1, D_MODEL)
        u, vt = _inproj(x2d, g1, w_ext, gain, tm, nb, tk)
        u3 = u.reshape(nb, s, U_WIDTH)
        lam_init = 0.8 - 0.6 * math.exp(-0.3 * l)
        conv_params = (conv_w[l], conv_b[l].reshape(1, CONV_CH), conv_ln_g[l].reshape(1, CONV_CH),
                       conv_ln_b[l].reshape(1, CONV_CH))
        da_args = (da_lambda[l], da_subln_g[l].reshape(DA_VDIM, 1), conv_params, lam_init, tq, tk)
        br_da, br_conv = lax.cond(
            _da_logits_bounded(da_table, *da_g),
            lambda u3, vt: _diff_attention_and_conv(u3, vt, bias_t, jnp.exp2(cfar), *da_args, True),
            lambda u3, vt: _diff_attention_and_conv(u3, vt, bias_t, cfar, *da_args, False),
            u3, vt)
        wa_plain, wa_shifted, wa_safe = _wa_consts(*wa_tables, wa_sink[l].astype(F32) * LOG2E,
                                                   *wa_g)
        tw = _tile(s, 512)
        br_wa = lax.cond(
            wa_safe,
            lambda u3: _window_attention(u3, *wa_shifted, tw, True),
            lambda u3: _window_attention(u3, *wa_plain, tw, False),
            u3)
        br_ma = _memory_attention(u3, kmem[l], vmem[l], _tile(s, 1024))
        flat = lambda a: a.reshape(nb * s, BRANCH_WIDTH)
        x2d = _merge(x2d, g1, flat(br_da), flat(br_conv), flat(br_wa), flat(br_ma), w_gate,
                     w_branch[l].astype(BF16), w_out[l].astype(BF16), tm)
        x2d = _ffn(x2d, norm2_g[l].reshape(1, D_MODEL), w_ff1[l].astype(BF16),
                   w_ff2[l].astype(BF16), tm)
    return x2d.reshape(nb, s, D_MODEL)


def kernel(x_prompt, x_sample, mem_prompt, mem_sample, rel_bias, norm1_g, w_in, da_qk_g, da_lambda, da_subln_g, conv_w, conv_b, conv_ln_g, conv_ln_b, wa_qk_g, wa_sink, mem_norm_g, w_mem_kv, ma_qk_g, w_branch, w_out, norm2_g, w_ff1, w_ff2):
    params = (rel_bias, norm1_g, w_in, da_qk_g, da_lambda, da_subln_g, conv_w, conv_b, conv_ln_g,
              conv_ln_b, wa_qk_g, wa_sink, mem_norm_g, w_mem_kv, ma_qk_g, w_branch, w_out,
              norm2_g, w_ff1, w_ff2)
    return (_trunk(x_prompt, mem_prompt, *params), _trunk(x_sample, mem_sample, *params))
```

```python
import functools
import math

import jax
import jax.numpy as jnp
from jax import lax
from jax.experimental import pallas as pl
from jax.experimental.pallas import tpu as pltpu

F32 = jnp.float32
BF16 = jnp.bfloat16

D_MODEL = 1024
DEPTH = 4
HEAD_DIM = 64
DA_HEADS = 4
DA_VDIM = 2 * HEAD_DIM
CONV_CH = 512
CONV_WIDTH = 31
WA_HEADS = 8
WA_KV_HEADS = 2
WINDOW = 128
BLOCK = 128
MEM_TOKENS = 256
MA_HEADS = 4
MA_HEAD_DIM = 128
BRANCH_WIDTH = 512
N_BRANCHES = 4
D_FF = 4 * D_MODEL
REL_BUCKETS = 32
REL_MAX_DIST = 128
EPS = 1e-6
NEG_INF = -1e30

U_WIDTH = 4096
COL_DA_Q, COL_DA_K, COL_DA_V, COL_CONV, COL_WA_Q, COL_WA_K, COL_WA_V, COL_MA_Q = (
    0, 512, 1024, 1536, 2560, 3072, 3328, 3584)
SUB = 256
MXU_COUNT = 2
SUB_SEG = (64, 64, 64, 64, 0, 0, 0, 0, 0, 0, 64, 64, 64, 0, 128, 128)
GATE_START = 3840
SUBLANES = 8
DA_VROWS = DA_VDIM + SUBLANES
DA_GROUP = 16
LOG2E = 1.4426950408889634
MAX_SHIFT_GAP = 100.0
MAX_LOGIT = 60.0

VMEM_LIMIT = 56 * 1024 * 1024


def _cparams(sem):
    return pltpu.CompilerParams(dimension_semantics=sem, vmem_limit_bytes=VMEM_LIMIT)


def _resident(shape, index_map):
    return pl.BlockSpec(shape, index_map, pipeline_mode=pl.Buffered(1))


def _rms(x, g):
    return x * lax.rsqrt(jnp.mean(x * x, axis=-1, keepdims=True) + EPS) * g


def _inproj_kernel(x_ref, g1_ref, w_ref, gain_ref, o_ref, vt_ref):
    tk = vt_ref.shape[-1]
    ones_rows = jnp.where(lax.broadcasted_iota(jnp.int32, (SUBLANES, tk), 0) == 0, 1.0, 0.0)
    h = _rms(x_ref[...], g1_ref[...]).astype(BF16)
    r = lax.broadcasted_iota(jnp.int32, (SUB, SUB), 0)
    c = lax.broadcasted_iota(jnp.int32, (SUB, SUB), 1)
    for s, seg in enumerate(SUB_SEG):
        cols = slice(s * SUB, (s + 1) * SUB)
        if s % MXU_COUNT == 0:
            wide = jnp.dot(h, w_ref[:, s * SUB:(s + MXU_COUNT) * SUB], preferred_element_type=F32)
        u = wide[:, (s % MXU_COUNT) * SUB:(s % MXU_COUNT + 1) * SUB]
        if seg:
            e = jnp.where(r // seg == c // seg, 1.0 / seg, 0.0).astype(BF16)
            ms = jnp.dot((u * u).astype(BF16), e, preferred_element_type=F32)
            u = u * lax.rsqrt(ms + EPS) * gain_ref[:, cols]
        o_ref[:, cols] = u.astype(BF16)
        if s * SUB in range(COL_DA_V, COL_DA_V + DA_HEADS * DA_VDIM, SUB):
            for hh in range(SUB // DA_VDIM):
                head = (s * SUB - COL_DA_V) // DA_VDIM + hh
                for ch in range(vt_ref.shape[2]):
                    blk = u[ch * tk:(ch + 1) * tk, hh * DA_VDIM:(hh + 1) * DA_VDIM]
                    vt_ref[0, head, ch, 0:DA_VDIM, :] = blk.T.astype(BF16)
                    vt_ref[0, head, ch, DA_VDIM:DA_VROWS, :] = ones_rows.astype(BF16)


def _inproj(x2d, g1, w_ext, gain, tm, nb, tk):
    t = x2d.shape[0]
    s = t // nb
    assert tm % tk == 0 and s % tm == 0
    tiles, cpt = s // tm, tm // tk
    return pl.pallas_call(
        _inproj_kernel,
        grid=(t // tm,),
        in_specs=[
            pl.BlockSpec((tm, D_MODEL), lambda i: (i, 0)),
            _resident((1, D_MODEL), lambda i: (0, 0)),
            _resident((D_MODEL, U_WIDTH), lambda i: (0, 0)),
            _resident((1, U_WIDTH), lambda i: (0, 0)),
        ],
        out_specs=[
            pl.BlockSpec((tm, U_WIDTH), lambda i: (i, 0)),
            pl.BlockSpec((1, DA_HEADS, cpt, DA_VROWS, tk),
                         lambda i: (i // tiles, 0, i % tiles, 0, 0)),
        ],
        out_shape=[jax.ShapeDtypeStruct((t, U_WIDTH), BF16),
                   jax.ShapeDtypeStruct((nb, DA_HEADS, s // tk, DA_VROWS, tk), BF16)],
        compiler_params=_cparams(("parallel",)),
        name="inproj",
    )(x2d, g1, w_ext, gain)


def _da_kernel(q_ref, k_ref, vt_ref, bias_ref, cfar_ref, lam_ref, g_ref, *rest,
               tq, tk, nk, lam_init, bounded):
    zero_ref, conv_refs = rest[0], rest[1:11]
    o_ref, oc_ref, q2_s, m_s, acc_s, z_s, zp_s = rest[11:]
    i = pl.program_id(2)
    row_block = i * DA_HEADS + pl.program_id(1)
    conv = _conv_pieces(conv_refs, oc_ref, z_s, zp_s, tq // DA_HEADS,
                        row_block == 0, row_block == pl.num_programs(2) * DA_HEADS - 1)
    group_size = max(c for c in (1, 2, DA_GROUP) if nk % c == 0) if bounded else 1
    if nk != group_size:
        for _ in conv:
            pass
    q = q_ref[0]
    first = lax.broadcasted_iota(jnp.int32, (tq, 2 * HEAD_DIM), 1) < HEAD_DIM
    zero = jnp.zeros_like(q)
    q2_s[0:tq, :] = jnp.where(first, q, zero)
    q2_s[tq:2 * tq, :] = jnp.where(first, zero, q)
    m_s[...] = jnp.full(m_s.shape, NEG_INF, F32)
    acc_s[...] = jnp.zeros(acc_s.shape, F32)

    def group(g, carry):
        pv = None
        for c in range(group_size):
            j = g * group_size + c
            d = j - i
            kind = jnp.where(d < -1, 0, jnp.where(d > 1, 2, 1))
            tile = jnp.where(kind == 1, d + 2, 0)
            start = j * tk if isinstance(j, int) else pl.multiple_of(j * tk, tk)
            kc = k_ref[0, pl.ds(start, tk), :]
            st = lax.dot_general(kc, q2_s[...], (((1,), (1,)), ((), ())),
                                 preferred_element_type=F32)
            st = st + bias_ref[0, tile]
            cvec = cfar_ref[0, kind]
            if bounded:
                p = jnp.exp2(st).astype(BF16)
                tok = next(conv, None)
                if tok is not None:
                    cvec = _order_after(cvec, tok, zero_ref)
                t = jnp.dot(vt_ref[0, 0, j], p, preferred_element_type=F32) * cvec
                pv = t if pv is None else pv + t
            else:
                m_old = m_s[...]
                m_new = jnp.maximum(m_old, jnp.max(st, axis=0, keepdims=True) + cvec)
                alpha = jnp.exp2(m_old - m_new)
                p = jnp.exp2(st - (m_new - cvec)).astype(BF16)
                m_s[...] = m_new
                acc_s[...] = (alpha * acc_s[...]
                              + jnp.dot(vt_ref[0, 0, j], p, preferred_element_type=F32))
        if bounded:
            acc_s[...] += pv
        return carry

    if nk == group_size:
        group(0, 0)
        for _ in conv:
            pass
    else:
        lax.fori_loop(0, nk // group_size, group, 0)

    lp = lam_ref[...]
    lam = (jnp.exp(jnp.sum(lp[0:1] * lp[1:2], keepdims=True))
           - jnp.exp(jnp.sum(lp[2:3] * lp[3:4], keepdims=True)) + lam_init)
    acc = acc_s[...]
    o = acc[0:DA_VDIM] * (1.0 / acc[DA_VDIM:DA_VDIM + 1])
    dd = o[:, 0:tq] - lam * o[:, tq:2 * tq]
    ms = jnp.mean(dd * dd, axis=0, keepdims=True)
    y = dd * lax.rsqrt(ms + EPS) * g_ref[...] * (1.0 - lam_init)
    o_ref[0] = y.T.astype(BF16)


def _diff_attention_and_conv(u3, vt, bias_t, cfar, lam_p, subln_g, conv_params, lam_init, tq, tk,
                             bounded):
    nb, s, _ = u3.shape
    nq, nk = s // tq, s // tk
    assert tq == tk and tq % (DA_HEADS * CONV_HALO) == 0
    kern = functools.partial(_da_kernel, tq=tq, tk=tk, nk=nk, lam_init=lam_init, bounded=bounded)
    conv_in, conv_out, conv_scratch = _conv_specs(tq // DA_HEADS, s, lambda h, i: i * DA_HEADS + h)
    return pl.pallas_call(
        kern,
        grid=(nb, DA_HEADS, nq),
        in_specs=[
            pl.BlockSpec((1, tq, DA_VDIM), lambda b, h, i: (b, i, COL_DA_Q // DA_VDIM + h)),
            pl.BlockSpec((1, s, DA_VDIM), lambda b, h, i: (b, 0, COL_DA_K // DA_VDIM + h)),
            pl.BlockSpec((1, 1, nk, DA_VROWS, tk), lambda b, h, i: (b, h, 0, 0, 0)),
            pl.BlockSpec((1, 4, tk, 2 * tq), lambda b, h, i: (h, 0, 0, 0)),
            pl.BlockSpec((1, 3, 1, 2 * tq), lambda b, h, i: (h, 0, 0, 0)),
            pl.BlockSpec((4, HEAD_DIM), lambda b, h, i: (0, 0)),
            pl.BlockSpec((DA_VDIM, 1), lambda b, h, i: (0, 0)),
            pl.BlockSpec((1, 2 * tq), lambda b, h, i: (0, 0)),
        ] + conv_in,
        out_specs=[pl.BlockSpec((1, tq, DA_VDIM), lambda b, h, i: (b, i, h)), conv_out],
        out_shape=[jax.ShapeDtypeStruct((nb, s, DA_HEADS * DA_VDIM), BF16),
                   jax.ShapeDtypeStruct((nb, s, CONV_CH), BF16)],
        scratch_shapes=[
            pltpu.VMEM((2 * tq, DA_VDIM), BF16),
            pltpu.VMEM((1, 2 * tq), F32),
            pltpu.VMEM((DA_VROWS, 2 * tq), F32),
        ] + conv_scratch,
        compiler_params=_cparams(("parallel", "parallel", "parallel")),
        name="diff_attn_bounded" if bounded else "diff_attn_runmax",
    )(u3, u3, vt, bias_t, cfar, lam_p, subln_g, jnp.zeros((1, 2 * tq), jnp.int32),
      *([u3] * 6), *conv_params)


CONV_HALO = 16
CONV_ROWS = 64
CONV_TAPS_PER_PIECE = 11


def _conv_pieces(refs, o_ref, z_s, zp_s, tc, first, last):
    ap_ref, ac_ref, an_ref, gp_ref, gc_ref, gn_ref, w_ref, b_ref, lg_ref, lb_ref = refs

    def glu(a_ref, g_ref):
        return a_ref[0].astype(F32) * jax.nn.sigmoid(g_ref[0].astype(F32))

    def token(v):
        return jnp.max(v, axis=0, keepdims=True)

    zp, zc, zn = (jnp.where(first, 0.0, glu(ap_ref, gp_ref)), glu(ac_ref, gc_ref),
                  jnp.where(last, 0.0, glu(an_ref, gn_ref)))
    z_s[0:CONV_HALO, :] = zp
    z_s[CONV_HALO:CONV_HALO + tc, :] = zc
    z_s[CONV_HALO + tc:2 * CONV_HALO + tc, :] = zn
    yield jnp.maximum(token(zc), token(zp) + token(zn))

    pad = CONV_WIDTH // 2
    span = tc + 3 * SUBLANES
    for p in range(SUBLANES):
        shifted = z_s[p:p + span, :]
        zp_s[p, 0:span, :] = shifted
        yield token(shifted)

    rows = min(CONV_ROWS, tc)
    for r0 in range(0, tc, rows):
        acc = jnp.zeros((rows, CONV_CH), F32) + b_ref[...]
        for t in range(CONV_WIDTH):
            off = CONV_HALO - pad + t
            p = off % SUBLANES
            acc = acc + w_ref[t:t + 1, :] * zp_s[p, r0 + off - p:r0 + off - p + rows, :]
            if t % CONV_TAPS_PER_PIECE == CONV_TAPS_PER_PIECE - 1:
                yield token(acc)
        mu = jnp.mean(acc, axis=-1, keepdims=True)
        xc = acc - mu
        var = jnp.mean(xc * xc, axis=-1, keepdims=True)
        y = xc * lax.rsqrt(var + EPS) * lg_ref[...] + lb_ref[...]
        y = y * jax.nn.sigmoid(y)
        o_ref[0, r0:r0 + rows, :] = y.astype(BF16)
        yield token(y)


def _order_after(x, tok, zero_ref):
    z = lax.bitcast_convert_type(tok, jnp.int32)
    assert x.shape[-1] % z.shape[-1] == 0
    z = jnp.concatenate([z] * (x.shape[-1] // z.shape[-1]), axis=-1) & zero_ref[...]
    return x + lax.bitcast_convert_type(z, F32)


def _conv_specs(tc, s, row_block):
    hb = tc // CONV_HALO
    nhb = s // CONV_HALO
    ca = COL_CONV // CONV_CH

    def prev(col):
        return pl.BlockSpec((1, CONV_HALO, CONV_CH),
                            lambda b, h, i: (b, jnp.maximum(row_block(h, i) * hb - 1, 0), col))

    def cur(col):
        return pl.BlockSpec((1, tc, CONV_CH), lambda b, h, i: (b, row_block(h, i), col))

    def nxt(col):
        return pl.BlockSpec((1, CONV_HALO, CONV_CH),
                            lambda b, h, i: (b, jnp.minimum((row_block(h, i) + 1) * hb, nhb - 1), col))

    vec = pl.BlockSpec((1, CONV_CH), lambda b, h, i: (0, 0))
    ins = [prev(ca), cur(ca), nxt(ca), prev(ca + 1), cur(ca + 1), nxt(ca + 1),
           pl.BlockSpec((CONV_WIDTH, CONV_CH), lambda b, h, i: (0, 0)), vec, vec, vec]
    out = pl.BlockSpec((1, tc, CONV_CH), lambda b, h, i: (b, row_block(h, i), 0))
    scratch = [pltpu.VMEM((tc + 2 * CONV_HALO, CONV_CH), F32),
               pltpu.VMEM((SUBLANES, tc + 3 * SUBLANES, CONV_CH), F32)]
    return ins, out, scratch


def _wa_kernel(sink_ref, q_ref, kp_ref, kc_ref, kn_ref, vp_ref, vc_ref, vn_ref, bias_ref,
               o_ref, k_s, v_s, *, tw, s_len, fixed_shift):
    i = pl.program_id(1)
    nt = pl.num_programs(1)
    k_s[0:BLOCK, :] = kp_ref[0]
    k_s[BLOCK:BLOCK + tw, :] = kc_ref[0]
    k_s[BLOCK + tw:2 * BLOCK + tw, :] = kn_ref[0]

    nkey = 3 * BLOCK
    pair = 2 * HEAD_DIM
    lane_q = lax.broadcasted_iota(jnp.int32, (BLOCK, pair), 1)
    row2 = lax.broadcasted_iota(jnp.int32, (2 * BLOCK, 1), 0)

    if fixed_shift:
        for kvh in range(WA_KV_HEADS):
            cols = slice(kvh * pair, (kvh + 1) * pair)
            parts = ((0, BLOCK, vp_ref, i > 0), (BLOCK, tw, vc_ref, None),
                     (BLOCK + tw, BLOCK, vn_ref, i < nt - 1))
            for r0, n, ref, valid in parts:
                vals = ref[0, :, cols]
                one = jnp.where(lax.broadcasted_iota(jnp.int32, (n, pair), 1) == 0, 1.0, 0.0)
                one = one.astype(BF16)
                if valid is not None:
                    vals = jnp.where(valid, vals, jnp.zeros_like(vals))
                    one = jnp.where(valid, one, jnp.zeros_like(one))
                v_s[kvh, r0:r0 + n, 0:pair] = vals
                v_s[kvh, r0:r0 + n, pair:2 * pair] = one
    else:
        v_s[0:BLOCK, :] = vp_ref[0]
        v_s[BLOCK:BLOCK + tw, :] = vc_ref[0]
        v_s[BLOCK + tw:2 * BLOCK + tw, :] = vn_ref[0]
        lane_v = lax.broadcasted_iota(jnp.int32, (nkey, pair), 1)
        col = lax.broadcasted_iota(jnp.int32, (1, nkey), 1)

    def blk(sb, carry):
        r0 = sb * BLOCK if isinstance(sb, int) else pl.multiple_of(sb * BLOCK, BLOCK)
        for jg in range(WA_HEADS // 2):
            kvh = (2 * jg) // (WA_HEADS // WA_KV_HEADS)
            qp = q_ref[0, pl.ds(r0, BLOCK), jg * pair:(jg + 1) * pair]
            zq = jnp.zeros_like(qp)
            q2 = jnp.concatenate([jnp.where(lane_q < HEAD_DIM, qp, zq),
                                  jnp.where(lane_q >= HEAD_DIM, qp, zq)], axis=0)
            kd = k_s[pl.ds(r0, nkey), kvh * pair:(kvh + 1) * pair]
            sc = lax.dot_general(q2, kd, (((1,), (1,)), ((), ())),
                                 preferred_element_type=F32)
            snk = jnp.where(row2 < BLOCK, sink_ref[2 * jg], sink_ref[2 * jg + 1])
            if fixed_shift:
                p = jnp.exp2(sc + bias_ref[jg]).astype(BF16)
                oa = jnp.dot(p, v_s[kvh, pl.ds(r0, nkey), :], preferred_element_type=F32)
                den = oa[:, pair:pair + 1] + jnp.exp2(snk)
                o = oa[:, 0:pair] * (1.0 / den)
                o = jnp.where(lane_q < HEAD_DIM, o[0:BLOCK], o[BLOCK:2 * BLOCK])
            else:
                kpos = i * tw + r0 - BLOCK + col
                inside = jnp.logical_and(kpos >= 0, kpos < s_len)
                sc = jnp.where(inside, sc + bias_ref[jg], NEG_INF)
                m = jnp.maximum(jnp.max(sc, axis=-1, keepdims=True), snk)
                e = jnp.exp2(sc - m)
                den = jnp.sum(e, axis=-1, keepdims=True) + jnp.exp2(snk - m)
                p = (e / den).astype(BF16)
                vd = v_s[pl.ds(r0, nkey), kvh * pair:(kvh + 1) * pair]
                zv = jnp.zeros_like(vd)
                o = (jnp.dot(p[0:BLOCK], jnp.where(lane_v < HEAD_DIM, vd, zv),
                             preferred_element_type=F32)
                     + jnp.dot(p[BLOCK:2 * BLOCK], jnp.where(lane_v >= HEAD_DIM, vd, zv),
                               preferred_element_type=F32))
            o_ref[0, pl.ds(r0, BLOCK), jg * pair:(jg + 1) * pair] = o.astype(BF16)
        return carry

    if fixed_shift:
        for sb in range(tw // BLOCK):
            blk(sb, 0)
    else:
        lax.fori_loop(0, tw // BLOCK, blk, 0)


def _window_attention(u3, wa_bias, sink, tw, fixed_shift):
    nb, s, _ = u3.shape
    nt = s // tw
    bpt = tw // BLOCK
    nblk = s // BLOCK
    width = WA_HEADS * HEAD_DIM
    kvw = 2 * WA_KV_HEADS * HEAD_DIM

    def prev(col):
        return pl.BlockSpec((1, BLOCK, kvw), lambda b, i: (b, jnp.maximum(i * bpt - 1, 0), col))

    def cur(col):
        return pl.BlockSpec((1, tw, kvw), lambda b, i: (b, i, col))

    def nxt(col):
        return pl.BlockSpec((1, BLOCK, kvw),
                            lambda b, i: (b, jnp.minimum((i + 1) * bpt, nblk - 1), col))

    ck, cv = COL_WA_K // kvw, COL_WA_V // kvw
    kern = functools.partial(_wa_kernel, tw=tw, s_len=s, fixed_shift=fixed_shift)
    v_scratch = ((WA_KV_HEADS, tw + 2 * BLOCK, 4 * HEAD_DIM) if fixed_shift
                 else (tw + 2 * BLOCK, kvw))
    return pl.pallas_call(
        kern,
        grid=(nb, nt),
        in_specs=[
            pl.BlockSpec(memory_space=pltpu.SMEM),
            pl.BlockSpec((1, tw, width), lambda b, i: (b, i, COL_WA_Q // width)),
            prev(ck), cur(ck), nxt(ck), prev(cv), cur(cv), nxt(cv),
            pl.BlockSpec((WA_HEADS // 2, 2 * BLOCK, 3 * BLOCK), lambda b, i: (0, 0, 0)),
        ],
        out_specs=pl.BlockSpec((1, tw, width), lambda b, i: (b, i, 0)),
        out_shape=jax.ShapeDtypeStruct((nb, s, width), BF16),
        scratch_shapes=[pltpu.VMEM((tw + 2 * BLOCK, kvw), BF16),
                        pltpu.VMEM(v_scratch, BF16)],
        compiler_params=_cparams(("parallel", "parallel")),
        name="window_attn_fixed" if fixed_shift else "window_attn_runmax",
    )(sink, u3, u3, u3, u3, u3, u3, u3, wa_bias)


def _memkv_kernel(mem_ref, g_ref, w_ref, gk_ref, k_ref, v_ref):
    hn = _rms(mem_ref[0], g_ref[0]).astype(BF16)
    kv = jnp.dot(hn, w_ref[0], preferred_element_type=F32)
    width = MA_HEADS * MA_HEAD_DIM
    ones_col = jnp.where(lax.broadcasted_iota(jnp.int32, (kv.shape[0], MA_HEAD_DIM), 1) == 0, 1.0, 0.0)
    for h in range(MA_HEADS):
        cols = slice(h * MA_HEAD_DIM, (h + 1) * MA_HEAD_DIM)
        k_ref[0, 0, :, cols] = _rms(kv[:, cols], gk_ref[0]).astype(BF16)
        vcols = slice(width + h * MA_HEAD_DIM, width + (h + 1) * MA_HEAD_DIM)
        v_ref[0, 0, :, 2 * h * MA_HEAD_DIM:(2 * h + 1) * MA_HEAD_DIM] = kv[:, vcols].astype(BF16)
        v_ref[0, 0, :, (2 * h + 1) * MA_HEAD_DIM:(2 * h + 2) * MA_HEAD_DIM] = ones_col.astype(BF16)


def _mem_kv(mem, mem_norm_g, w_mem_kv, gk):
    nb, m, _ = mem.shape
    width = MA_HEADS * MA_HEAD_DIM
    out = jax.ShapeDtypeStruct((DEPTH, nb, m, width), BF16)
    out_v = jax.ShapeDtypeStruct((DEPTH, nb, m, 2 * width), BF16)
    return pl.pallas_call(
        _memkv_kernel,
        grid=(DEPTH, nb),
        in_specs=[
            pl.BlockSpec((1, m, D_MODEL), lambda l, b: (b, 0, 0)),
            pl.BlockSpec((1, 1, D_MODEL), lambda l, b: (l, 0, 0)),
            pl.BlockSpec((1, D_MODEL, 2 * width), lambda l, b: (l, 0, 0)),
            pl.BlockSpec((1, 1, MA_HEAD_DIM), lambda l, b: (l, 0, 0)),
        ],
        out_specs=[pl.BlockSpec((1, 1, m, width), lambda l, b: (l, b, 0, 0)),
                   pl.BlockSpec((1, 1, m, 2 * width), lambda l, b: (l, b, 0, 0))],
        out_shape=[out, out_v],
        compiler_params=_cparams(("parallel", "parallel")),
        name="mem_kv",
    )(mem, mem_norm_g, w_mem_kv, gk)


def _ma_kernel(q_ref, k_ref, v_ref, o_ref, *, bounded):
    for h in range(MA_HEADS):
        cols = slice(h * MA_HEAD_DIM, (h + 1) * MA_HEAD_DIM)
        sc = lax.dot_general(q_ref[0, :, cols], k_ref[0, :, cols], (((1,), (1,)), ((), ())),
                             preferred_element_type=F32)
        if bounded:
            oa = jnp.dot(jnp.exp2(sc).astype(BF16),
                         v_ref[0, :, 2 * h * MA_HEAD_DIM:(2 * h + 2) * MA_HEAD_DIM],
                         preferred_element_type=F32)
            o = oa[:, 0:MA_HEAD_DIM] * (1.0 / oa[:, MA_HEAD_DIM:MA_HEAD_DIM + 1])
        else:
            e = jnp.exp2(sc - jnp.max(sc, axis=-1, keepdims=True))
            p = (e / jnp.sum(e, axis=-1, keepdims=True)).astype(BF16)
            o = jnp.dot(p, v_ref[0, :, 2 * h * MA_HEAD_DIM:(2 * h + 1) * MA_HEAD_DIM],
                        preferred_element_type=F32)
        o_ref[0, :, cols] = o.astype(BF16)


def _memory_attention(u3, kmem, vmem, tq, bounded):
    nb, s, _ = u3.shape
    m = kmem.shape[1]
    width = MA_HEADS * MA_HEAD_DIM
    return pl.pallas_call(
        functools.partial(_ma_kernel, bounded=bounded),
        grid=(nb, s // tq),
        in_specs=[
            pl.BlockSpec((1, tq, width), lambda b, i: (b, i, COL_MA_Q // width)),
            pl.BlockSpec((1, m, width), lambda b, i: (b, 0, 0)),
            pl.BlockSpec((1, m, 2 * width), lambda b, i: (b, 0, 0)),
        ],
        out_specs=pl.BlockSpec((1, tq, width), lambda b, i: (b, i, 0)),
        out_shape=jax.ShapeDtypeStruct((nb, s, width), BF16),
        compiler_params=_cparams(("parallel", "parallel")),
        name="mem_attn_bounded" if bounded else "mem_attn_runmax",
    )(u3, kmem, vmem)


def _merge_kernel(x_ref, g1_ref, da_ref, cv_ref, wa_ref, ma_ref, wg_ref, wb_ref, wo_ref, o_ref):
    x = x_ref[...]
    h = _rms(x, g1_ref[...]).astype(BF16)
    merged = None
    for n, br in enumerate((da_ref, cv_ref, wa_ref, ma_ref)):
        logits = jnp.dot(h, wg_ref[:, n * D_MODEL:(n + 1) * D_MODEL], preferred_element_type=F32)
        t = jax.nn.sigmoid(logits) * jnp.dot(br[...], wb_ref[n], preferred_element_type=F32)
        merged = t if merged is None else merged + t
    o_ref[...] = x + jnp.dot(merged.astype(BF16), wo_ref[...], preferred_element_type=F32)


def _merge(x2d, g1, br_da, br_conv, br_wa, br_ma, w_gate, w_branch, w_out, tm):
    t = x2d.shape[0]
    br = pl.BlockSpec((tm, BRANCH_WIDTH), lambda i: (i, 0))
    return pl.pallas_call(
        _merge_kernel,
        grid=(t // tm,),
        in_specs=[
            pl.BlockSpec((tm, D_MODEL), lambda i: (i, 0)),
            _resident((1, D_MODEL), lambda i: (0, 0)),
            br, br, br, br,
            _resident((D_MODEL, N_BRANCHES * D_MODEL), lambda i: (0, 0)),
            _resident((N_BRANCHES, BRANCH_WIDTH, D_MODEL), lambda i: (0, 0, 0)),
            _resident((D_MODEL, D_MODEL), lambda i: (0, 0)),
        ],
        out_specs=pl.BlockSpec((tm, D_MODEL), lambda i: (i, 0)),
        out_shape=jax.ShapeDtypeStruct((t, D_MODEL), F32),
        compiler_params=_cparams(("parallel",)),
        name="merge",
    )(x2d, g1, br_da, br_conv, br_wa, br_ma, w_gate, w_branch, w_out)


FF_CHUNK = 1024


def _ffn_kernel(x_ref, g2_ref, w1_ref, w2_ref, o_ref):
    x = x_ref[...]
    h = _rms(x, g2_ref[...]).astype(BF16)
    acc = x
    for c in range(D_FF // FF_CHUNK):
        cols = slice(c * FF_CHUNK, (c + 1) * FF_CHUNK)
        f = jnp.maximum(jnp.dot(h, w1_ref[:, cols], preferred_element_type=F32), 0.0)
        acc = acc + jnp.dot((f * f).astype(BF16), w2_ref[cols, :], preferred_element_type=F32)
    o_ref[...] = acc


def _ffn(x2d, g2, w1, w2, tm):
    t = x2d.shape[0]
    return pl.pallas_call(
        _ffn_kernel,
        grid=(t // tm,),
        in_specs=[
            pl.BlockSpec((tm, D_MODEL), lambda i: (i, 0)),
            _resident((1, D_MODEL), lambda i: (0, 0)),
            _resident((D_MODEL, D_FF), lambda i: (0, 0)),
            _resident((D_FF, D_MODEL), lambda i: (0, 0)),
        ],
        out_specs=pl.BlockSpec((tm, D_MODEL), lambda i: (i, 0)),
        out_shape=jax.ShapeDtypeStruct((t, D_MODEL), F32),
        compiler_params=_cparams(("parallel",)),
        name="ffn",
    )(x2d, g2, w1, w2)


def _rel_bucket(rel):
    nb = REL_BUCKETS // 2
    max_exact = nb // 2
    ret = jnp.where(rel > 0, nb, 0)
    n = jnp.abs(rel)
    nf = jnp.maximum(n, 1).astype(F32)
    large = max_exact + (jnp.log(nf / max_exact) / math.log(REL_MAX_DIST / max_exact)
                         * (nb - max_exact)).astype(jnp.int32)
    large = jnp.minimum(large, nb - 1)
    return ret + jnp.where(n < max_exact, n, large)


def _lookup(table, bucket):
    out = jnp.zeros(bucket.shape + (table.shape[1],), F32)
    for b in range(REL_BUCKETS):
        out = jnp.where((bucket == b)[..., None], table[b], out)
    return out


def _da_bias_tables(rel_bias, tq, tk):
    table = rel_bias[:, :2 * DA_HEADS].astype(F32) * LOG2E
    kk = jnp.arange(tk)[None, :, None]
    qq = jnp.arange(tq)[None, None, :]
    dd = (jnp.arange(3) - 1)[:, None, None]
    vals = _lookup(table, _rel_bucket(dd * tk + kk - qq))
    vals = vals.reshape(3, tk, tq, DA_HEADS, 2).transpose(3, 0, 1, 4, 2)
    bias_t = vals.reshape(DA_HEADS, 3, tk, 2 * tq)
    bias_t = jnp.concatenate([jnp.zeros_like(bias_t[:, :1]), bias_t], axis=1)
    nbk = REL_BUCKETS // 2
    far = jnp.stack([table[nbk - 1], jnp.zeros_like(table[0]), table[REL_BUCKETS - 1]])
    far = far.reshape(3, DA_HEADS, 2).transpose(1, 0, 2)
    cfar = jnp.repeat(far, tq, axis=-1).reshape(DA_HEADS, 3, 1, 2 * tq)
    return bias_t, cfar, table


def _da_logits_bounded(table, gq, gk):
    bound = 1.02 * HEAD_DIM * jnp.max(jnp.abs(gq)) * jnp.max(jnp.abs(gk))
    return bound + jnp.max(jnp.abs(table)) <= MAX_LOGIT


def _wa_bias_table(rel_bias):
    table = rel_bias[:, 2 * DA_HEADS:].astype(F32) * LOG2E
    qoff = jnp.arange(BLOCK)
    koff = jnp.arange(3 * BLOCK) - BLOCK
    rel = koff[None, :] - qoff[:, None]
    bias = _lookup(table, _rel_bucket(rel)).transpose(2, 0, 1)
    return bias, jnp.abs(rel) <= WINDOW, table


def _wa_consts(bias, in_window, table, sink2, gq, gk):
    def tiles(b):
        b = jnp.where(in_window[None], b, NEG_INF)
        return b.reshape(WA_HEADS // 2, 2 * BLOCK, 3 * BLOCK)

    bound = 1.02 * HEAD_DIM * jnp.max(jnp.abs(gq)) * jnp.max(jnp.abs(gk))
    cmax, cmin = jnp.max(table, axis=0), jnp.min(table, axis=0)
    shift = jnp.maximum(bound + cmax, sink2)
    safe = jnp.max(shift - jnp.maximum(cmin - bound, sink2)) <= MAX_SHIFT_GAP
    return (tiles(bias), sink2), (tiles(bias - shift[:, None, None]), sink2 - shift), safe


def _dup_heads(w, heads, dim):
    w = w.reshape(w.shape[0], heads, 1, dim)
    return jnp.broadcast_to(w, (w.shape[0], heads, 2, dim)).reshape(w.shape[0], heads * 2 * dim)


def _layer_params(l, w_in, da_qk_g, wa_qk_g, ma_qk_g):
    w = w_in[l]
    wk = w[:, 3072:3200]
    wv = w[:, 3200:3328]
    w_ext = jnp.concatenate([
        w[:, 0:3072],
        _dup_heads(wk, WA_KV_HEADS, HEAD_DIM), _dup_heads(wv, WA_KV_HEADS, HEAD_DIM),
        w[:, 3328:3840]], axis=1).astype(BF16)
    ones = lambda n: jnp.ones((n,), F32)
    gq = da_qk_g[l, 0] * (HEAD_DIM ** -0.5 * LOG2E)
    gk = da_qk_g[l, 1]
    wq = wa_qk_g[l, 0] * (HEAD_DIM ** -0.5 * LOG2E)
    mq = ma_qk_g[l, 0] * (MA_HEAD_DIM ** -0.5 * LOG2E)
    gain = jnp.concatenate([
        jnp.tile(gq, 2 * DA_HEADS),
        jnp.tile(gk, 2 * DA_HEADS),
        ones(512 + 1024),
        jnp.tile(wq, WA_HEADS),
        jnp.tile(wa_qk_g[l, 1], 2 * WA_KV_HEADS),
        ones(256),
        jnp.tile(mq, MA_HEADS),
    ]).reshape(1, U_WIDTH).astype(F32)
    w_gate = w[:, GATE_START:].astype(BF16)
    return w_ext, gain, w_gate, (gq, gk), (wq, wa_qk_g[l, 1]), (mq, ma_qk_g[l, 1])


def _tile(n, pref):
    return pref if n % pref == 0 else n


def _trunk(x, mem, rel_bias, norm1_g, w_in, da_qk_g, da_lambda, da_subln_g, conv_w, conv_b,
           conv_ln_g, conv_ln_b, wa_qk_g, wa_sink, mem_norm_g, w_mem_kv, ma_qk_g, w_branch, w_out,
           norm2_g, w_ff1, w_ff2, *, tq=512, tm=512):
    nb, s, _ = x.shape
    tq = _tile(s, tq)
    tk = tq
    tm = _tile(nb * s, tm)
    nk = s // tk
    bias_t, cfar, da_table = _da_bias_tables(rel_bias, tq, tk)
    wa_tables = _wa_bias_table(rel_bias)
    kmem, vmem = _mem_kv(mem, mem_norm_g.reshape(DEPTH, 1, D_MODEL), w_mem_kv.astype(BF16),
                         ma_qk_g[:, 1].reshape(DEPTH, 1, MA_HEAD_DIM))
    x2d = x.reshape(nb * s, D_MODEL)
    for l in range(DEPTH):
        w_ext, gain, w_gate, da_g, wa_g, ma_g = _layer_params(l, w_in, da_qk_g, wa_qk_g, ma_qk_g)
        g1 = norm1_g[l].reshape(1, D_MODEL)
        u, vt = _inproj(x2d, g1, w_ext, gain, tm, nb, tk)
        u3 = u.reshape(nb, s, U_WIDTH)
        lam_init = 0.8 - 0.6 * math.exp(-0.3 * l)
        conv_params = (conv_w[l], conv_b[l].reshape(1, CONV_CH), conv_ln_g[l].reshape(1, CONV_CH),
                       conv_ln_b[l].reshape(1, CONV_CH))
        da_args = (da_lambda[l], da_subln_g[l].reshape(DA_VDIM, 1), conv_params, lam_init, tq, tk)
        br_da, br_conv = lax.cond(
            _da_logits_bounded(da_table, *da_g),
            lambda u3, vt: _diff_attention_and_conv(u3, vt, bias_t, jnp.exp2(cfar), *da_args, True),
            lambda u3, vt: _diff_attention_and_conv(u3, vt, bias_t, cfar, *da_args, False),
            u3, vt)
        wa_plain, wa_shifted, wa_safe = _wa_consts(*wa_tables, wa_sink[l].astype(F32) * LOG2E,
                                                   *wa_g)
        tw = _tile(s, 512)
        br_wa = lax.cond(
            wa_safe,
            lambda u3: _window_attention(u3, *wa_shifted, tw, True),
            lambda u3: _window_attention(u3, *wa_plain, tw, False),
            u3)
        tma = _tile(s, 1024)
        ma_bound = 1.02 * MA_HEAD_DIM * jnp.max(jnp.abs(ma_g[0])) * jnp.max(jnp.abs(ma_g[1]))
        br_ma = lax.cond(
            ma_bound <= MAX_LOGIT,
            lambda u3: _memory_attention(u3, kmem[l], vmem[l], tma, True),
            lambda u3: _memory_attention(u3, kmem[l], vmem[l], tma, False),
            u3)
        flat = lambda a: a.reshape(nb * s, BRANCH_WIDTH)
        x2d = _merge(x2d, g1, flat(br_da), flat(br_conv), flat(br_wa), flat(br_ma), w_gate,
                     w_branch[l].astype(BF16), w_out[l].astype(BF16), tm)
        x2d = _ffn(x2d, norm2_g[l].reshape(1, D_MODEL), w_ff1[l].astype(BF16),
                   w_ff2[l].astype(BF16), tm)
    return x2d.reshape(nb, s, D_MODEL)


def kernel(x_prompt, x_sample, mem_prompt, mem_sample, rel_bias, norm1_g, w_in, da_qk_g, da_lambda, da_subln_g, conv_w, conv_b, conv_ln_g, conv_ln_b, wa_qk_g, wa_sink, mem_norm_g, w_mem_kv, ma_qk_g, w_branch, w_out, norm2_g, w_ff1, w_ff2):
    params = (rel_bias, norm1_g, w_in, da_qk_g, da_lambda, da_subln_g, conv_w, conv_b, conv_ln_g,
              conv_ln_b, wa_qk_g, wa_sink, mem_norm_g, w_mem_kv, ma_qk_g, w_branch, w_out,
              norm2_g, w_ff1, w_ff2)
    return (_trunk(x_prompt, mem_prompt, *params), _trunk(x_sample, mem_sample, *params))
```

```python
import functools
import math

import jax
import jax.numpy as jnp
from jax import lax
from jax.experimental import pallas as pl
from jax.experimental.pallas import tpu as pltpu

F32 = jnp.float32
BF16 = jnp.bfloat16

D_MODEL = 1024
DEPTH = 4
HEAD_DIM = 64
DA_HEADS = 4
DA_VDIM = 2 * HEAD_DIM
CONV_CH = 512
CONV_WIDTH = 31
WA_HEADS = 8
WA_KV_HEADS = 2
WINDOW = 128
BLOCK = 128
MEM_TOKENS = 256
MA_HEADS = 4
MA_HEAD_DIM = 128
BRANCH_WIDTH = 512
N_BRANCHES = 4
D_FF = 4 * D_MODEL
REL_BUCKETS = 32
REL_MAX_DIST = 128
EPS = 1e-6
NEG_INF = -1e30

U_WIDTH = 4096
COL_DA_Q, COL_DA_K, COL_DA_V, COL_CONV, COL_WA_Q, COL_WA_K, COL_WA_V, COL_MA_Q = (
    0, 512, 1024, 1536, 2560, 3072, 3328, 3584)
SUB = 256
MXU_COUNT = 2
SUB_SEG = (64, 64, 64, 64, 0, 0, 0, 0, 0, 0, 64, 64, 64, 0, 128, 128)
GATE_START = 3840
SUBLANES = 8
DA_VROWS = DA_VDIM + SUBLANES
DA_GROUP = 16
LOG2E = 1.4426950408889634
MAX_SHIFT_GAP = 100.0
MAX_LOGIT = 60.0

VMEM_LIMIT = 56 * 1024 * 1024


def _cparams(sem):
    return pltpu.CompilerParams(dimension_semantics=sem, vmem_limit_bytes=VMEM_LIMIT)


def _resident(shape, index_map):
    return pl.BlockSpec(shape, index_map, pipeline_mode=pl.Buffered(1))


def _rms(x, g):
    return x * lax.rsqrt(jnp.mean(x * x, axis=-1, keepdims=True) + EPS) * g


def _inproj_kernel(x_ref, g1_ref, w_ref, gain_ref, o_ref, vt_ref):
    tk = vt_ref.shape[-1]
    ones_rows = jnp.where(lax.broadcasted_iota(jnp.int32, (SUBLANES, tk), 0) == 0, 1.0, 0.0)
    h = _rms(x_ref[...], g1_ref[...]).astype(BF16)
    r = lax.broadcasted_iota(jnp.int32, (SUB, SUB), 0)
    c = lax.broadcasted_iota(jnp.int32, (SUB, SUB), 1)
    for s, seg in enumerate(SUB_SEG):
        cols = slice(s * SUB, (s + 1) * SUB)
        if s % MXU_COUNT == 0:
            wide = jnp.dot(h, w_ref[:, s * SUB:(s + MXU_COUNT) * SUB], preferred_element_type=F32)
        u = wide[:, (s % MXU_COUNT) * SUB:(s % MXU_COUNT + 1) * SUB]
        if seg:
            e = jnp.where(r // seg == c // seg, 1.0 / seg, 0.0).astype(BF16)
            ms = jnp.dot((u * u).astype(BF16), e, preferred_element_type=F32)
            u = u * lax.rsqrt(ms + EPS) * gain_ref[:, cols]
        o_ref[:, cols] = u.astype(BF16)
        if s * SUB in range(COL_DA_V, COL_DA_V + DA_HEADS * DA_VDIM, SUB):
            for hh in range(SUB // DA_VDIM):
                head = (s * SUB - COL_DA_V) // DA_VDIM + hh
                for ch in range(vt_ref.shape[2]):
                    blk = u[ch * tk:(ch + 1) * tk, hh * DA_VDIM:(hh + 1) * DA_VDIM]
                    vt_ref[0, head, ch, 0:DA_VDIM, :] = blk.T.astype(BF16)
                    vt_ref[0, head, ch, DA_VDIM:DA_VROWS, :] = ones_rows.astype(BF16)


def _inproj(x2d, g1, w_ext, gain, tm, nb, tk):
    t = x2d.shape[0]
    s = t // nb
    assert tm % tk == 0 and s % tm == 0
    tiles, cpt = s // tm, tm // tk
    return pl.pallas_call(
        _inproj_kernel,
        grid=(t // tm,),
        in_specs=[
            pl.BlockSpec((tm, D_MODEL), lambda i: (i, 0)),
            _resident((1, D_MODEL), lambda i: (0, 0)),
            _resident((D_MODEL, U_WIDTH), lambda i: (0, 0)),
            _resident((1, U_WIDTH), lambda i: (0, 0)),
        ],
        out_specs=[
            pl.BlockSpec((tm, U_WIDTH), lambda i: (i, 0)),
            pl.BlockSpec((1, DA_HEADS, cpt, DA_VROWS, tk),
                         lambda i: (i // tiles, 0, i % tiles, 0, 0)),
        ],
        out_shape=[jax.ShapeDtypeStruct((t, U_WIDTH), BF16),
                   jax.ShapeDtypeStruct((nb, DA_HEADS, s // tk, DA_VROWS, tk), BF16)],
        compiler_params=_cparams(("parallel",)),
        name="inproj",
    )(x2d, g1, w_ext, gain)


def _da_kernel(q_ref, k_ref, vt_ref, bias_ref, cfar_ref, lam_ref, g_ref, o_ref,
               q2_s, m_s, acc_s, *, tq, tk, nk, lam_init, bounded):
    i = pl.program_id(2)
    q = q_ref[0]
    first = lax.broadcasted_iota(jnp.int32, (tq, 2 * HEAD_DIM), 1) < HEAD_DIM
    zero = jnp.zeros_like(q)
    q2_s[0:tq, :] = jnp.where(first, q, zero)
    q2_s[tq:2 * tq, :] = jnp.where(first, zero, q)
    m_s[...] = jnp.full(m_s.shape, NEG_INF, F32)
    acc_s[...] = jnp.zeros(acc_s.shape, F32)

    def group(g, carry):
        pv = None
        for c in range(group_size):
            j = g * group_size + c
            d = j - i
            kind = jnp.where(d < -1, 0, jnp.where(d > 1, 2, 1))
            tile = jnp.where(kind == 1, d + 2, 0)
            kc = k_ref[0, pl.ds(pl.multiple_of(j * tk, tk), tk), :]
            st = lax.dot_general(kc, q2_s[...], (((1,), (1,)), ((), ())),
                                 preferred_element_type=F32)
            st = st + bias_ref[0, tile]
            cvec = cfar_ref[0, kind]
            if bounded:
                p = jnp.exp2(st).astype(BF16)
                t = jnp.dot(vt_ref[0, 0, j], p, preferred_element_type=F32) * cvec
                pv = t if pv is None else pv + t
            else:
                m_old = m_s[...]
                m_new = jnp.maximum(m_old, jnp.max(st, axis=0, keepdims=True) + cvec)
                alpha = jnp.exp2(m_old - m_new)
                p = jnp.exp2(st - (m_new - cvec)).astype(BF16)
                m_s[...] = m_new
                acc_s[...] = (alpha * acc_s[...]
                              + jnp.dot(vt_ref[0, 0, j], p, preferred_element_type=F32))
        if bounded:
            acc_s[...] += pv
        return carry

    group_size = max(c for c in (1, 2, DA_GROUP) if nk % c == 0) if bounded else 1
    lax.fori_loop(0, nk // group_size, group, 0)

    lp = lam_ref[...]
    lam = (jnp.exp(jnp.sum(lp[0:1] * lp[1:2], keepdims=True))
           - jnp.exp(jnp.sum(lp[2:3] * lp[3:4], keepdims=True)) + lam_init)
    acc = acc_s[...]
    o = acc[0:DA_VDIM] * (1.0 / acc[DA_VDIM:DA_VDIM + 1])
    dd = o[:, 0:tq] - lam * o[:, tq:2 * tq]
    ms = jnp.mean(dd * dd, axis=0, keepdims=True)
    y = dd * lax.rsqrt(ms + EPS) * g_ref[...] * (1.0 - lam_init)
    o_ref[0] = y.T.astype(BF16)


def _diff_attention(u3, vt, bias_t, cfar, lam_p, subln_g, lam_init, tq, tk, bounded):
    nb, s, _ = u3.shape
    nq, nk = s // tq, s // tk
    assert tq == tk
    kern = functools.partial(_da_kernel, tq=tq, tk=tk, nk=nk, lam_init=lam_init, bounded=bounded)
    return pl.pallas_call(
        kern,
        grid=(nb, DA_HEADS, nq),
        in_specs=[
            pl.BlockSpec((1, tq, DA_VDIM), lambda b, h, i: (b, i, COL_DA_Q // DA_VDIM + h)),
            pl.BlockSpec((1, s, DA_VDIM), lambda b, h, i: (b, 0, COL_DA_K // DA_VDIM + h)),
            pl.BlockSpec((1, 1, nk, DA_VROWS, tk), lambda b, h, i: (b, h, 0, 0, 0)),
            pl.BlockSpec((1, 4, tk, 2 * tq), lambda b, h, i: (h, 0, 0, 0)),
            pl.BlockSpec((1, 3, 1, 2 * tq), lambda b, h, i: (h, 0, 0, 0)),
            pl.BlockSpec((4, HEAD_DIM), lambda b, h, i: (0, 0)),
            pl.BlockSpec((DA_VDIM, 1), lambda b, h, i: (0, 0)),
        ],
        out_specs=pl.BlockSpec((1, tq, DA_VDIM), lambda b, h, i: (b, i, h)),
        out_shape=jax.ShapeDtypeStruct((nb, s, DA_HEADS * DA_VDIM), BF16),
        scratch_shapes=[
            pltpu.VMEM((2 * tq, DA_VDIM), BF16),
            pltpu.VMEM((1, 2 * tq), F32),
            pltpu.VMEM((DA_VROWS, 2 * tq), F32),
        ],
        compiler_params=_cparams(("parallel", "parallel", "parallel")),
        name="diff_attn_bounded" if bounded else "diff_attn_runmax",
    )(u3, u3, vt, bias_t, cfar, lam_p, subln_g)


CONV_HALO = 16
CONV_ROWS = 64


def _conv_kernel(ap_ref, ac_ref, an_ref, gp_ref, gc_ref, gn_ref, w_ref, b_ref, lg_ref, lb_ref,
                 o_ref, z_s, zp_s, *, tc, nt):
    i = pl.program_id(1)

    def glu(a_ref, g_ref):
        return a_ref[0].astype(F32) * jax.nn.sigmoid(g_ref[0].astype(F32))

    z_s[0:CONV_HALO, :] = jnp.where(i > 0, glu(ap_ref, gp_ref), 0.0)
    z_s[CONV_HALO:CONV_HALO + tc, :] = glu(ac_ref, gc_ref)
    z_s[CONV_HALO + tc:2 * CONV_HALO + tc, :] = jnp.where(i < nt - 1, glu(an_ref, gn_ref), 0.0)

    pad = CONV_WIDTH // 2
    span = tc + 3 * SUBLANES
    for p in range(SUBLANES):
        zp_s[p, 0:span, :] = z_s[p:p + span, :]

    def rows(r, carry):
        r0 = pl.multiple_of(r * CONV_ROWS, CONV_ROWS)
        acc = jnp.zeros((CONV_ROWS, CONV_CH), F32) + b_ref[...]
        for t in range(CONV_WIDTH):
            off = CONV_HALO - pad + t
            p = off % SUBLANES
            base = pl.multiple_of(r0 + (off - p), SUBLANES)
            acc = acc + w_ref[t:t + 1, :] * zp_s[p, pl.ds(base, CONV_ROWS), :]
        mu = jnp.mean(acc, axis=-1, keepdims=True)
        xc = acc - mu
        var = jnp.mean(xc * xc, axis=-1, keepdims=True)
        y = xc * lax.rsqrt(var + EPS) * lg_ref[...] + lb_ref[...]
        o_ref[0, pl.ds(r0, CONV_ROWS), :] = (y * jax.nn.sigmoid(y)).astype(BF16)
        return carry

    lax.fori_loop(0, tc // CONV_ROWS, rows, 0)


def _conv_module(u3, conv_w, conv_b, ln_g, ln_b, tc):
    nb, s, _ = u3.shape
    nt = s // tc
    hb = tc // CONV_HALO
    nhb = s // CONV_HALO
    ca = COL_CONV // CONV_CH
    cg = ca + 1

    def prev(col):
        return pl.BlockSpec((1, CONV_HALO, CONV_CH),
                            lambda b, i: (b, jnp.maximum(i * hb - 1, 0), col))

    def cur(col):
        return pl.BlockSpec((1, tc, CONV_CH), lambda b, i: (b, i, col))

    def nxt(col):
        return pl.BlockSpec((1, CONV_HALO, CONV_CH),
                            lambda b, i: (b, jnp.minimum((i + 1) * hb, nhb - 1), col))

    vec = pl.BlockSpec((1, CONV_CH), lambda b, i: (0, 0))
    kern = functools.partial(_conv_kernel, tc=tc, nt=nt)
    return pl.pallas_call(
        kern,
        grid=(nb, nt),
        in_specs=[prev(ca), cur(ca), nxt(ca), prev(cg), cur(cg), nxt(cg),
                  pl.BlockSpec((CONV_WIDTH, CONV_CH), lambda b, i: (0, 0)), vec, vec, vec],
        out_specs=pl.BlockSpec((1, tc, CONV_CH), lambda b, i: (b, i, 0)),
        out_shape=jax.ShapeDtypeStruct((nb, s, CONV_CH), BF16),
        scratch_shapes=[pltpu.VMEM((tc + 2 * CONV_HALO, CONV_CH), F32),
                        pltpu.VMEM((SUBLANES, tc + 3 * SUBLANES, CONV_CH), F32)],
        compiler_params=_cparams(("parallel", "parallel")),
        name="conv_module",
    )(u3, u3, u3, u3, u3, u3, conv_w, conv_b, ln_g, ln_b)


def _wa_kernel(sink_ref, q_ref, kp_ref, kc_ref, kn_ref, vp_ref, vc_ref, vn_ref, bias_ref,
               o_ref, k_s, v_s, *, tw, s_len, fixed_shift):
    i = pl.program_id(1)
    nt = pl.num_programs(1)
    k_s[0:BLOCK, :] = kp_ref[0]
    k_s[BLOCK:BLOCK + tw, :] = kc_ref[0]
    k_s[BLOCK + tw:2 * BLOCK + tw, :] = kn_ref[0]

    nkey = 3 * BLOCK
    pair = 2 * HEAD_DIM
    lane_q = lax.broadcasted_iota(jnp.int32, (BLOCK, pair), 1)
    row2 = lax.broadcasted_iota(jnp.int32, (2 * BLOCK, 1), 0)

    if fixed_shift:
        for kvh in range(WA_KV_HEADS):
            cols = slice(kvh * pair, (kvh + 1) * pair)
            parts = ((0, BLOCK, vp_ref, i > 0), (BLOCK, tw, vc_ref, None),
                     (BLOCK + tw, BLOCK, vn_ref, i < nt - 1))
            for r0, n, ref, valid in parts:
                vals = ref[0, :, cols]
                one = jnp.where(lax.broadcasted_iota(jnp.int32, (n, pair), 1) == 0, 1.0, 0.0)
                one = one.astype(BF16)
                if valid is not None:
                    vals = jnp.where(valid, vals, jnp.zeros_like(vals))
                    one = jnp.where(valid, one, jnp.zeros_like(one))
                v_s[kvh, r0:r0 + n, 0:pair] = vals
                v_s[kvh, r0:r0 + n, pair:2 * pair] = one
    else:
        v_s[0:BLOCK, :] = vp_ref[0]
        v_s[BLOCK:BLOCK + tw, :] = vc_ref[0]
        v_s[BLOCK + tw:2 * BLOCK + tw, :] = vn_ref[0]
        lane_v = lax.broadcasted_iota(jnp.int32, (nkey, pair), 1)
        col = lax.broadcasted_iota(jnp.int32, (1, nkey), 1)

    def blk(sb, carry):
        r0 = sb * BLOCK if isinstance(sb, int) else pl.multiple_of(sb * BLOCK, BLOCK)
        for jg in range(WA_HEADS // 2):
            kvh = (2 * jg) // (WA_HEADS // WA_KV_HEADS)
            qp = q_ref[0, pl.ds(r0, BLOCK), jg * pair:(jg + 1) * pair]
            zq = jnp.zeros_like(qp)
            q2 = jnp.concatenate([jnp.where(lane_q < HEAD_DIM, qp, zq),
                                  jnp.where(lane_q >= HEAD_DIM, qp, zq)], axis=0)
            kd = k_s[pl.ds(r0, nkey), kvh * pair:(kvh + 1) * pair]
            sc = lax.dot_general(q2, kd, (((1,), (1,)), ((), ())),
                                 preferred_element_type=F32)
            snk = jnp.where(row2 < BLOCK, sink_ref[2 * jg], sink_ref[2 * jg + 1])
            if fixed_shift:
                p = jnp.exp2(sc + bias_ref[jg]).astype(BF16)
                oa = jnp.dot(p, v_s[kvh, pl.ds(r0, nkey), :], preferred_element_type=F32)
                den = oa[:, pair:pair + 1] + jnp.exp2(snk)
                o = oa[:, 0:pair] * (1.0 / den)
                o = jnp.where(lane_q < HEAD_DIM, o[0:BLOCK], o[BLOCK:2 * BLOCK])
            else:
                kpos = i * tw + r0 - BLOCK + col
                inside = jnp.logical_and(kpos >= 0, kpos < s_len)
                sc = jnp.where(inside, sc + bias_ref[jg], NEG_INF)
                m = jnp.maximum(jnp.max(sc, axis=-1, keepdims=True), snk)
                e = jnp.exp2(sc - m)
                den = jnp.sum(e, axis=-1, keepdims=True) + jnp.exp2(snk - m)
                p = (e / den).astype(BF16)
                vd = v_s[pl.ds(r0, nkey), kvh * pair:(kvh + 1) * pair]
                zv = jnp.zeros_like(vd)
                o = (jnp.dot(p[0:BLOCK], jnp.where(lane_v < HEAD_DIM, vd, zv),
                             preferred_element_type=F32)
                     + jnp.dot(p[BLOCK:2 * BLOCK], jnp.where(lane_v >= HEAD_DIM, vd, zv),
                               preferred_element_type=F32))
            o_ref[0, pl.ds(r0, BLOCK), jg * pair:(jg + 1) * pair] = o.astype(BF16)
        return carry

    if fixed_shift:
        for sb in range(tw // BLOCK):
            blk(sb, 0)
    else:
        lax.fori_loop(0, tw // BLOCK, blk, 0)


def _window_attention(u3, wa_bias, sink, tw, fixed_shift):
    nb, s, _ = u3.shape
    nt = s // tw
    bpt = tw // BLOCK
    nblk = s // BLOCK
    width = WA_HEADS * HEAD_DIM
    kvw = 2 * WA_KV_HEADS * HEAD_DIM

    def prev(col):
        return pl.BlockSpec((1, BLOCK, kvw), lambda b, i: (b, jnp.maximum(i * bpt - 1, 0), col))

    def cur(col):
        return pl.BlockSpec((1, tw, kvw), lambda b, i: (b, i, col))

    def nxt(col):
        return pl.BlockSpec((1, BLOCK, kvw),
                            lambda b, i: (b, jnp.minimum((i + 1) * bpt, nblk - 1), col))

    ck, cv = COL_WA_K // kvw, COL_WA_V // kvw
    kern = functools.partial(_wa_kernel, tw=tw, s_len=s, fixed_shift=fixed_shift)
    v_scratch = ((WA_KV_HEADS, tw + 2 * BLOCK, 4 * HEAD_DIM) if fixed_shift
                 else (tw + 2 * BLOCK, kvw))
    return pl.pallas_call(
        kern,
        grid=(nb, nt),
        in_specs=[
            pl.BlockSpec(memory_space=pltpu.SMEM),
            pl.BlockSpec((1, tw, width), lambda b, i: (b, i, COL_WA_Q // width)),
            prev(ck), cur(ck), nxt(ck), prev(cv), cur(cv), nxt(cv),
            pl.BlockSpec((WA_HEADS // 2, 2 * BLOCK, 3 * BLOCK), lambda b, i: (0, 0, 0)),
        ],
        out_specs=pl.BlockSpec((1, tw, width), lambda b, i: (b, i, 0)),
        out_shape=jax.ShapeDtypeStruct((nb, s, width), BF16),
        scratch_shapes=[pltpu.VMEM((tw + 2 * BLOCK, kvw), BF16),
                        pltpu.VMEM(v_scratch, BF16)],
        compiler_params=_cparams(("parallel", "parallel")),
        name="window_attn_fixed" if fixed_shift else "window_attn_runmax",
    )(sink, u3, u3, u3, u3, u3, u3, u3, wa_bias)


def _memkv_kernel(mem_ref, g_ref, w_ref, gk_ref, k_ref, v_ref):
    hn = _rms(mem_ref[0], g_ref[0]).astype(BF16)
    kv = jnp.dot(hn, w_ref[0], preferred_element_type=F32)
    width = MA_HEADS * MA_HEAD_DIM
    ones_col = jnp.where(lax.broadcasted_iota(jnp.int32, (kv.shape[0], MA_HEAD_DIM), 1) == 0, 1.0, 0.0)
    for h in range(MA_HEADS):
        cols = slice(h * MA_HEAD_DIM, (h + 1) * MA_HEAD_DIM)
        k_ref[0, 0, :, cols] = _rms(kv[:, cols], gk_ref[0]).astype(BF16)
        vcols = slice(width + h * MA_HEAD_DIM, width + (h + 1) * MA_HEAD_DIM)
        v_ref[0, 0, :, 2 * h * MA_HEAD_DIM:(2 * h + 1) * MA_HEAD_DIM] = kv[:, vcols].astype(BF16)
        v_ref[0, 0, :, (2 * h + 1) * MA_HEAD_DIM:(2 * h + 2) * MA_HEAD_DIM] = ones_col.astype(BF16)


def _mem_kv(mem, mem_norm_g, w_mem_kv, gk):
    nb, m, _ = mem.shape
    width = MA_HEADS * MA_HEAD_DIM
    out = jax.ShapeDtypeStruct((DEPTH, nb, m, width), BF16)
    out_v = jax.ShapeDtypeStruct((DEPTH, nb, m, 2 * width), BF16)
    return pl.pallas_call(
        _memkv_kernel,
        grid=(DEPTH, nb),
        in_specs=[
            pl.BlockSpec((1, m, D_MODEL), lambda l, b: (b, 0, 0)),
            pl.BlockSpec((1, 1, D_MODEL), lambda l, b: (l, 0, 0)),
            pl.BlockSpec((1, D_MODEL, 2 * width), lambda l, b: (l, 0, 0)),
            pl.BlockSpec((1, 1, MA_HEAD_DIM), lambda l, b: (l, 0, 0)),
        ],
        out_specs=[pl.BlockSpec((1, 1, m, width), lambda l, b: (l, b, 0, 0)),
                   pl.BlockSpec((1, 1, m, 2 * width), lambda l, b: (l, b, 0, 0))],
        out_shape=[out, out_v],
        compiler_params=_cparams(("parallel", "parallel")),
        name="mem_kv",
    )(mem, mem_norm_g, w_mem_kv, gk)


def _ma_kernel(q_ref, k_ref, v_ref, o_ref, *, bounded):
    for h in range(MA_HEADS):
        cols = slice(h * MA_HEAD_DIM, (h + 1) * MA_HEAD_DIM)
        sc = lax.dot_general(q_ref[0, :, cols], k_ref[0, :, cols], (((1,), (1,)), ((), ())),
                             preferred_element_type=F32)
        if bounded:
            oa = jnp.dot(jnp.exp2(sc).astype(BF16),
                         v_ref[0, :, 2 * h * MA_HEAD_DIM:(2 * h + 2) * MA_HEAD_DIM],
                         preferred_element_type=F32)
            o = oa[:, 0:MA_HEAD_DIM] * (1.0 / oa[:, MA_HEAD_DIM:MA_HEAD_DIM + 1])
        else:
            e = jnp.exp2(sc - jnp.max(sc, axis=-1, keepdims=True))
            p = (e / jnp.sum(e, axis=-1, keepdims=True)).astype(BF16)
            o = jnp.dot(p, v_ref[0, :, 2 * h * MA_HEAD_DIM:(2 * h + 1) * MA_HEAD_DIM],
                        preferred_element_type=F32)
        o_ref[0, :, cols] = o.astype(BF16)


def _memory_attention(u3, kmem, vmem, tq, bounded):
    nb, s, _ = u3.shape
    m = kmem.shape[1]
    width = MA_HEADS * MA_HEAD_DIM
    return pl.pallas_call(
        functools.partial(_ma_kernel, bounded=bounded),
        grid=(nb, s // tq),
        in_specs=[
            pl.BlockSpec((1, tq, width), lambda b, i: (b, i, COL_MA_Q // width)),
            pl.BlockSpec((1, m, width), lambda b, i: (b, 0, 0)),
            pl.BlockSpec((1, m, 2 * width), lambda b, i: (b, 0, 0)),
        ],
        out_specs=pl.BlockSpec((1, tq, width), lambda b, i: (b, i, 0)),
        out_shape=jax.ShapeDtypeStruct((nb, s, width), BF16),
        compiler_params=_cparams(("parallel", "parallel")),
        name="mem_attn_bounded" if bounded else "mem_attn_runmax",
    )(u3, kmem, vmem)


def _merge_kernel(x_ref, g1_ref, da_ref, cv_ref, wa_ref, ma_ref, wg_ref, wb_ref, wo_ref, o_ref):
    x = x_ref[...]
    h = _rms(x, g1_ref[...]).astype(BF16)
    merged = None
    for n, br in enumerate((da_ref, cv_ref, wa_ref, ma_ref)):
        logits = jnp.dot(h, wg_ref[:, n * D_MODEL:(n + 1) * D_MODEL], preferred_element_type=F32)
        t = jax.nn.sigmoid(logits) * jnp.dot(br[...], wb_ref[n], preferred_element_type=F32)
        merged = t if merged is None else merged + t
    o_ref[...] = x + jnp.dot(merged.astype(BF16), wo_ref[...], preferred_element_type=F32)


def _merge(x2d, g1, br_da, br_conv, br_wa, br_ma, w_gate, w_branch, w_out, tm):
    t = x2d.shape[0]
    br = pl.BlockSpec((tm, BRANCH_WIDTH), lambda i: (i, 0))
    return pl.pallas_call(
        _merge_kernel,
        grid=(t // tm,),
        in_specs=[
            pl.BlockSpec((tm, D_MODEL), lambda i: (i, 0)),
            _resident((1, D_MODEL), lambda i: (0, 0)),
            br, br, br, br,
            _resident((D_MODEL, N_BRANCHES * D_MODEL), lambda i: (0, 0)),
            _resident((N_BRANCHES, BRANCH_WIDTH, D_MODEL), lambda i: (0, 0, 0)),
            _resident((D_MODEL, D_MODEL), lambda i: (0, 0)),
        ],
        out_specs=pl.BlockSpec((tm, D_MODEL), lambda i: (i, 0)),
        out_shape=jax.ShapeDtypeStruct((t, D_MODEL), F32),
        compiler_params=_cparams(("parallel",)),
        name="merge",
    )(x2d, g1, br_da, br_conv, br_wa, br_ma, w_gate, w_branch, w_out)


FF_CHUNK = 1024


def _ffn_kernel(x_ref, g2_ref, w1_ref, w2_ref, o_ref):
    x = x_ref[...]
    h = _rms(x, g2_ref[...]).astype(BF16)
    acc = x
    for c in range(D_FF // FF_CHUNK):
        cols = slice(c * FF_CHUNK, (c + 1) * FF_CHUNK)
        f = jnp.maximum(jnp.dot(h, w1_ref[:, cols], preferred_element_type=F32), 0.0)
        acc = acc + jnp.dot((f * f).astype(BF16), w2_ref[cols, :], preferred_element_type=F32)
    o_ref[...] = acc


def _ffn(x2d, g2, w1, w2, tm):
    t = x2d.shape[0]
    return pl.pallas_call(
        _ffn_kernel,
        grid=(t // tm,),
        in_specs=[
            pl.BlockSpec((tm, D_MODEL), lambda i: (i, 0)),
            _resident((1, D_MODEL), lambda i: (0, 0)),
            _resident((D_MODEL, D_FF), lambda i: (0, 0)),
            _resident((D_FF, D_MODEL), lambda i: (0, 0)),
        ],
        out_specs=pl.BlockSpec((tm, D_MODEL), lambda i: (i, 0)),
        out_shape=jax.ShapeDtypeStruct((t, D_MODEL), F32),
        compiler_params=_cparams(("parallel",)),
        name="ffn",
    )(x2d, g2, w1, w2)


def _rel_bucket(rel):
    nb = REL_BUCKETS // 2
    max_exact = nb // 2
    ret = jnp.where(rel > 0, nb, 0)
    n = jnp.abs(rel)
    nf = jnp.maximum(n, 1).astype(F32)
    large = max_exact + (jnp.log(nf / max_exact) / math.log(REL_MAX_DIST / max_exact)
                         * (nb - max_exact)).astype(jnp.int32)
    large = jnp.minimum(large, nb - 1)
    return ret + jnp.where(n < max_exact, n, large)


def _lookup(table, bucket):
    out = jnp.zeros(bucket.shape + (table.shape[1],), F32)
    for b in range(REL_BUCKETS):
        out = jnp.where((bucket == b)[..., None], table[b], out)
    return out


def _da_bias_tables(rel_bias, tq, tk):
    table = rel_bias[:, :2 * DA_HEADS].astype(F32) * LOG2E
    kk = jnp.arange(tk)[None, :, None]
    qq = jnp.arange(tq)[None, None, :]
    dd = (jnp.arange(3) - 1)[:, None, None]
    vals = _lookup(table, _rel_bucket(dd * tk + kk - qq))
    vals = vals.reshape(3, tk, tq, DA_HEADS, 2).transpose(3, 0, 1, 4, 2)
    bias_t = vals.reshape(DA_HEADS, 3, tk, 2 * tq)
    bias_t = jnp.concatenate([jnp.zeros_like(bias_t[:, :1]), bias_t], axis=1)
    nbk = REL_BUCKETS // 2
    far = jnp.stack([table[nbk - 1], jnp.zeros_like(table[0]), table[REL_BUCKETS - 1]])
    far = far.reshape(3, DA_HEADS, 2).transpose(1, 0, 2)
    cfar = jnp.repeat(far, tq, axis=-1).reshape(DA_HEADS, 3, 1, 2 * tq)
    return bias_t, cfar, table


def _da_logits_bounded(table, gq, gk):
    bound = 1.02 * HEAD_DIM * jnp.max(jnp.abs(gq)) * jnp.max(jnp.abs(gk))
    return bound + jnp.max(jnp.abs(table)) <= MAX_LOGIT


def _wa_bias_table(rel_bias):
    table = rel_bias[:, 2 * DA_HEADS:].astype(F32) * LOG2E
    qoff = jnp.arange(BLOCK)
    koff = jnp.arange(3 * BLOCK) - BLOCK
    rel = koff[None, :] - qoff[:, None]
    bias = _lookup(table, _rel_bucket(rel)).transpose(2, 0, 1)
    return bias, jnp.abs(rel) <= WINDOW, table


def _wa_consts(bias, in_window, table, sink2, gq, gk):
    def tiles(b):
        b = jnp.where(in_window[None], b, NEG_INF)
        return b.reshape(WA_HEADS // 2, 2 * BLOCK, 3 * BLOCK)

    bound = 1.02 * HEAD_DIM * jnp.max(jnp.abs(gq)) * jnp.max(jnp.abs(gk))
    cmax, cmin = jnp.max(table, axis=0), jnp.min(table, axis=0)
    shift = jnp.maximum(bound + cmax, sink2)
    safe = jnp.max(shift - jnp.maximum(cmin - bound, sink2)) <= MAX_SHIFT_GAP
    return (tiles(bias), sink2), (tiles(bias - shift[:, None, None]), sink2 - shift), safe


def _dup_heads(w, heads, dim):
    w = w.reshape(w.shape[0], heads, 1, dim)
    return jnp.broadcast_to(w, (w.shape[0], heads, 2, dim)).reshape(w.shape[0], heads * 2 * dim)


def _layer_params(l, w_in, da_qk_g, wa_qk_g, ma_qk_g):
    w = w_in[l]
    wk = w[:, 3072:3200]
    wv = w[:, 3200:3328]
    w_ext = jnp.concatenate([
        w[:, 0:3072],
        _dup_heads(wk, WA_KV_HEADS, HEAD_DIM), _dup_heads(wv, WA_KV_HEADS, HEAD_DIM),
        w[:, 3328:3840]], axis=1).astype(BF16)
    ones = lambda n: jnp.ones((n,), F32)
    gq = da_qk_g[l, 0] * (HEAD_DIM ** -0.5 * LOG2E)
    gk = da_qk_g[l, 1]
    wq = wa_qk_g[l, 0] * (HEAD_DIM ** -0.5 * LOG2E)
    mq = ma_qk_g[l, 0] * (MA_HEAD_DIM ** -0.5 * LOG2E)
    gain = jnp.concatenate([
        jnp.tile(gq, 2 * DA_HEADS),
        jnp.tile(gk, 2 * DA_HEADS),
        ones(512 + 1024),
        jnp.tile(wq, WA_HEADS),
        jnp.tile(wa_qk_g[l, 1], 2 * WA_KV_HEADS),
        ones(256),
        jnp.tile(mq, MA_HEADS),
    ]).reshape(1, U_WIDTH).astype(F32)
    w_gate = w[:, GATE_START:].astype(BF16)
    return w_ext, gain, w_gate, (gq, gk), (wq, wa_qk_g[l, 1]), (mq, ma_qk_g[l, 1])


def _tile(n, pref):
    return pref if n % pref == 0 else n


def _trunk(x, mem, rel_bias, norm1_g, w_in, da_qk_g, da_lambda, da_subln_g, conv_w, conv_b,
           conv_ln_g, conv_ln_b, wa_qk_g, wa_sink, mem_norm_g, w_mem_kv, ma_qk_g, w_branch, w_out,
           norm2_g, w_ff1, w_ff2, *, tq=512, tm=512):
    nb, s, _ = x.shape
    tq = _tile(s, tq)
    tk = tq
    tm = _tile(nb * s, tm)
    nk = s // tk
    bias_t, cfar, da_table = _da_bias_tables(rel_bias, tq, tk)
    wa_tables = _wa_bias_table(rel_bias)
    kmem, vmem = _mem_kv(mem, mem_norm_g.reshape(DEPTH, 1, D_MODEL), w_mem_kv.astype(BF16),
                         ma_qk_g[:, 1].reshape(DEPTH, 1, MA_HEAD_DIM))
    x2d = x.reshape(nb * s, D_MODEL)
    for l in range(DEPTH):
        w_ext, gain, w_gate, da_g, wa_g, ma_g = _layer_params(l, w_in, da_qk_g, wa_qk_g, ma_qk_g)
        g1 = norm1_g[l].reshape(1, D_MODEL)
        u, vt = _inproj(x2d, g1, w_ext, gain, tm, nb, tk)
        u3 = u.reshape(nb, s, U_WIDTH)
        lam_init = 0.8 - 0.6 * math.exp(-0.3 * l)
        da_args = (da_lambda[l], da_subln_g[l].reshape(DA_VDIM, 1), lam_init, tq, tk)
        br_da = lax.cond(
            _da_logits_bounded(da_table, *da_g),
            lambda u3, vt: _diff_attention(u3, vt, bias_t, jnp.exp2(cfar), *da_args, True),
            lambda u3, vt: _diff_attention(u3, vt, bias_t, cfar, *da_args, False),
            u3, vt)
        br_conv = _conv_module(u3, conv_w[l], conv_b[l].reshape(1, CONV_CH),
                               conv_ln_g[l].reshape(1, CONV_CH), conv_ln_b[l].reshape(1, CONV_CH),
                               _tile(s, 512))
        wa_plain, wa_shifted, wa_safe = _wa_consts(*wa_tables, wa_sink[l].astype(F32) * LOG2E,
                                                   *wa_g)
        tw = _tile(s, 512)
        br_wa = lax.cond(
            wa_safe,
            lambda u3: _window_attention(u3, *wa_shifted, tw, True),
            lambda u3: _window_attention(u3, *wa_plain, tw, False),
            u3)
        tma = _tile(s, 1024)
        ma_bound = 1.02 * MA_HEAD_DIM * jnp.max(jnp.abs(ma_g[0])) * jnp.max(jnp.abs(ma_g[1]))
        br_ma = lax.cond(
            ma_bound <= MAX_LOGIT,
            lambda u3: _memory_attention(u3, kmem[l], vmem[l], tma, True),
            lambda u3: _memory_attention(u3, kmem[l], vmem[l], tma, False),
            u3)
        flat = lambda a: a.reshape(nb * s, BRANCH_WIDTH)
        x2d = _merge(x2d, g1, flat(br_da), flat(br_conv), flat(br_wa), flat(br_ma), w_gate,
                     w_branch[l].astype(BF16), w_out[l].astype(BF16), tm)
        x2d = _ffn(x2d, norm2_g[l].reshape(1, D_MODEL), w_ff1[l].astype(BF16),
                   w_ff2[l].astype(BF16), tm)
    return x2d.reshape(nb, s, D_MODEL)


def kernel(x_prompt, x_sample, mem_prompt, mem_sample, rel_bias, norm1_g, w_in, da_qk_g, da_lambda, da_subln_g, conv_w, conv_b, conv_ln_g, conv_ln_b, wa_qk_g, wa_sink, mem_norm_g, w_mem_kv, ma_qk_g, w_branch, w_out, norm2_g, w_ff1, w_ff2):
    params = (rel_bias, norm1_g, w_in, da_qk_g, da_lambda, da_subln_g, conv_w, conv_b, conv_ln_g,
              conv_ln_b, wa_qk_g, wa_sink, mem_norm_g, w_mem_kv, ma_qk_g, w_branch, w_out,
              norm2_g, w_ff1, w_ff2)
    return (_trunk(x_prompt, mem_prompt, *params), _trunk(x_sample, mem_sample, *params))
```

```python
import functools
import math

import jax
import jax.numpy as jnp
from jax import lax
from jax.experimental import pallas as pl
from jax.experimental.pallas import tpu as pltpu

F32 = jnp.float32
BF16 = jnp.bfloat16

D_MODEL = 1024
DEPTH = 4
HEAD_DIM = 64
DA_HEADS = 4
DA_VDIM = 2 * HEAD_DIM
CONV_CH = 512
CONV_WIDTH = 31
WA_HEADS = 8
WA_KV_HEADS = 2
WINDOW = 128
BLOCK = 128
MEM_TOKENS = 256
MA_HEADS = 4
MA_HEAD_DIM = 128
BRANCH_WIDTH = 512
N_BRANCHES = 4
D_FF = 4 * D_MODEL
REL_BUCKETS = 32
REL_MAX_DIST = 128
EPS = 1e-6
NEG_INF = -1e30

U_WIDTH = 4096
COL_DA_Q, COL_DA_K, COL_DA_V, COL_CONV, COL_WA_Q, COL_WA_K, COL_WA_V, COL_MA_Q = (
    0, 512, 1024, 1536, 2560, 3072, 3328, 3584)
SUB = 256
MXU_COUNT = 2
SUB_SEG = (64, 64, 64, 64, 0, 0, 0, 0, 0, 0, 64, 64, 64, 0, 128, 128)
GATE_START = 3840
SUBLANES = 8
DA_VROWS = DA_VDIM + SUBLANES
DA_GROUP = 16
LOG2E = 1.4426950408889634
MAX_SHIFT_GAP = 100.0
MAX_LOGIT = 60.0

VMEM_LIMIT = 56 * 1024 * 1024


def _cparams(sem):
    return pltpu.CompilerParams(dimension_semantics=sem, vmem_limit_bytes=VMEM_LIMIT)


def _resident(shape, index_map):
    return pl.BlockSpec(shape, index_map, pipeline_mode=pl.Buffered(1))


def _rms(x, g):
    return x * lax.rsqrt(jnp.mean(x * x, axis=-1, keepdims=True) + EPS) * g


def _inproj_kernel(x_ref, g1_ref, w_ref, gain_ref, o_ref, vt_ref):
    tk = vt_ref.shape[-1]
    ones_rows = jnp.where(lax.broadcasted_iota(jnp.int32, (SUBLANES, tk), 0) == 0, 1.0, 0.0)
    h = _rms(x_ref[...], g1_ref[...]).astype(BF16)
    r = lax.broadcasted_iota(jnp.int32, (SUB, SUB), 0)
    c = lax.broadcasted_iota(jnp.int32, (SUB, SUB), 1)
    for s, seg in enumerate(SUB_SEG):
        cols = slice(s * SUB, (s + 1) * SUB)
        if s % MXU_COUNT == 0:
            wide = jnp.dot(h, w_ref[:, s * SUB:(s + MXU_COUNT) * SUB], preferred_element_type=F32)
        u = wide[:, (s % MXU_COUNT) * SUB:(s % MXU_COUNT + 1) * SUB]
        if seg:
            e = jnp.where(r // seg == c // seg, 1.0 / seg, 0.0).astype(BF16)
            ms = jnp.dot((u * u).astype(BF16), e, preferred_element_type=F32)
            u = u * lax.rsqrt(ms + EPS) * gain_ref[:, cols]
        o_ref[:, cols] = u.astype(BF16)
        if s * SUB in range(COL_DA_V, COL_DA_V + DA_HEADS * DA_VDIM, SUB):
            for hh in range(SUB // DA_VDIM):
                head = (s * SUB - COL_DA_V) // DA_VDIM + hh
                for ch in range(vt_ref.shape[2]):
                    blk = u[ch * tk:(ch + 1) * tk, hh * DA_VDIM:(hh + 1) * DA_VDIM]
                    vt_ref[0, head, ch, 0:DA_VDIM, :] = blk.T.astype(BF16)
                    vt_ref[0, head, ch, DA_VDIM:DA_VROWS, :] = ones_rows.astype(BF16)


def _inproj(x2d, g1, w_ext, gain, tm, nb, tk):
    t = x2d.shape[0]
    s = t // nb
    assert tm % tk == 0 and s % tm == 0
    tiles, cpt = s // tm, tm // tk
    return pl.pallas_call(
        _inproj_kernel,
        grid=(t // tm,),
        in_specs=[
            pl.BlockSpec((tm, D_MODEL), lambda i: (i, 0)),
            _resident((1, D_MODEL), lambda i: (0, 0)),
            _resident((D_MODEL, U_WIDTH), lambda i: (0, 0)),
            _resident((1, U_WIDTH), lambda i: (0, 0)),
        ],
        out_specs=[
            pl.BlockSpec((tm, U_WIDTH), lambda i: (i, 0)),
            pl.BlockSpec((1, DA_HEADS, cpt, DA_VROWS, tk),
                         lambda i: (i // tiles, 0, i % tiles, 0, 0)),
        ],
        out_shape=[jax.ShapeDtypeStruct((t, U_WIDTH), BF16),
                   jax.ShapeDtypeStruct((nb, DA_HEADS, s // tk, DA_VROWS, tk), BF16)],
        compiler_params=_cparams(("parallel",)),
        name="inproj",
    )(x2d, g1, w_ext, gain)


def _da_kernel(q_ref, k_ref, vt_ref, bias_ref, cfar_ref, lam_ref, g_ref, o_ref,
               q2_s, m_s, acc_s, *, tq, tk, nk, lam_init, bounded):
    i = pl.program_id(2)
    q = q_ref[0]
    first = lax.broadcasted_iota(jnp.int32, (tq, 2 * HEAD_DIM), 1) < HEAD_DIM
    zero = jnp.zeros_like(q)
    q2_s[0:tq, :] = jnp.where(first, q, zero)
    q2_s[tq:2 * tq, :] = jnp.where(first, zero, q)
    m_s[...] = jnp.full(m_s.shape, NEG_INF, F32)
    acc_s[...] = jnp.zeros(acc_s.shape, F32)

    def group(g, carry):
        pv = None
        for c in range(group_size):
            j = g * group_size + c
            d = j - i
            kind = jnp.where(d < -1, 0, jnp.where(d > 1, 2, 1))
            tile = jnp.where(kind == 1, d + 2, 0)
            kc = k_ref[0, pl.ds(pl.multiple_of(j * tk, tk), tk), :]
            st = lax.dot_general(kc, q2_s[...], (((1,), (1,)), ((), ())),
                                 preferred_element_type=F32)
            st = st + bias_ref[0, tile]
            cvec = cfar_ref[0, kind]
            if bounded:
                p = jnp.exp2(st).astype(BF16)
                t = jnp.dot(vt_ref[0, 0, j], p, preferred_element_type=F32) * cvec
                pv = t if pv is None else pv + t
            else:
                m_old = m_s[...]
                m_new = jnp.maximum(m_old, jnp.max(st, axis=0, keepdims=True) + cvec)
                alpha = jnp.exp2(m_old - m_new)
                p = jnp.exp2(st - (m_new - cvec)).astype(BF16)
                m_s[...] = m_new
                acc_s[...] = (alpha * acc_s[...]
                              + jnp.dot(vt_ref[0, 0, j], p, preferred_element_type=F32))
        if bounded:
            acc_s[...] += pv
        return carry

    group_size = max(c for c in (1, 2, DA_GROUP) if nk % c == 0) if bounded else 1
    lax.fori_loop(0, nk // group_size, group, 0)

    lp = lam_ref[...]
    lam = (jnp.exp(jnp.sum(lp[0:1] * lp[1:2], keepdims=True))
           - jnp.exp(jnp.sum(lp[2:3] * lp[3:4], keepdims=True)) + lam_init)
    acc = acc_s[...]
    o = acc[0:DA_VDIM] * (1.0 / acc[DA_VDIM:DA_VDIM + 1])
    dd = o[:, 0:tq] - lam * o[:, tq:2 * tq]
    ms = jnp.mean(dd * dd, axis=0, keepdims=True)
    y = dd * lax.rsqrt(ms + EPS) * g_ref[...] * (1.0 - lam_init)
    o_ref[0] = y.T.astype(BF16)


def _diff_attention(u3, vt, bias_t, cfar, lam_p, subln_g, lam_init, tq, tk, bounded):
    nb, s, _ = u3.shape
    nq, nk = s // tq, s // tk
    assert tq == tk
    kern = functools.partial(_da_kernel, tq=tq, tk=tk, nk=nk, lam_init=lam_init, bounded=bounded)
    return pl.pallas_call(
        kern,
        grid=(nb, DA_HEADS, nq),
        in_specs=[
            pl.BlockSpec((1, tq, DA_VDIM), lambda b, h, i: (b, i, COL_DA_Q // DA_VDIM + h)),
            pl.BlockSpec((1, s, DA_VDIM), lambda b, h, i: (b, 0, COL_DA_K // DA_VDIM + h)),
            pl.BlockSpec((1, 1, nk, DA_VROWS, tk), lambda b, h, i: (b, h, 0, 0, 0)),
            pl.BlockSpec((1, 4, tk, 2 * tq), lambda b, h, i: (h, 0, 0, 0)),
            pl.BlockSpec((1, 3, 1, 2 * tq), lambda b, h, i: (h, 0, 0, 0)),
            pl.BlockSpec((4, HEAD_DIM), lambda b, h, i: (0, 0)),
            pl.BlockSpec((DA_VDIM, 1), lambda b, h, i: (0, 0)),
        ],
        out_specs=pl.BlockSpec((1, tq, DA_VDIM), lambda b, h, i: (b, i, h)),
        out_shape=jax.ShapeDtypeStruct((nb, s, DA_HEADS * DA_VDIM), BF16),
        scratch_shapes=[
            pltpu.VMEM((2 * tq, DA_VDIM), BF16),
            pltpu.VMEM((1, 2 * tq), F32),
            pltpu.VMEM((DA_VROWS, 2 * tq), F32),
        ],
        compiler_params=_cparams(("parallel", "parallel", "parallel")),
        name="diff_attn_bounded" if bounded else "diff_attn_runmax",
    )(u3, u3, vt, bias_t, cfar, lam_p, subln_g)


CONV_HALO = 16
CONV_ROWS = 64


def _conv_kernel(ap_ref, ac_ref, an_ref, gp_ref, gc_ref, gn_ref, w_ref, b_ref, lg_ref, lb_ref,
                 o_ref, z_s, zp_s, *, tc, nt):
    i = pl.program_id(1)

    def glu(a_ref, g_ref):
        return a_ref[0].astype(F32) * jax.nn.sigmoid(g_ref[0].astype(F32))

    z_s[0:CONV_HALO, :] = jnp.where(i > 0, glu(ap_ref, gp_ref), 0.0)
    z_s[CONV_HALO:CONV_HALO + tc, :] = glu(ac_ref, gc_ref)
    z_s[CONV_HALO + tc:2 * CONV_HALO + tc, :] = jnp.where(i < nt - 1, glu(an_ref, gn_ref), 0.0)

    pad = CONV_WIDTH // 2
    span = tc + 3 * SUBLANES
    for p in range(SUBLANES):
        zp_s[p, 0:span, :] = z_s[p:p + span, :]

    def rows(r, carry):
        r0 = pl.multiple_of(r * CONV_ROWS, CONV_ROWS)
        acc = jnp.zeros((CONV_ROWS, CONV_CH), F32) + b_ref[...]
        for t in range(CONV_WIDTH):
            off = CONV_HALO - pad + t
            p = off % SUBLANES
            base = pl.multiple_of(r0 + (off - p), SUBLANES)
            acc = acc + w_ref[t:t + 1, :] * zp_s[p, pl.ds(base, CONV_ROWS), :]
        mu = jnp.mean(acc, axis=-1, keepdims=True)
        xc = acc - mu
        var = jnp.mean(xc * xc, axis=-1, keepdims=True)
        y = xc * lax.rsqrt(var + EPS) * lg_ref[...] + lb_ref[...]
        o_ref[0, pl.ds(r0, CONV_ROWS), :] = (y * jax.nn.sigmoid(y)).astype(BF16)
        return carry

    lax.fori_loop(0, tc // CONV_ROWS, rows, 0)


def _conv_module(u3, conv_w, conv_b, ln_g, ln_b, tc):
    nb, s, _ = u3.shape
    nt = s // tc
    hb = tc // CONV_HALO
    nhb = s // CONV_HALO
    ca = COL_CONV // CONV_CH
    cg = ca + 1

    def prev(col):
        return pl.BlockSpec((1, CONV_HALO, CONV_CH),
                            lambda b, i: (b, jnp.maximum(i * hb - 1, 0), col))

    def cur(col):
        return pl.BlockSpec((1, tc, CONV_CH), lambda b, i: (b, i, col))

    def nxt(col):
        return pl.BlockSpec((1, CONV_HALO, CONV_CH),
                            lambda b, i: (b, jnp.minimum((i + 1) * hb, nhb - 1), col))

    vec = pl.BlockSpec((1, CONV_CH), lambda b, i: (0, 0))
    kern = functools.partial(_conv_kernel, tc=tc, nt=nt)
    return pl.pallas_call(
        kern,
        grid=(nb, nt),
        in_specs=[prev(ca), cur(ca), nxt(ca), prev(cg), cur(cg), nxt(cg),
                  pl.BlockSpec((CONV_WIDTH, CONV_CH), lambda b, i: (0, 0)), vec, vec, vec],
        out_specs=pl.BlockSpec((1, tc, CONV_CH), lambda b, i: (b, i, 0)),
        out_shape=jax.ShapeDtypeStruct((nb, s, CONV_CH), BF16),
        scratch_shapes=[pltpu.VMEM((tc + 2 * CONV_HALO, CONV_CH), F32),
                        pltpu.VMEM((SUBLANES, tc + 3 * SUBLANES, CONV_CH), F32)],
        compiler_params=_cparams(("parallel", "parallel")),
        name="conv_module",
    )(u3, u3, u3, u3, u3, u3, conv_w, conv_b, ln_g, ln_b)


def _wa_kernel(sink_ref, q_ref, kp_ref, kc_ref, kn_ref, vp_ref, vc_ref, vn_ref, bias_ref,
               o_ref, k_s, v_s, *, tw, s_len, fixed_shift):
    i = pl.program_id(1)
    nt = pl.num_programs(1)
    k_s[0:BLOCK, :] = kp_ref[0]
    k_s[BLOCK:BLOCK + tw, :] = kc_ref[0]
    k_s[BLOCK + tw:2 * BLOCK + tw, :] = kn_ref[0]

    nkey = 3 * BLOCK
    pair = 2 * HEAD_DIM
    lane_q = lax.broadcasted_iota(jnp.int32, (BLOCK, pair), 1)
    row2 = lax.broadcasted_iota(jnp.int32, (2 * BLOCK, 1), 0)

    if fixed_shift:
        for kvh in range(WA_KV_HEADS):
            cols = slice(kvh * pair, (kvh + 1) * pair)
            parts = ((0, BLOCK, vp_ref, i > 0), (BLOCK, tw, vc_ref, None),
                     (BLOCK + tw, BLOCK, vn_ref, i < nt - 1))
            for r0, n, ref, valid in parts:
                vals = ref[0, :, cols]
                one = jnp.where(lax.broadcasted_iota(jnp.int32, (n, pair), 1) == 0, 1.0, 0.0)
                one = one.astype(BF16)
                if valid is not None:
                    vals = jnp.where(valid, vals, jnp.zeros_like(vals))
                    one = jnp.where(valid, one, jnp.zeros_like(one))
                v_s[kvh, r0:r0 + n, 0:pair] = vals
                v_s[kvh, r0:r0 + n, pair:2 * pair] = one
    else:
        v_s[0:BLOCK, :] = vp_ref[0]
        v_s[BLOCK:BLOCK + tw, :] = vc_ref[0]
        v_s[BLOCK + tw:2 * BLOCK + tw, :] = vn_ref[0]
        lane_v = lax.broadcasted_iota(jnp.int32, (nkey, pair), 1)
        col = lax.broadcasted_iota(jnp.int32, (1, nkey), 1)

    def blk(sb, carry):
        r0 = sb * BLOCK if isinstance(sb, int) else pl.multiple_of(sb * BLOCK, BLOCK)
        for jg in range(WA_HEADS // 2):
            kvh = (2 * jg) // (WA_HEADS // WA_KV_HEADS)
            qp = q_ref[0, pl.ds(r0, BLOCK), jg * pair:(jg + 1) * pair]
            zq = jnp.zeros_like(qp)
            q2 = jnp.concatenate([jnp.where(lane_q < HEAD_DIM, qp, zq),
                                  jnp.where(lane_q >= HEAD_DIM, qp, zq)], axis=0)
            kd = k_s[pl.ds(r0, nkey), kvh * pair:(kvh + 1) * pair]
            sc = lax.dot_general(q2, kd, (((1,), (1,)), ((), ())),
                                 preferred_element_type=F32)
            snk = jnp.where(row2 < BLOCK, sink_ref[2 * jg], sink_ref[2 * jg + 1])
            if fixed_shift:
                p = jnp.exp2(sc + bias_ref[jg]).astype(BF16)
                oa = jnp.dot(p, v_s[kvh, pl.ds(r0, nkey), :], preferred_element_type=F32)
                den = oa[:, pair:pair + 1] + jnp.exp2(snk)
                o = oa[:, 0:pair] * (1.0 / den)
                o = jnp.where(lane_q < HEAD_DIM, o[0:BLOCK], o[BLOCK:2 * BLOCK])
            else:
                kpos = i * tw + r0 - BLOCK + col
                inside = jnp.logical_and(kpos >= 0, kpos < s_len)
                sc = jnp.where(inside, sc + bias_ref[jg], NEG_INF)
                m = jnp.maximum(jnp.max(sc, axis=-1, keepdims=True), snk)
                e = jnp.exp2(sc - m)
                den = jnp.sum(e, axis=-1, keepdims=True) + jnp.exp2(snk - m)
                p = (e / den).astype(BF16)
                vd = v_s[pl.ds(r0, nkey), kvh * pair:(kvh + 1) * pair]
                zv = jnp.zeros_like(vd)
                o = (jnp.dot(p[0:BLOCK], jnp.where(lane_v < HEAD_DIM, vd, zv),
                             preferred_element_type=F32)
                     + jnp.dot(p[BLOCK:2 * BLOCK], jnp.where(lane_v >= HEAD_DIM, vd, zv),
                               preferred_element_type=F32))
            o_ref[0, pl.ds(r0, BLOCK), jg * pair:(jg + 1) * pair] = o.astype(BF16)
        return carry

    if fixed_shift:
        for sb in range(tw // BLOCK):
            blk(sb, 0)
    else:
        lax.fori_loop(0, tw // BLOCK, blk, 0)


def _window_attention(u3, wa_bias, sink, tw, fixed_shift):
    nb, s, _ = u3.shape
    nt = s // tw
    bpt = tw // BLOCK
    nblk = s // BLOCK
    width = WA_HEADS * HEAD_DIM
    kvw = 2 * WA_KV_HEADS * HEAD_DIM

    def prev(col):
        return pl.BlockSpec((1, BLOCK, kvw), lambda b, i: (b, jnp.maximum(i * bpt - 1, 0), col))

    def cur(col):
        return pl.BlockSpec((1, tw, kvw), lambda b, i: (b, i, col))

    def nxt(col):
        return pl.BlockSpec((1, BLOCK, kvw),
                            lambda b, i: (b, jnp.minimum((i + 1) * bpt, nblk - 1), col))

    ck, cv = COL_WA_K // kvw, COL_WA_V // kvw
    kern = functools.partial(_wa_kernel, tw=tw, s_len=s, fixed_shift=fixed_shift)
    v_scratch = ((WA_KV_HEADS, tw + 2 * BLOCK, 4 * HEAD_DIM) if fixed_shift
                 else (tw + 2 * BLOCK, kvw))
    return pl.pallas_call(
        kern,
        grid=(nb, nt),
        in_specs=[
            pl.BlockSpec(memory_space=pltpu.SMEM),
            pl.BlockSpec((1, tw, width), lambda b, i: (b, i, COL_WA_Q // width)),
            prev(ck), cur(ck), nxt(ck), prev(cv), cur(cv), nxt(cv),
            pl.BlockSpec((WA_HEADS // 2, 2 * BLOCK, 3 * BLOCK), lambda b, i: (0, 0, 0)),
        ],
        out_specs=pl.BlockSpec((1, tw, width), lambda b, i: (b, i, 0)),
        out_shape=jax.ShapeDtypeStruct((nb, s, width), BF16),
        scratch_shapes=[pltpu.VMEM((tw + 2 * BLOCK, kvw), BF16),
                        pltpu.VMEM(v_scratch, BF16)],
        compiler_params=_cparams(("parallel", "parallel")),
        name="window_attn_fixed" if fixed_shift else "window_attn_runmax",
    )(sink, u3, u3, u3, u3, u3, u3, u3, wa_bias)


def _memkv_kernel(mem_ref, g_ref, w_ref, gk_ref, k_ref, v_ref):
    hn = _rms(mem_ref[0], g_ref[0]).astype(BF16)
    kv = jnp.dot(hn, w_ref[0], preferred_element_type=F32)
    width = MA_HEADS * MA_HEAD_DIM
    ones_col = jnp.where(lax.broadcasted_iota(jnp.int32, (kv.shape[0], MA_HEAD_DIM), 1) == 0, 1.0, 0.0)
    for h in range(MA_HEADS):
        cols = slice(h * MA_HEAD_DIM, (h + 1) * MA_HEAD_DIM)
        k_ref[0, 0, :, cols] = _rms(kv[:, cols], gk_ref[0]).astype(BF16)
        vcols = slice(width + h * MA_HEAD_DIM, width + (h + 1) * MA_HEAD_DIM)
        v_ref[0, 0, :, 2 * h * MA_HEAD_DIM:(2 * h + 1) * MA_HEAD_DIM] = kv[:, vcols].astype(BF16)
        v_ref[0, 0, :, (2 * h + 1) * MA_HEAD_DIM:(2 * h + 2) * MA_HEAD_DIM] = ones_col.astype(BF16)


def _mem_kv(mem, mem_norm_g, w_mem_kv, gk):
    nb, m, _ = mem.shape
    width = MA_HEADS * MA_HEAD_DIM
    out = jax.ShapeDtypeStruct((DEPTH, nb, m, width), BF16)
    out_v = jax.ShapeDtypeStruct((DEPTH, nb, m, 2 * width), BF16)
    return pl.pallas_call(
        _memkv_kernel,
        grid=(DEPTH, nb),
        in_specs=[
            pl.BlockSpec((1, m, D_MODEL), lambda l, b: (b, 0, 0)),
            pl.BlockSpec((1, 1, D_MODEL), lambda l, b: (l, 0, 0)),
            pl.BlockSpec((1, D_MODEL, 2 * width), lambda l, b: (l, 0, 0)),
            pl.BlockSpec((1, 1, MA_HEAD_DIM), lambda l, b: (l, 0, 0)),
        ],
        out_specs=[pl.BlockSpec((1, 1, m, width), lambda l, b: (l, b, 0, 0)),
                   pl.BlockSpec((1, 1, m, 2 * width), lambda l, b: (l, b, 0, 0))],
        out_shape=[out, out_v],
        compiler_params=_cparams(("parallel", "parallel")),
        name="mem_kv",
    )(mem, mem_norm_g, w_mem_kv, gk)


def _ma_kernel(q_ref, k_ref, v_ref, o_ref, *, bounded):
    for h in range(MA_HEADS):
        cols = slice(h * MA_HEAD_DIM, (h + 1) * MA_HEAD_DIM)
        sc = lax.dot_general(q_ref[0, :, cols], k_ref[0, :, cols], (((1,), (1,)), ((), ())),
                             preferred_element_type=F32)
        if bounded:
            oa = jnp.dot(jnp.exp2(sc).astype(BF16),
                         v_ref[0, :, 2 * h * MA_HEAD_DIM:(2 * h + 2) * MA_HEAD_DIM],
                         preferred_element_type=F32)
            o = oa[:, 0:MA_HEAD_DIM] * (1.0 / oa[:, MA_HEAD_DIM:MA_HEAD_DIM + 1])
        else:
            e = jnp.exp2(sc - jnp.max(sc, axis=-1, keepdims=True))
            p = (e / jnp.sum(e, axis=-1, keepdims=True)).astype(BF16)
            o = jnp.dot(p, v_ref[0, :, 2 * h * MA_HEAD_DIM:(2 * h + 1) * MA_HEAD_DIM],
                        preferred_element_type=F32)
        o_ref[0, :, cols] = o.astype(BF16)


def _memory_attention(u3, kmem, vmem, tq, bounded):
    nb, s, _ = u3.shape
    m = kmem.shape[1]
    width = MA_HEADS * MA_HEAD_DIM
    return pl.pallas_call(
        functools.partial(_ma_kernel, bounded=bounded),
        grid=(nb, s // tq),
        in_specs=[
            pl.BlockSpec((1, tq, width), lambda b, i: (b, i, COL_MA_Q // width)),
            pl.BlockSpec((1, m, width), lambda b, i: (b, 0, 0)),
            pl.BlockSpec((1, m, 2 * width), lambda b, i: (b, 0, 0)),
        ],
        out_specs=pl.BlockSpec((1, tq, width), lambda b, i: (b, i, 0)),
        out_shape=jax.ShapeDtypeStruct((nb, s, width), BF16),
        compiler_params=_cparams(("parallel", "parallel")),
        name="mem_attn_bounded" if bounded else "mem_attn_runmax",
    )(u3, kmem, vmem)


def _merge_kernel(x_ref, g1_ref, da_ref, cv_ref, wa_ref, ma_ref, wg_ref, wb_ref, wo_ref, o_ref):
    x = x_ref[...]
    h = _rms(x, g1_ref[...]).astype(BF16)
    merged = None
    for n, br in enumerate((da_ref, cv_ref, wa_ref, ma_ref)):
        logits = jnp.dot(h, wg_ref[:, n * D_MODEL:(n + 1) * D_MODEL], preferred_element_type=F32)
        t = jax.nn.sigmoid(logits) * jnp.dot(br[...], wb_ref[n], preferred_element_type=F32)
        merged = t if merged is None else merged + t
    o_ref[...] = x + jnp.dot(merged.astype(BF16), wo_ref[...], preferred_element_type=F32)


def _merge(x2d, g1, br_da, br_conv, br_wa, br_ma, w_gate, w_branch, w_out, tm):
    t = x2d.shape[0]
    br = pl.BlockSpec((tm, BRANCH_WIDTH), lambda i: (i, 0))
    return pl.pallas_call(
        _merge_kernel,
        grid=(t // tm,),
        in_specs=[
            pl.BlockSpec((tm, D_MODEL), lambda i: (i, 0)),
            _resident((1, D_MODEL), lambda i: (0, 0)),
            br, br, br, br,
            _resident((D_MODEL, N_BRANCHES * D_MODEL), lambda i: (0, 0)),
            _resident((N_BRANCHES, BRANCH_WIDTH, D_MODEL), lambda i: (0, 0, 0)),
            _resident((D_MODEL, D_MODEL), lambda i: (0, 0)),
        ],
        out_specs=pl.BlockSpec((tm, D_MODEL), lambda i: (i, 0)),
        out_shape=jax.ShapeDtypeStruct((t, D_MODEL), F32),
        compiler_params=_cparams(("parallel",)),
        name="merge",
    )(x2d, g1, br_da, br_conv, br_wa, br_ma, w_gate, w_branch, w_out)


FF_CHUNK = 1024


def _ffn_kernel(x_ref, g2_ref, w1_ref, w2_ref, o_ref):
    x = x_ref[...]
    h = _rms(x, g2_ref[...]).astype(BF16)
    acc = x
    for c in range(D_FF // FF_CHUNK):
        cols = slice(c * FF_CHUNK, (c + 1) * FF_CHUNK)
        f = jnp.maximum(jnp.dot(h, w1_ref[:, cols], preferred_element_type=F32), 0.0)
        acc = acc + jnp.dot((f * f).astype(BF16), w2_ref[cols, :], preferred_element_type=F32)
    o_ref[...] = acc


def _ffn(x2d, g2, w1, w2, tm):
    t = x2d.shape[0]
    return pl.pallas_call(
        _ffn_kernel,
        grid=(t // tm,),
        in_specs=[
            pl.BlockSpec((tm, D_MODEL), lambda i: (i, 0)),
            _resident((1, D_MODEL), lambda i: (0, 0)),
            _resident((D_MODEL, D_FF), lambda i: (0, 0)),
            _resident((D_FF, D_MODEL), lambda i: (0, 0)),
        ],
        out_specs=pl.BlockSpec((tm, D_MODEL), lambda i: (i, 0)),
        out_shape=jax.ShapeDtypeStruct((t, D_MODEL), F32),
        compiler_params=_cparams(("parallel",)),
        name="ffn",
    )(x2d, g2, w1, w2)


def _rel_bucket(rel):
    nb = REL_BUCKETS // 2
    max_exact = nb // 2
    ret = jnp.where(rel > 0, nb, 0)
    n = jnp.abs(rel)
    nf = jnp.maximum(n, 1).astype(F32)
    large = max_exact + (jnp.log(nf / max_exact) / math.log(REL_MAX_DIST / max_exact)
                         * (nb - max_exact)).astype(jnp.int32)
    large = jnp.minimum(large, nb - 1)
    return ret + jnp.where(n < max_exact, n, large)


def _lookup(table, bucket):
    out = jnp.zeros(bucket.shape + (table.shape[1],), F32)
    for b in range(REL_BUCKETS):
        out = jnp.where((bucket == b)[..., None], table[b], out)
    return out


def _da_bias_tables(rel_bias, tq, tk):
    table = rel_bias[:, :2 * DA_HEADS].astype(F32) * LOG2E
    kk = jnp.arange(tk)[None, :, None]
    qq = jnp.arange(tq)[None, None, :]
    dd = (jnp.arange(3) - 1)[:, None, None]
    vals = _lookup(table, _rel_bucket(dd * tk + kk - qq))
    vals = vals.reshape(3, tk, tq, DA_HEADS, 2).transpose(3, 0, 1, 4, 2)
    bias_t = vals.reshape(DA_HEADS, 3, tk, 2 * tq)
    bias_t = jnp.concatenate([jnp.zeros_like(bias_t[:, :1]), bias_t], axis=1)
    nbk = REL_BUCKETS // 2
    far = jnp.stack([table[nbk - 1], jnp.zeros_like(table[0]), table[REL_BUCKETS - 1]])
    far = far.reshape(3, DA_HEADS, 2).transpose(1, 0, 2)
    cfar = jnp.repeat(far, tq, axis=-1).reshape(DA_HEADS, 3, 1, 2 * tq)
    return bias_t, cfar, table


def _da_logits_bounded(table, gq, gk):
    bound = 1.02 * HEAD_DIM * jnp.max(jnp.abs(gq)) * jnp.max(jnp.abs(gk))
    return bound + jnp.max(jnp.abs(table)) <= MAX_LOGIT


def _wa_bias_table(rel_bias):
    table = rel_bias[:, 2 * DA_HEADS:].astype(F32) * LOG2E
    qoff = jnp.arange(BLOCK)
    koff = jnp.arange(3 * BLOCK) - BLOCK
    rel = koff[None, :] - qoff[:, None]
    bias = _lookup(table, _rel_bucket(rel)).transpose(2, 0, 1)
    return bias, jnp.abs(rel) <= WINDOW, table


def _wa_consts(bias, in_window, table, sink2, gq, gk):
    def tiles(b):
        b = jnp.where(in_window[None], b, NEG_INF)
        return b.reshape(WA_HEADS // 2, 2 * BLOCK, 3 * BLOCK)

    bound = 1.02 * HEAD_DIM * jnp.max(jnp.abs(gq)) * jnp.max(jnp.abs(gk))
    cmax, cmin = jnp.max(table, axis=0), jnp.min(table, axis=0)
    shift = jnp.maximum(bound + cmax, sink2)
    safe = jnp.max(shift - jnp.maximum(cmin - bound, sink2)) <= MAX_SHIFT_GAP
    return (tiles(bias), sink2), (tiles(bias - shift[:, None, None]), sink2 - shift), safe


def _dup_heads(w, heads, dim):
    w = w.reshape(w.shape[0], heads, 1, dim)
    return jnp.broadcast_to(w, (w.shape[0], heads, 2, dim)).reshape(w.shape[0], heads * 2 * dim)


def _layer_params(l, w_in, da_qk_g, wa_qk_g, ma_qk_g):
    w = w_in[l]
    wk = w[:, 3072:3200]
    wv = w[:, 3200:3328]
    w_ext = jnp.concatenate([
        w[:, 0:3072],
        _dup_heads(wk, WA_KV_HEADS, HEAD_DIM), _dup_heads(wv, WA_KV_HEADS, HEAD_DIM),
        w[:, 3328:3840]], axis=1).astype(BF16)
    ones = lambda n: jnp.ones((n,), F32)
    gq = da_qk_g[l, 0] * (HEAD_DIM ** -0.5 * LOG2E)
    gk = da_qk_g[l, 1]
    wq = wa_qk_g[l, 0] * (HEAD_DIM ** -0.5 * LOG2E)
    mq = ma_qk_g[l, 0] * (MA_HEAD_DIM ** -0.5 * LOG2E)
    gain = jnp.concatenate([
        jnp.tile(gq, 2 * DA_HEADS),
        jnp.tile(gk, 2 * DA_HEADS),
        ones(512 + 1024),
        jnp.tile(wq, WA_HEADS),
        jnp.tile(wa_qk_g[l, 1], 2 * WA_KV_HEADS),
        ones(256),
        jnp.tile(mq, MA_HEADS),
    ]).reshape(1, U_WIDTH).astype(F32)
    w_gate = w[:, GATE_START:].astype(BF16)
    return w_ext, gain, w_gate, (gq, gk), (wq, wa_qk_g[l, 1]), (mq, ma_qk_g[l, 1])


def _tile(n, pref):
    return pref if n % pref == 0 else n


def _trunk(x, mem, rel_bias, norm1_g, w_in, da_qk_g, da_lambda, da_subln_g, conv_w, conv_b,
           conv_ln_g, conv_ln_b, wa_qk_g, wa_sink, mem_norm_g, w_mem_kv, ma_qk_g, w_branch, w_out,
           norm2_g, w_ff1, w_ff2, *, tq=512, tm=512):
    nb, s, _ = x.shape
    tq = _tile(s, tq)
    tk = tq
    tm = _tile(nb * s, tm)
    tm_big = _tile(s, 2 * tm)
    nk = s // tk
    bias_t, cfar, da_table = _da_bias_tables(rel_bias, tq, tk)
    wa_tables = _wa_bias_table(rel_bias)
    kmem, vmem = _mem_kv(mem, mem_norm_g.reshape(DEPTH, 1, D_MODEL), w_mem_kv.astype(BF16),
                         ma_qk_g[:, 1].reshape(DEPTH, 1, MA_HEAD_DIM))
    x2d = x.reshape(nb * s, D_MODEL)
    for l in range(DEPTH):
        w_ext, gain, w_gate, da_g, wa_g, ma_g = _layer_params(l, w_in, da_qk_g, wa_qk_g, ma_qk_g)
        g1 = norm1_g[l].reshape(1, D_MODEL)
        u, vt = _inproj(x2d, g1, w_ext, gain, tm_big, nb, tk)
        u3 = u.reshape(nb, s, U_WIDTH)
        lam_init = 0.8 - 0.6 * math.exp(-0.3 * l)
        da_args = (da_lambda[l], da_subln_g[l].reshape(DA_VDIM, 1), lam_init, tq, tk)
        br_da = lax.cond(
            _da_logits_bounded(da_table, *da_g),
            lambda u3, vt: _diff_attention(u3, vt, bias_t, jnp.exp2(cfar), *da_args, True),
            lambda u3, vt: _diff_attention(u3, vt, bias_t, cfar, *da_args, False),
            u3, vt)
        br_conv = _conv_module(u3, conv_w[l], conv_b[l].reshape(1, CONV_CH),
                               conv_ln_g[l].reshape(1, CONV_CH), conv_ln_b[l].reshape(1, CONV_CH),
                               _tile(s, 512))
        wa_plain, wa_shifted, wa_safe = _wa_consts(*wa_tables, wa_sink[l].astype(F32) * LOG2E,
                                                   *wa_g)
        tw = _tile(s, 512)
        br_wa = lax.cond(
            wa_safe,
            lambda u3: _window_attention(u3, *wa_shifted, tw, True),
            lambda u3: _window_attention(u3, *wa_plain, tw, False),
            u3)
        tma = _tile(s, 1024)
        ma_bound = 1.02 * MA_HEAD_DIM * jnp.max(jnp.abs(ma_g[0])) * jnp.max(jnp.abs(ma_g[1]))
        br_ma = lax.cond(
            ma_bound <= MAX_LOGIT,
            lambda u3: _memory_attention(u3, kmem[l], vmem[l], tma, True),
            lambda u3: _memory_attention(u3, kmem[l], vmem[l], tma, False),
            u3)
        flat = lambda a: a.reshape(nb * s, BRANCH_WIDTH)
        x2d = _merge(x2d, g1, flat(br_da), flat(br_conv), flat(br_wa), flat(br_ma), w_gate,
                     w_branch[l].astype(BF16), w_out[l].astype(BF16), tm)
        x2d = _ffn(x2d, norm2_g[l].reshape(1, D_MODEL), w_ff1[l].astype(BF16),
                   w_ff2[l].astype(BF16), tm_big)
    return x2d.reshape(nb, s, D_MODEL)


def kernel(x_prompt, x_sample, mem_prompt, mem_sample, rel_bias, norm1_g, w_in, da_qk_g, da_lambda, da_subln_g, conv_w, conv_b, conv_ln_g, conv_ln_b, wa_qk_g, wa_sink, mem_norm_g, w_mem_kv, ma_qk_g, w_branch, w_out, norm2_g, w_ff1, w_ff2):
    params = (rel_bias, norm1_g, w_in, da_qk_g, da_lambda, da_subln_g, conv_w, conv_b, conv_ln_g,
              conv_ln_b, wa_qk_g, wa_sink, mem_norm_g, w_mem_kv, ma_qk_g, w_branch, w_out,
              norm2_g, w_ff1, w_ff2)
    return (_trunk(x_prompt, mem_prompt, *params), _trunk(x_sample, mem_sample, *params))
```
